```python
import math
import jax, jax.numpy as jnp
from jax import lax
import numpy as np

D_MODEL = 4096
BATCH = 4
SEQ = 4096
DEPTH = 2

MIX_WIDTH = D_MODEL
LRU_WIDTH = D_MODEL // 4
LRU_HEADS = 8
LRU_HEAD_DIM = LRU_WIDTH // LRU_HEADS
CONV_WIDTH = 4
LRU_C = 8.0
LRU_A_MIN = 0.9
LRU_A_MAX = 0.999
HGRN_HEADS = 8
HGRN_DK = 128
HGRN_DV = 128
HGRN_KEY_WIDTH = HGRN_HEADS * HGRN_DK
HGRN_VAL_WIDTH = HGRN_HEADS * HGRN_DV
HGRN_CHUNK = 64
DIFF_WIDTH = D_MODEL // 2
DIFF_HEADS = 8
DIFF_HEAD_DIM = DIFF_WIDTH // (2 * DIFF_HEADS)
Q_BLOCK = 128
ROPE_THETA = 10000.0
D_FF = 4 * D_MODEL
NORM_EPS = 1e-6

OFF_A_GATE = 0
OFF_A_X = OFF_A_GATE + LRU_WIDTH
OFF_B_Q = OFF_A_X + LRU_WIDTH
OFF_B_F = OFF_B_Q + HGRN_KEY_WIDTH
OFF_B_I = OFF_B_F + HGRN_KEY_WIDTH
OFF_B_G = OFF_B_I + HGRN_VAL_WIDTH
OFF_C_Q = OFF_B_G + HGRN_VAL_WIDTH
OFF_C_K = OFF_C_Q + DIFF_WIDTH
OFF_C_V = OFF_C_K + DIFF_WIDTH
IN_WIDTH = OFF_C_V + DIFF_WIDTH

kernel_name = "hymba_rglru_hgrn2_diffattn_hybrid"

F32 = jnp.float32


def rms_norm(x, gain, eps=NORM_EPS):
    xf = x.astype(F32)
    y = xf * lax.rsqrt(jnp.mean(xf * xf, axis=-1, keepdims=True) + eps)
    return (y * gain.astype(F32)).astype(x.dtype)


def rotary_tables(seq, dim):
    inv = 1.0 / (ROPE_THETA ** (jnp.arange(0, dim, 2, dtype=F32) / dim))
    ang = jnp.arange(seq, dtype=F32)[:, None] * inv[None, :]
    ang = jnp.concatenate([ang, ang], axis=-1)
    return jnp.cos(ang), jnp.sin(ang)


def apply_rope(x, cos, sin):
    x1, x2 = jnp.split(x, 2, axis=-1)
    rot = jnp.concatenate([-x2, x1], axis=-1)
    c = cos[None, :, None, None, :]
    s = sin[None, :, None, None, :]
    return (x.astype(F32) * c + rot.astype(F32) * s).astype(x.dtype)


def causal_depthwise_conv(x, w, b):
    c = x.shape[-1]
    y = lax.conv_general_dilated(
        x, w[:, None, :].astype(x.dtype), window_strides=(1,),
        padding=[(CONV_WIDTH - 1, 0)], dimension_numbers=("NWC", "WIO", "NWC"),
        feature_group_count=c)
    return y + b.astype(x.dtype)


def rg_lru(x, w_r, b_r, w_i, b_i, lam):
    bsz, seq, width = x.shape
    xh = x.reshape(bsz, seq, LRU_HEADS, LRU_HEAD_DIM)
    r = jax.nn.sigmoid(jnp.einsum("bshi,hij->bshj", xh, w_r) + b_r).reshape(bsz, seq, width)
    i = jax.nn.sigmoid(jnp.einsum("bshi,hij->bshj", xh, w_i) + b_i).reshape(bsz, seq, width)
    log_a = -LRU_C * r.astype(F32) * jax.nn.softplus(-lam.astype(F32))
    a = jnp.exp(log_a)
    mult = jnp.sqrt(-jnp.expm1(2.0 * log_a))
    mult = jnp.where((jnp.arange(seq) == 0)[None, :, None], 1.0, mult)
    u = mult * (i * x).astype(F32)

    def combine(c1, c2):
        a1, b1 = c1
        a2, b2 = c2
        return a1 * a2, a2 * b1 + b2

    _, h = lax.associative_scan(combine, (a, u), axis=1)
    return h.astype(x.dtype)


def hgrn2_chunkwise(q, k, v, log_f):
    bsz, seq, heads, dk = q.shape
    dv = v.shape[-1]
    n = seq // HGRN_CHUNK

    def to_chunks(t):
        return t.astype(F32).reshape(bsz, n, HGRN_CHUNK, heads, t.shape[-1]).transpose(1, 0, 3, 2, 4)

    qc, kc, vc, gc = to_chunks(q), to_chunks(k), to_chunks(v), to_chunks(log_f)
    causal = jnp.tril(jnp.ones((HGRN_CHUNK, HGRN_CHUNK), dtype=bool))

    def step(state, inp):
        qb, kb, vb, gb = inp
        g_cum = jnp.cumsum(gb, axis=2)
        diff = g_cum[:, :, :, None, :] - g_cum[:, :, None, :, :]
        decay = jnp.exp(jnp.where(causal[None, None, :, :, None], diff, -jnp.inf))
        scores = jnp.einsum("bhtd,bhsd,bhtsd->bhts", qb, kb, decay)
        o = (jnp.einsum("bhts,bhsv->bhtv", scores, vb)
             + jnp.einsum("bhtd,bhdv->bhtv", qb * jnp.exp(g_cum), state))
        g_last = g_cum[:, :, -1, :]
        new_state = (jnp.exp(g_last)[..., None] * state
                     + jnp.einsum("bhsd,bhsv->bhdv", kb * jnp.exp(g_last[:, :, None, :] - g_cum), vb))
        return new_state, o

    state0 = jnp.zeros((bsz, heads, dk, dv), F32)
    _, o = lax.scan(step, state0, (qc, kc, vc, gc))
    return o.transpose(1, 0, 3, 2, 4).reshape(bsz, seq, heads, dv)


def diff_attention(q, k, v, lam):
    bsz, seq, heads, _, d = q.shape
    nb = seq // Q_BLOCK
    qb = q.astype(F32).reshape(bsz, nb, Q_BLOCK, heads, 2, d).transpose(1, 0, 2, 3, 4, 5)
    kf = k.astype(F32)
    vf = v.astype(F32)
    kpos = jnp.arange(seq)
    scale = d ** -0.5

    def block(args):
        qi, bi = args
        s = jnp.einsum("bqhcd,bkhcd->bhcqk", qi, kf) * scale
        qpos = bi * Q_BLOCK + jnp.arange(Q_BLOCK)
        mask = kpos[None, :] <= qpos[:, None]
        p = jax.nn.softmax(jnp.where(mask, s, -jnp.inf), axis=-1)
        w = p[:, :, 0] - lam * p[:, :, 1]
        return jnp.einsum("bhqk,bkhv->bqhv", w, vf)

    o = lax.map(block, (qb, jnp.arange(nb)))
    return o.transpose(1, 0, 2, 3, 4).reshape(bsz, seq, heads, 2 * d)


def setup_inputs(seed: int = 0) -> dict:
    key = jax.random.key(seed)
    ks = jax.random.split(key, 24)

    def normal(k, shape, scale):
        return jax.random.normal(k, shape, F32) * scale

    def gain(k, shape):
        return 1.0 + 0.02 * jax.random.normal(k, shape, F32)

    x = normal(ks[0], (BATCH, SEQ, D_MODEL), 1.0)
    ln1_gain = gain(ks[1], (DEPTH, D_MODEL))
    w_in = normal(ks[2], (DEPTH, D_MODEL, IN_WIDTH), D_MODEL ** -0.5)
    conv_w = normal(ks[3], (DEPTH, CONV_WIDTH, LRU_WIDTH), CONV_WIDTH ** -0.5)
    conv_b = normal(ks[4], (DEPTH, LRU_WIDTH), 0.01)
    lru_w_r = normal(ks[5], (DEPTH, LRU_HEADS, LRU_HEAD_DIM, LRU_HEAD_DIM), LRU_HEAD_DIM ** -0.5)
    lru_b_r = normal(ks[6], (DEPTH, LRU_HEADS, LRU_HEAD_DIM), 0.01)
    lru_w_i = normal(ks[7], (DEPTH, LRU_HEADS, LRU_HEAD_DIM, LRU_HEAD_DIM), LRU_HEAD_DIM ** -0.5)
    lru_b_i = normal(ks[8], (DEPTH, LRU_HEADS, LRU_HEAD_DIM), 0.01)
    a_pow_c = jax.random.uniform(ks[9], (DEPTH, LRU_WIDTH), F32, LRU_A_MIN, LRU_A_MAX)
    a0 = a_pow_c ** (1.0 / LRU_C)
    lru_lambda = jnp.log(a0) - jnp.log1p(-a0)
    lru_norm_gain = gain(ks[10], (DEPTH, LRU_WIDTH))
    hgrn_lower_bounds = normal(ks[11], (DEPTH, HGRN_KEY_WIDTH), 0.1)
    hgrn_norm_gain = gain(ks[12], (DEPTH, HGRN_DV))
    q_norm_gain = gain(ks[13], (DEPTH, DIFF_HEAD_DIM))
    k_norm_gain = gain(ks[14], (DEPTH, DIFF_HEAD_DIM))
    diff_lambda = normal(ks[15], (DEPTH, 4, DIFF_HEAD_DIM), 0.1)
    diff_subln_gain = gain(ks[16], (DEPTH, 2 * DIFF_HEAD_DIM))
    w_out = normal(ks[17], (DEPTH, MIX_WIDTH, D_MODEL), MIX_WIDTH ** -0.5)
    ln2_gain = gain(ks[18], (DEPTH, D_MODEL))
    w_ff1 = normal(ks[19], (DEPTH, D_MODEL, D_FF), D_MODEL ** -0.5)
    w_ff2 = normal(ks[20], (DEPTH, D_FF, D_MODEL), D_FF ** -0.5)
    return {
        "x": x, "ln1_gain": ln1_gain, "w_in": w_in, "conv_w": conv_w, "conv_b": conv_b,
        "lru_w_r": lru_w_r, "lru_b_r": lru_b_r, "lru_w_i": lru_w_i, "lru_b_i": lru_b_i,
        "lru_lambda": lru_lambda, "lru_norm_gain": lru_norm_gain,
        "hgrn_lower_bounds": hgrn_lower_bounds, "hgrn_norm_gain": hgrn_norm_gain,
        "q_norm_gain": q_norm_gain, "k_norm_gain": k_norm_gain, "diff_lambda": diff_lambda,
        "diff_subln_gain": diff_subln_gain, "w_out": w_out, "ln2_gain": ln2_gain,
        "w_ff1": w_ff1, "w_ff2": w_ff2,
    }


def reference(x, ln1_gain, w_in, conv_w, conv_b, lru_w_r, lru_b_r, lru_w_i, lru_b_i,
              lru_lambda, lru_norm_gain, hgrn_lower_bounds, hgrn_norm_gain, q_norm_gain,
              k_norm_gain, diff_lambda, diff_subln_gain, w_out, ln2_gain, w_ff1, w_ff2):
    bsz, seq, _ = x.shape
    dt = x.dtype
    cos, sin = rotary_tables(seq, DIFF_HEAD_DIM)
    lb_all = jnp.cumsum(jax.nn.softmax(hgrn_lower_bounds.astype(F32), axis=0), axis=0)
    lb_all = lb_all - lb_all[0:1]

    for l in range(DEPTH):
        h = rms_norm(x, ln1_gain[l])
        z = h @ w_in[l]

        gate_a = jax.nn.gelu(z[..., OFF_A_GATE:OFF_A_GATE + LRU_WIDTH])
        xa = causal_depthwise_conv(z[..., OFF_A_X:OFF_A_X + LRU_WIDTH], conv_w[l], conv_b[l])
        ya = rg_lru(xa, lru_w_r[l], lru_b_r[l], lru_w_i[l], lru_b_i[l], lru_lambda[l]) * gate_a
        ya = rms_norm(ya, lru_norm_gain[l]).astype(dt)

        qB = jax.nn.silu(z[..., OFF_B_Q:OFF_B_Q + HGRN_KEY_WIDTH].astype(F32))
        qB = qB.reshape(bsz, seq, HGRN_HEADS, HGRN_DK) * (HGRN_DK ** -0.5)
        zf = z[..., OFF_B_F:OFF_B_F + HGRN_KEY_WIDTH].astype(F32).reshape(bsz, seq, HGRN_HEADS, HGRN_DK)
        lb = lb_all[l].reshape(HGRN_HEADS, HGRN_DK)
        log_f = jnp.logaddexp(jnp.log(lb), jnp.log1p(-lb) + jax.nn.log_sigmoid(zf))
        kB = (1.0 - lb) * jax.nn.sigmoid(-zf)
        vB = z[..., OFF_B_I:OFF_B_I + HGRN_VAL_WIDTH].reshape(bsz, seq, HGRN_HEADS, HGRN_DV)
        gB = z[..., OFF_B_G:OFF_B_G + HGRN_VAL_WIDTH].reshape(bsz, seq, HGRN_HEADS, HGRN_DV)
        oB = hgrn2_chunkwise(qB, kB, vB, log_f)
        yb = rms_norm(oB, hgrn_norm_gain[l]) * jax.nn.silu(gB.astype(F32))
        yb = yb.reshape(bsz, seq, HGRN_VAL_WIDTH).astype(dt)

        qC = z[..., OFF_C_Q:OFF_C_Q + DIFF_WIDTH].reshape(bsz, seq, DIFF_HEADS, 2, DIFF_HEAD_DIM)
        kC = z[..., OFF_C_K:OFF_C_K + DIFF_WIDTH].reshape(bsz, seq, DIFF_HEADS, 2, DIFF_HEAD_DIM)
        vC = z[..., OFF_C_V:OFF_C_V + DIFF_WIDTH].reshape(bsz, seq, DIFF_HEADS, 2 * DIFF_HEAD_DIM)
        qC = apply_rope(rms_norm(qC, q_norm_gain[l]), cos, sin)
        kC = apply_rope(rms_norm(kC, k_norm_gain[l]), cos, sin)
        lam_init = 0.8 - 0.6 * math.exp(-0.3 * l)
        lp = diff_lambda[l].astype(F32)
        lam = jnp.exp(jnp.sum(lp[0] * lp[1])) - jnp.exp(jnp.sum(lp[2] * lp[3])) + lam_init
        oC = diff_attention(qC, kC, vC, lam)
        yc = rms_norm(oC, diff_subln_gain[l]) * (1.0 - lam_init)
        yc = yc.reshape(bsz, seq, DIFF_WIDTH).astype(dt)

        y = jnp.concatenate([ya, yb, yc], axis=-1) @ w_out[l]
        x = x + y.astype(dt)

        h2 = rms_norm(x, ln2_gain[l])
        x = x + (jnp.square(jax.nn.relu(h2 @ w_ff1[l])) @ w_ff2[l]).astype(dt)
    return x
```

```python
import functools
import math

import jax
import jax.numpy as jnp
from jax import lax
from jax.experimental import pallas as pl
from jax.experimental.pallas import tpu as pltpu

F32 = jnp.float32
BF16 = jnp.bfloat16

NORM_EPS = 1e-6
LRU_C = 8.0
ROPE_THETA = 10000.0
CONV_WIDTH = 4
HEAD_DIM = 128
HGRN_CHUNK = 128
HGRN_SUB = 16
VMEM_LIMIT = 56 * 1024 * 1024


def _params(semantics):
    return pltpu.CompilerParams(dimension_semantics=semantics, vmem_limit_bytes=VMEM_LIMIT)


def _sigmoid(x):
    return 1.0 / (1.0 + jnp.exp(-x))


def _log_sigmoid(x):
    return jnp.minimum(x, 0.0) - jnp.log1p(jnp.exp(-jnp.abs(x)))


def _rmsnorm_kernel(x_ref, g_ref, o_ref):
    x = x_ref[...]
    ms = jnp.mean(x * x, axis=-1, keepdims=True)
    o_ref[...] = (x * lax.rsqrt(ms + NORM_EPS) * g_ref[...]).astype(o_ref.dtype)


def _rmsnorm(x, gain, tm=256):
    n, d = x.shape
    return pl.pallas_call(
        _rmsnorm_kernel,
        grid=(n // tm,),
        in_specs=[pl.BlockSpec((tm, d), lambda i: (i, 0)),
                  pl.BlockSpec((1, d), lambda i: (0, 0))],
        out_specs=pl.BlockSpec((tm, d), lambda i: (i, 0)),
        out_shape=jax.ShapeDtypeStruct((n, d), BF16),
        compiler_params=_params(("parallel",)),
        name="rmsnorm",
    )(x, gain.reshape(1, d))


def _mm_kernel(*refs, nk, mode):
    if mode == "residual":
        a_ref, w_ref, r_ref, o_ref, acc_ref = refs
    else:
        a_ref, w_ref, o_ref, acc_ref = refs
        r_ref = None
    k = pl.program_id(2)

    @pl.when(k == 0)
    def _():
        acc_ref[...] = jnp.zeros_like(acc_ref)

    acc_ref[...] += jnp.dot(a_ref[...], w_ref[...], preferred_element_type=F32)

    @pl.when(k == nk - 1)
    def _():
        acc = acc_ref[...]
        if mode == "relu2":
            acc = jnp.square(jnp.maximum(acc, 0.0))
        elif mode == "residual":
            acc = acc + r_ref[...]
        o_ref[...] = acc.astype(o_ref.dtype)


def _matmul(a, w, *, mode="plain", residual=None, out_dtype=F32, tm=1024, tn=512, tk=1024):
    m, kdim = a.shape
    _, n = w.shape
    tm, tn, tk = min(tm, m), min(tn, n), min(tk, kdim)
    nk = kdim // tk
    in_specs = [pl.BlockSpec((tm, tk), lambda i, j, k: (i, k)),
                pl.BlockSpec((tk, tn), lambda i, j, k: (k, j))]
    args = [a, w]
    if mode == "residual":
        in_specs.append(pl.BlockSpec((tm, tn), lambda i, j, k: (i, j)))
        args.append(residual)
    return pl.pallas_call(
        functools.partial(_mm_kernel, nk=nk, mode=mode),
        grid=(m // tm, n // tn, nk),
        in_specs=in_specs,
        out_specs=pl.BlockSpec((tm, tn), lambda i, j, k: (i, j)),
        out_shape=jax.ShapeDtypeStruct((m, n), out_dtype),
        scratch_shapes=[pltpu.VMEM((tm, tn), F32)],
        compiler_params=_params(("parallel", "parallel", "arbitrary")),
        name="matmul_" + mode,
    )(*args)


def _lru_kernel(zg_ref, zx_ref, cw_ref, cb_ref, wr_ref, br_ref, wi_ref, bi_ref, lam_ref, ng_ref,
                o_ref, xprev_ref, hprev_ref, *, rows, heads):
    s = pl.program_id(1)
    width = heads * HEAD_DIM

    @pl.when(s == 0)
    def _():
        xprev_ref[...] = jnp.zeros_like(xprev_ref)
        hprev_ref[...] = jnp.zeros_like(hprev_ref)

    xz = zx_ref[...].astype(F32)
    prev8 = xprev_ref[...]
    row8 = lax.broadcasted_iota(jnp.int32, (8, width), 0)
    cw = cw_ref[...]
    xa = cw[CONV_WIDTH - 1:CONV_WIDTH] * xz + cb_ref[...]
    for d in range(1, CONV_WIDTH):
        rolled = pltpu.roll(xz, d, 0)
        head = jnp.where(row8 < d, pltpu.roll(prev8, d, 0), rolled[0:8])
        shifted = jnp.concatenate([head, rolled[8:]], axis=0)
        xa = xa + cw[CONV_WIDTH - 1 - d:CONV_WIDTH - d] * shifted
    xprev_ref[...] = xz[rows - 8:rows]

    xab = xa.astype(BF16)
    r_parts, i_parts = [], []
    for h in range(heads):
        xh = xab[:, h * HEAD_DIM:(h + 1) * HEAD_DIM]
        r_parts.append(jnp.dot(xh, wr_ref[h], preferred_element_type=F32))
        i_parts.append(jnp.dot(xh, wi_ref[h], preferred_element_type=F32))
    r = _sigmoid(jnp.concatenate(r_parts, axis=1) + br_ref[...])
    gate_i = _sigmoid(jnp.concatenate(i_parts, axis=1) + bi_ref[...])

    lam = lam_ref[...]
    softplus_neg_lam = jnp.maximum(-lam, 0.0) + jnp.log1p(jnp.exp(-jnp.abs(lam)))
    log_a = (-LRU_C) * r * softplus_neg_lam
    a = jnp.exp(log_a)
    mult = jnp.sqrt(1.0 - jnp.exp(2.0 * log_a))
    row = lax.broadcasted_iota(jnp.int32, (rows, width), 0)
    mult = jnp.where(jnp.logical_and(row == 0, s == 0), 1.0, mult)
    u = mult * (gate_i * xa)

    d = 1
    while d < rows:
        a_sh = pltpu.roll(a, d, 0)
        u_sh = pltpu.roll(u, d, 0)
        valid = row >= d
        u = jnp.where(valid, u + a * u_sh, u)
        a = jnp.where(valid, a * a_sh, a)
        d *= 2
    h = u + a * hprev_ref[0:1, :]
    hprev_ref[...] = jnp.broadcast_to(h[rows - 1:rows], hprev_ref.shape)

    zg = zg_ref[...].astype(F32)
    gelu = 0.5 * zg * (1.0 + jnp.tanh(math.sqrt(2.0 / math.pi) * (zg + 0.044715 * (zg * zg * zg))))
    y = h * gelu
    ms = jnp.mean(y * y, axis=-1, keepdims=True)
    o_ref[...] = (y * lax.rsqrt(ms + NORM_EPS) * ng_ref[...]).astype(o_ref.dtype)


def _lru(z, conv_w, conv_b, w_r, b_r, w_i, b_i, lam, norm_gain, *, batch, seq, rows=256):
    heads = w_r.shape[0]
    width = heads * HEAD_DIM
    nblk = seq // rows
    row_spec = lambda c: pl.BlockSpec((rows, width), lambda b, s, c=c: (b * nblk + s, c))
    vec = pl.BlockSpec((1, width), lambda b, s: (0, 0))
    wspec = pl.BlockSpec((heads, HEAD_DIM, HEAD_DIM), lambda b, s: (0, 0, 0))
    return pl.pallas_call(
        functools.partial(_lru_kernel, rows=rows, heads=heads),
        grid=(batch, nblk),
        in_specs=[row_spec(0), row_spec(1),
                  pl.BlockSpec((CONV_WIDTH, width), lambda b, s: (0, 0)), vec,
                  wspec, vec, wspec, vec, vec, vec],
        out_specs=pl.BlockSpec((rows, width), lambda b, s: (b * nblk + s, 0)),
        out_shape=jax.ShapeDtypeStruct((batch * seq, width), BF16),
        scratch_shapes=[pltpu.VMEM((8, width), F32), pltpu.VMEM((8, width), F32)],
        compiler_params=_params(("parallel", "arbitrary")),
        name="rg_lru",
    )(z, z, conv_w, conv_b.reshape(1, width), w_r.astype(BF16), b_r.reshape(1, width),
      w_i.astype(BF16), b_i.reshape(1, width), lam.reshape(1, width), norm_gain.reshape(1, width))


def _split3(x):
    hi = x.astype(BF16)
    r1 = x - hi.astype(F32)
    mid = r1.astype(BF16)
    lo = (r1 - mid.astype(F32)).astype(BF16)
    return hi, mid, lo


def _hgrn_kernel(zq_ref, zf_ref, zv_ref, zg_ref, lbp_ref, ng_ref, o_ref,
                 q_s, k_s, g_s, o_s, st_s, *, rows, layer):
    s = pl.program_id(2)
    C, c = HGRN_CHUNK, HGRN_SUB
    D = HEAD_DIM

    @pl.when(s == 0)
    def _():
        st_s[...] = jnp.zeros_like(st_s)

    lbp = lbp_ref[...].astype(F32)
    e = jnp.exp(lbp - jnp.max(lbp, axis=0, keepdims=True))
    sm = e / jnp.sum(e, axis=0, keepdims=True)
    lb = jnp.zeros((1, D), F32)
    for r_ in range(1, layer + 1):
        lb = lb + sm[r_:r_ + 1]

    zf = zf_ref[...].astype(F32)
    zq = zq_ref[...].astype(F32)
    q_s[...] = zq * _sigmoid(zq) * (D ** -0.5)
    k_s[...] = (1.0 - lb) * _sigmoid(-zf)
    y = jnp.log1p(-lb) + _log_sigmoid(zf)
    if layer == 0:
        log_f = y
    else:
        la = jnp.log(lb)
        log_f = jnp.maximum(la, y) + jnp.log1p(jnp.exp(-jnp.abs(la - y)))

    tri = (lax.broadcasted_iota(jnp.int32, (C, C), 0)
           >= lax.broadcasted_iota(jnp.int32, (C, C), 1)).astype(BF16)
    rid = lax.broadcasted_iota(jnp.int32, (c, D), 0)

    for ch in range(rows // C):
        base = ch * C
        hi, mid, lo = _split3(log_f[base:base + C])
        g3 = jnp.dot(tri, jnp.concatenate([hi, mid, lo], axis=1), preferred_element_type=F32)
        G = g3[:, 0:D] + g3[:, D:2 * D] + g3[:, 2 * D:3 * D]
        g_s[...] = G
        q = q_s[base:base + C]
        k = k_s[base:base + C]
        v = zv_ref[base:base + C, :].astype(F32)
        vb = v.astype(BF16)
        g_last = G[C - 1:C]

        st = st_s[...]
        qg = (q * jnp.exp(G)).astype(BF16)
        o_s[...] = lax.dot_general(qg, st.astype(BF16), (((1,), (1,)), ((), ())),
                                   preferred_element_type=F32)
        kd = (k * jnp.exp(g_last - G)).astype(BF16)
        st_s[...] = st * jnp.exp(g_last) + lax.dot_general(
            vb, kd, (((0,), (0,)), ((), ())), preferred_element_type=F32)

        m = C // 2
        while m >= c:
            for p in range(C // (2 * m)):
                lo0, hi0 = 2 * p * m, 2 * p * m + m
                g_ref_row = G[hi0 - 1:hi0]
                qh = (q[hi0:hi0 + m] * jnp.exp(G[hi0:hi0 + m] - g_ref_row)).astype(BF16)
                kl = (k[lo0:lo0 + m] * jnp.exp(g_ref_row - G[lo0:lo0 + m])).astype(BF16)
                sc = lax.dot_general(qh, kl, (((1,), (1,)), ((), ())), preferred_element_type=F32)
                o_s[hi0:hi0 + m, :] += jnp.dot(sc.astype(BF16), vb[lo0:lo0 + m],
                                               preferred_element_type=F32)
            m //= 2

        def diag_block(b, carry):
            r0 = pl.multiple_of(b * c, c)
            gb = g_s[pl.ds(r0, c), :]
            qb = q_s[pl.ds(base + r0, c), :]
            acc = jnp.zeros((c, D), F32)
            for j in range(c):
                gj = g_s[pl.ds(r0 + j, 1), :]
                kj = k_s[pl.ds(base + r0 + j, 1), :]
                vj = zv_ref[pl.ds(base + r0 + j, 1), :].astype(F32)
                w = qb * jnp.exp(jnp.minimum(gb - gj, 0.0)) * kj
                col = jnp.sum(w, axis=-1, keepdims=True)
                acc = acc + jnp.where(rid >= j, col, 0.0) * vj
            o_s[pl.ds(r0, c), :] += acc
            return carry

        lax.fori_loop(0, C // c, diag_block, 0)

        o = o_s[...]
        ms = jnp.mean(o * o, axis=-1, keepdims=True)
        zg = zg_ref[base:base + C, :].astype(F32)
        o_ref[base:base + C, :] = (o * lax.rsqrt(ms + NORM_EPS) * ng_ref[...]
                                   * (zg * _sigmoid(zg))).astype(o_ref.dtype)


def _hgrn(z, lower_bounds, norm_gain, *, layer, batch, seq, heads, col0, rows=256):
    nblk = seq // rows
    D = HEAD_DIM
    depth = lower_bounds.shape[0]

    def zspec(group):
        off = col0 // D + group * heads
        return pl.BlockSpec((rows, D), lambda b, h, s, off=off: (b * nblk + s, off + h))

    return pl.pallas_call(
        functools.partial(_hgrn_kernel, rows=rows, layer=layer),
        grid=(batch, heads, nblk),
        in_specs=[zspec(0), zspec(1), zspec(2), zspec(3),
                  pl.BlockSpec((depth, D), lambda b, h, s: (0, h)),
                  pl.BlockSpec((1, D), lambda b, h, s: (0, 0))],
        out_specs=pl.BlockSpec((rows, D), lambda b, h, s: (b * nblk + s, h)),
        out_shape=jax.ShapeDtypeStruct((batch * seq, heads * D), BF16),
        scratch_shapes=[pltpu.VMEM((rows, D), F32), pltpu.VMEM((rows, D), F32),
                        pltpu.VMEM((HGRN_CHUNK, D), F32), pltpu.VMEM((HGRN_CHUNK, D), F32),
                        pltpu.VMEM((D, D), F32)],
        compiler_params=_params(("parallel", "parallel", "arbitrary")),
        name="hgrn2",
    )(z, z, z, z, lower_bounds, norm_gain.reshape(1, D))


def _qk_prep_kernel(zq_ref, zk_ref, zv_ref, cos_ref, sin_ref, qg_ref, kg_ref,
                    q_ref, k_ref, v_ref, *, nsub):
    D = HEAD_DIM
    cos = cos_ref[...]
    sin = sin_ref[...]

    def norm_rope(x, gain, scale):
        ms = jnp.mean(x * x, axis=-1, keepdims=True)
        xn = x * lax.rsqrt(ms + NORM_EPS) * gain
        return (xn * cos + pltpu.roll(xn, D // 2, 1) * sin) * scale

    for c in range(nsub):
        sl = slice(c * D, (c + 1) * D)
        q_ref[:, sl] = norm_rope(zq_ref[:, sl].astype(F32), qg_ref[...], D ** -0.5).astype(q_ref.dtype)
        k_ref[:, sl] = norm_rope(zk_ref[:, sl].astype(F32), kg_ref[...], 1.0).astype(k_ref.dtype)
    v_ref[...] = zv_ref[...].astype(v_ref.dtype)


def _qk_prep(z, cos, sin_signed, q_gain, k_gain, *, batch, seq, width, col0, rows=256):
    nblk = seq // rows
    D = HEAD_DIM
    cb = col0 // width
    zspec = lambda g: pl.BlockSpec((rows, width), lambda i, g=g: (i, cb + g))
    tab = pl.BlockSpec((rows, D), lambda i: (i % nblk, 0))
    vec = pl.BlockSpec((1, D), lambda i: (0, 0))
    out = pl.BlockSpec((rows, width), lambda i: (i, 0))
    shp = jax.ShapeDtypeStruct((batch * seq, width), BF16)
    return pl.pallas_call(
        functools.partial(_qk_prep_kernel, nsub=width // D),
        grid=(batch * nblk,),
        in_specs=[zspec(0), zspec(1), zspec(2), tab, tab, vec, vec],
        out_specs=[out, out, out],
        out_shape=[shp, shp, shp],
        compiler_params=_params(("parallel",)),
        name="qk_prep",
    )(z, z, z, cos, sin_signed, q_gain.reshape(1, D), k_gain.reshape(1, D))


def _flash_kernel(q_ref, k_ref, v_ref, lp_ref, g_ref, o_ref, m_s, l_s, acc_s, *, blk, lam_init):
    i = pl.program_id(2)
    j = pl.program_id(3)
    D = HEAD_DIM

    @pl.when(j == 0)
    def _():
        m_s[...] = jnp.full_like(m_s, -jnp.inf)
        l_s[...] = jnp.zeros_like(l_s)
        acc_s[...] = jnp.zeros_like(acc_s)

    def step(masked):
        v = v_ref[...]
        for c in range(2):
            q = q_ref[:, c * D:(c + 1) * D]
            k = k_ref[:, c * D:(c + 1) * D]
            s = lax.dot_general(q, k, (((1,), (1,)), ((), ())), preferred_element_type=F32)
            if masked:
                row = lax.broadcasted_iota(jnp.int32, (blk, blk), 0)
                col = lax.broadcasted_iota(jnp.int32, (blk, blk), 1)
                s = jnp.where(col <= row, s, -jnp.inf)
            m_prev = m_s[c]
            m_new = jnp.maximum(m_prev, jnp.max(s, axis=-1, keepdims=True))
            alpha = jnp.exp(m_prev - m_new)
            p = jnp.exp(s - m_new)
            l_s[c] = alpha * l_s[c] + jnp.sum(p, axis=-1, keepdims=True)
            acc_s[c] = alpha * acc_s[c] + jnp.dot(p.astype(BF16), v, preferred_element_type=F32)
            m_s[c] = m_new

    @pl.when(j < i)
    def _():
        step(False)

    @pl.when(j == i)
    def _():
        step(True)
        lp = lp_ref[...].astype(F32)
        lam = (jnp.exp(jnp.sum(lp[0:1] * lp[1:2], axis=-1, keepdims=True))
               - jnp.exp(jnp.sum(lp[2:3] * lp[3:4], axis=-1, keepdims=True)) + lam_init)
        o = acc_s[0] / l_s[0] - lam * (acc_s[1] / l_s[1])
        ms = jnp.mean(o * o, axis=-1, keepdims=True)
        o_ref[...] = (o * lax.rsqrt(ms + NORM_EPS) * g_ref[...] * (1.0 - lam_init)).astype(o_ref.dtype)


def _flash(q, k, v, lam_params, subln_gain, *, layer, batch, seq, heads, blk=512):
    nblk = seq // blk
    D = HEAD_DIM
    lam_init = 0.8 - 0.6 * math.exp(-0.3 * layer)
    qspec = pl.BlockSpec((blk, 2 * D), lambda b, h, i, j: (b * nblk + i, h))
    kvspec = pl.BlockSpec((blk, 2 * D), lambda b, h, i, j: (b * nblk + jnp.minimum(j, i), h))
    return pl.pallas_call(
        functools.partial(_flash_kernel, blk=blk, lam_init=lam_init),
        grid=(batch, heads, nblk, nblk),
        in_specs=[qspec, kvspec, kvspec,
                  pl.BlockSpec((4, D), lambda b, h, i, j: (0, 0)),
                  pl.BlockSpec((1, 2 * D), lambda b, h, i, j: (0, 0))],
        out_specs=qspec,
        out_shape=jax.ShapeDtypeStruct((batch * seq, heads * 2 * D), BF16),
        scratch_shapes=[pltpu.VMEM((2, blk, 1), F32), pltpu.VMEM((2, blk, 1), F32),
                        pltpu.VMEM((2, blk, 2 * D), F32)],
        compiler_params=_params(("parallel", "parallel", "parallel", "arbitrary")),
        name="diff_flash",
    )(q, k, v, lam_params, subln_gain.reshape(1, 2 * D))


def kernel(x, ln1_gain, w_in, conv_w, conv_b, lru_w_r, lru_b_r, lru_w_i, lru_b_i, lru_lambda, lru_norm_gain, hgrn_lower_bounds, hgrn_norm_gain, q_norm_gain, k_norm_gain, diff_lambda, diff_subln_gain, w_out, ln2_gain, w_ff1, w_ff2):
    batch, seq, d_model = x.shape
    depth = w_in.shape[0]
    D = HEAD_DIM
    lru_width = conv_w.shape[-1]
    key_width = hgrn_lower_bounds.shape[-1]
    in_width = w_in.shape[-1]
    mix_width = w_out.shape[1]
    diff_width = in_width - 2 * key_width - 2 * mix_width
    val_width = mix_width - lru_width - diff_width
    assert q_norm_gain.shape[-1] == D and hgrn_norm_gain.shape[-1] == D
    assert lru_w_r.shape[-1] == D and key_width == val_width
    hgrn_heads = key_width // D
    diff_heads = diff_width // (2 * D)
    off_b = 2 * lru_width
    off_c = off_b + 2 * key_width + 2 * val_width

    inv = 1.0 / (ROPE_THETA ** (jnp.arange(0, D, 2, dtype=F32) / D))
    ang = jnp.arange(seq, dtype=F32)[:, None] * inv[None, :]
    ang = jnp.concatenate([ang, ang], axis=-1)
    cos = jnp.cos(ang)
    sign = jnp.concatenate([-jnp.ones((D // 2,), F32), jnp.ones((D // 2,), F32)])
    sin_signed = jnp.sin(ang) * sign[None, :]

    xf = x.reshape(batch * seq, d_model)
    for l in range(depth):
        h = _rmsnorm(xf, ln1_gain[l])
        z = _matmul(h, w_in[l].astype(BF16), out_dtype=F32)

        ya = _lru(z, conv_w[l], conv_b[l], lru_w_r[l], lru_b_r[l], lru_w_i[l], lru_b_i[l],
                  lru_lambda[l], lru_norm_gain[l], batch=batch, seq=seq)
        yb = _hgrn(z, hgrn_lower_bounds, hgrn_norm_gain[l], layer=l, batch=batch, seq=seq,
                   heads=hgrn_heads, col0=off_b)
        qn, kn, vn = _qk_prep(z, cos, sin_signed, q_norm_gain[l], k_norm_gain[l],
                              batch=batch, seq=seq, width=diff_width, col0=off_c)
        yc = _flash(qn, kn, vn, diff_lambda[l], diff_subln_gain[l], layer=l, batch=batch, seq=seq,
                    heads=diff_heads)

        y = jnp.concatenate([ya, yb, yc], axis=-1)
        xf = _matmul(y, w_out[l].astype(BF16), mode="residual", residual=xf)

        h2 = _rmsnorm(xf, ln2_gain[l])
        u = _matmul(h2, w_ff1[l].astype(BF16), mode="relu2", out_dtype=BF16)
        xf = _matmul(u, w_ff2[l].astype(BF16), mode="residual", residual=xf)
    return xf.reshape(batch, seq, d_model)
```

```python
import functools
import math

import jax
import jax.numpy as jnp
from jax import lax
from jax.experimental import pallas as pl
from jax.experimental.pallas import tpu as pltpu

F32 = jnp.float32
BF16 = jnp.bfloat16

NORM_EPS = 1e-6
LRU_C = 8.0
ROPE_THETA = 10000.0
CONV_WIDTH = 4
HEAD_DIM = 128
HGRN_CHUNK = 128
HGRN_SUB = 16
VMEM_LIMIT = 56 * 1024 * 1024


def _params(semantics):
    return pltpu.CompilerParams(dimension_semantics=semantics, vmem_limit_bytes=VMEM_LIMIT)


def _sigmoid(x):
    return 1.0 / (1.0 + jnp.exp(-x))


def _log_sigmoid(x):
    return jnp.minimum(x, 0.0) - jnp.log1p(jnp.exp(-jnp.abs(x)))


def _rmsnorm_kernel(x_ref, g_ref, o_ref):
    x = x_ref[...]
    ms = jnp.mean(x * x, axis=-1, keepdims=True)
    o_ref[...] = (x * lax.rsqrt(ms + NORM_EPS) * g_ref[...]).astype(o_ref.dtype)


def _rmsnorm(x, gain, tm=256):
    n, d = x.shape
    return pl.pallas_call(
        _rmsnorm_kernel,
        grid=(n // tm,),
        in_specs=[pl.BlockSpec((tm, d), lambda i: (i, 0)),
                  pl.BlockSpec((1, d), lambda i: (0, 0))],
        out_specs=pl.BlockSpec((tm, d), lambda i: (i, 0)),
        out_shape=jax.ShapeDtypeStruct((n, d), BF16),
        compiler_params=_params(("parallel",)),
        name="rmsnorm",
    )(x, gain.reshape(1, d))


def _mm_kernel(*refs, nk, mode):
    if mode == "residual":
        a_ref, w_ref, r_ref, o_ref, acc_ref = refs
    else:
        a_ref, w_ref, o_ref, acc_ref = refs
        r_ref = None
    k = pl.program_id(2)

    @pl.when(k == 0)
    def _():
        acc_ref[...] = jnp.zeros_like(acc_ref)

    acc_ref[...] += jnp.dot(a_ref[...], w_ref[...], preferred_element_type=F32)

    @pl.when(k == nk - 1)
    def _():
        o_ref[...] = _epilogue(acc_ref[...], mode, r_ref).astype(o_ref.dtype)


def _epilogue(acc, mode, r_ref):
    if mode == "relu2":
        acc = jnp.square(jnp.maximum(acc, 0.0))
    elif mode == "residual":
        acc = acc + r_ref[...]
    return acc


def _mm_fullk_kernel(*refs, mode):
    if mode == "residual":
        a_ref, w_ref, r_ref, o_ref = refs
    else:
        a_ref, w_ref, o_ref = refs
        r_ref = None
    acc = jnp.dot(a_ref[...], w_ref[...], preferred_element_type=F32)
    o_ref[...] = _epilogue(acc, mode, r_ref).astype(o_ref.dtype)


def _matmul_fullk(a, w, *, mode, residual, out_dtype, tm, tn):
    m, kdim = a.shape
    _, n = w.shape
    in_specs = [pl.BlockSpec((tm, kdim), lambda i, j: (i, 0)),
                pl.BlockSpec((kdim, tn), lambda i, j: (0, j))]
    args = [a, w]
    if mode == "residual":
        in_specs.append(pl.BlockSpec((tm, tn), lambda i, j: (i, j)))
        args.append(residual)
    return pl.pallas_call(
        functools.partial(_mm_fullk_kernel, mode=mode),
        grid=(m // tm, n // tn),
        in_specs=in_specs,
        out_specs=pl.BlockSpec((tm, tn), lambda i, j: (i, j)),
        out_shape=jax.ShapeDtypeStruct((m, n), out_dtype),
        compiler_params=_params(("parallel", "arbitrary")),
        name="matmul_fullk_" + mode,
    )(*args)


def _matmul(a, w, *, mode="plain", residual=None, out_dtype=F32, tm=1024, tn=1024, tk=2048):
    m, kdim = a.shape
    _, n = w.shape
    tm, tn, tk = min(tm, m), min(tn, n), min(tk, kdim)
    if kdim <= 4096:
        return _matmul_fullk(a, w, mode=mode, residual=residual, out_dtype=out_dtype, tm=tm, tn=tn)
    nk = kdim // tk
    in_specs = [pl.BlockSpec((tm, tk), lambda i, j, k: (i, k)),
                pl.BlockSpec((tk, tn), lambda i, j, k: (k, j))]
    args = [a, w]
    if mode == "residual":
        in_specs.append(pl.BlockSpec((tm, tn), lambda i, j, k: (i, j)))
        args.append(residual)
    return pl.pallas_call(
        functools.partial(_mm_kernel, nk=nk, mode=mode),
        grid=(m // tm, n // tn, nk),
        in_specs=in_specs,
        out_specs=pl.BlockSpec((tm, tn), lambda i, j, k: (i, j)),
        out_shape=jax.ShapeDtypeStruct((m, n), out_dtype),
        scratch_shapes=[pltpu.VMEM((tm, tn), F32)],
        compiler_params=_params(("parallel", "parallel", "arbitrary")),
        name="matmul_" + mode,
    )(*args)


def _lru_kernel(zg_ref, zx_ref, cw_ref, cb_ref, wr_ref, br_ref, wi_ref, bi_ref, lam_ref, ng_ref,
                o_ref, xprev_ref, hprev_ref, *, rows, heads):
    s = pl.program_id(1)
    width = heads * HEAD_DIM

    @pl.when(s == 0)
    def _():
        xprev_ref[...] = jnp.zeros_like(xprev_ref)
        hprev_ref[...] = jnp.zeros_like(hprev_ref)

    xz = zx_ref[...].astype(F32)
    prev8 = xprev_ref[...]
    row8 = lax.broadcasted_iota(jnp.int32, (8, width), 0)
    cw = cw_ref[...]
    xa = cw[CONV_WIDTH - 1:CONV_WIDTH] * xz + cb_ref[...]
    for d in range(1, CONV_WIDTH):
        rolled = pltpu.roll(xz, d, 0)
        head = jnp.where(row8 < d, pltpu.roll(prev8, d, 0), rolled[0:8])
        shifted = jnp.concatenate([head, rolled[8:]], axis=0)
        xa = xa + cw[CONV_WIDTH - 1 - d:CONV_WIDTH - d] * shifted
    xprev_ref[...] = xz[rows - 8:rows]

    xab = xa.astype(BF16)
    r_parts, i_parts = [], []
    for h in range(heads):
        xh = xab[:, h * HEAD_DIM:(h + 1) * HEAD_DIM]
        r_parts.append(jnp.dot(xh, wr_ref[h], preferred_element_type=F32))
        i_parts.append(jnp.dot(xh, wi_ref[h], preferred_element_type=F32))
    r = _sigmoid(jnp.concatenate(r_parts, axis=1) + br_ref[...])
    gate_i = _sigmoid(jnp.concatenate(i_parts, axis=1) + bi_ref[...])

    lam = lam_ref[...]
    softplus_neg_lam = jnp.maximum(-lam, 0.0) + jnp.log1p(jnp.exp(-jnp.abs(lam)))
    log_a = (-LRU_C) * r * softplus_neg_lam
    a = jnp.exp(log_a)
    mult = jnp.sqrt(1.0 - jnp.exp(2.0 * log_a))
    row = lax.broadcasted_iota(jnp.int32, (rows, width), 0)
    mult = jnp.where(jnp.logical_and(row == 0, s == 0), 1.0, mult)
    u = mult * (gate_i * xa)

    d = 1
    while d < rows:
        a_sh = pltpu.roll(a, d, 0)
        u_sh = pltpu.roll(u, d, 0)
        valid = row >= d
        u = jnp.where(valid, u + a * u_sh, u)
        a = jnp.where(valid, a * a_sh, a)
        d *= 2
    h = u + a * hprev_ref[0:1, :]
    hprev_ref[...] = jnp.broadcast_to(h[rows - 1:rows], hprev_ref.shape)

    zg = zg_ref[...].astype(F32)
    gelu = 0.5 * zg * (1.0 + jnp.tanh(math.sqrt(2.0 / math.pi) * (zg + 0.044715 * (zg * zg * zg))))
    y = h * gelu
    ms = jnp.mean(y * y, axis=-1, keepdims=True)
    o_ref[...] = (y * lax.rsqrt(ms + NORM_EPS) * ng_ref[...]).astype(o_ref.dtype)


def _lru(z, conv_w, conv_b, w_r, b_r, w_i, b_i, lam, norm_gain, *, batch, seq, rows=256):
    heads = w_r.shape[0]
    width = heads * HEAD_DIM
    nblk = seq // rows
    row_spec = lambda c: pl.BlockSpec((rows, width), lambda b, s, c=c: (b * nblk + s, c))
    vec = pl.BlockSpec((1, width), lambda b, s: (0, 0))
    wspec = pl.BlockSpec((heads, HEAD_DIM, HEAD_DIM), lambda b, s: (0, 0, 0))
    return pl.pallas_call(
        functools.partial(_lru_kernel, rows=rows, heads=heads),
        grid=(batch, nblk),
        in_specs=[row_spec(0), row_spec(1),
                  pl.BlockSpec((CONV_WIDTH, width), lambda b, s: (0, 0)), vec,
                  wspec, vec, wspec, vec, vec, vec],
        out_specs=pl.BlockSpec((rows, width), lambda b, s: (b * nblk + s, 0)),
        out_shape=jax.ShapeDtypeStruct((batch * seq, width), BF16),
        scratch_shapes=[pltpu.VMEM((8, width), F32), pltpu.VMEM((8, width), F32)],
        compiler_params=_params(("parallel", "arbitrary")),
        name="rg_lru",
    )(z, z, conv_w, conv_b.reshape(1, width), w_r.astype(BF16), b_r.reshape(1, width),
      w_i.astype(BF16), b_i.reshape(1, width), lam.reshape(1, width), norm_gain.reshape(1, width))


def _split3(x):
    hi = x.astype(BF16)
    r1 = x - hi.astype(F32)
    mid = r1.astype(BF16)
    lo = (r1 - mid.astype(F32)).astype(BF16)
    return hi, mid, lo


def _hgrn_kernel(zq_ref, zf_ref, zv_ref, zg_ref, lbp_ref, ng_ref, o_ref,
                 q_s, k_s, g_s, o_s, st_s, *, rows, layer):
    s = pl.program_id(2)
    C, c = HGRN_CHUNK, HGRN_SUB
    D = HEAD_DIM

    @pl.when(s == 0)
    def _():
        st_s[...] = jnp.zeros_like(st_s)

    lbp = lbp_ref[...].astype(F32)
    e = jnp.exp(lbp - jnp.max(lbp, axis=0, keepdims=True))
    sm = e / jnp.sum(e, axis=0, keepdims=True)
    lb = jnp.zeros((1, D), F32)
    for r_ in range(1, layer + 1):
        lb = lb + sm[r_:r_ + 1]

    zf = zf_ref[...].astype(F32)
    zq = zq_ref[...].astype(F32)
    q_s[...] = zq * _sigmoid(zq) * (D ** -0.5)
    k_s[...] = (1.0 - lb) * _sigmoid(-zf)
    y = jnp.log1p(-lb) + _log_sigmoid(zf)
    if layer == 0:
        log_f = y
    else:
        la = jnp.log(lb)
        log_f = jnp.maximum(la, y) + jnp.log1p(jnp.exp(-jnp.abs(la - y)))

    tri = (lax.broadcasted_iota(jnp.int32, (C, C), 0)
           >= lax.broadcasted_iota(jnp.int32, (C, C), 1)).astype(BF16)
    rid = lax.broadcasted_iota(jnp.int32, (c, D), 0)

    for ch in range(rows // C):
        base = ch * C
        hi, mid, lo = _split3(log_f[base:base + C])
        g3 = jnp.dot(tri, jnp.concatenate([hi, mid, lo], axis=1), preferred_element_type=F32)
        G = g3[:, 0:D] + g3[:, D:2 * D] + g3[:, 2 * D:3 * D]
        g_s[...] = G
        q = q_s[base:base + C]
        k = k_s[base:base + C]
        v = zv_ref[base:base + C, :].astype(F32)
        vb = v.astype(BF16)
        g_last = G[C - 1:C]

        st = st_s[...]
        qg = (q * jnp.exp(G)).astype(BF16)
        o_s[...] = lax.dot_general(qg, st.astype(BF16), (((1,), (1,)), ((), ())),
                                   preferred_element_type=F32)
        kd = (k * jnp.exp(g_last - G)).astype(BF16)
        st_s[...] = st * jnp.exp(g_last) + lax.dot_general(
            vb, kd, (((0,), (0,)), ((), ())), preferred_element_type=F32)

        m = C // 2
        while m >= c:
            for p in range(C // (2 * m)):
                lo0, hi0 = 2 * p * m, 2 * p * m + m
                g_ref_row = G[hi0 - 1:hi0]
                qh = (q[hi0:hi0 + m] * jnp.exp(G[hi0:hi0 + m] - g_ref_row)).astype(BF16)
                kl = (k[lo0:lo0 + m] * jnp.exp(g_ref_row - G[lo0:lo0 + m])).astype(BF16)
                sc = lax.dot_general(qh, kl, (((1,), (1,)), ((), ())), preferred_element_type=F32)
                o_s[hi0:hi0 + m, :] += jnp.dot(sc.astype(BF16), vb[lo0:lo0 + m],
                                               preferred_element_type=F32)
            m //= 2

        def diag_block(b, carry):
            r0 = pl.multiple_of(b * c, c)
            gb = g_s[pl.ds(r0, c), :]
            qb = q_s[pl.ds(base + r0, c), :]
            acc = jnp.zeros((c, D), F32)
            for j in range(c):
                gj = g_s[pl.ds(r0 + j, 1), :]
                kj = k_s[pl.ds(base + r0 + j, 1), :]
                vj = zv_ref[pl.ds(base + r0 + j, 1), :].astype(F32)
                w = qb * jnp.exp(jnp.minimum(gb - gj, 0.0)) * kj
                col = jnp.sum(w, axis=-1, keepdims=True)
                acc = acc + jnp.where(rid >= j, col, 0.0) * vj
            o_s[pl.ds(r0, c), :] += acc
            return carry

        lax.fori_loop(0, C // c, diag_block, 0)

        o = o_s[...]
        ms = jnp.mean(o * o, axis=-1, keepdims=True)
        zg = zg_ref[base:base + C, :].astype(F32)
        o_ref[base:base + C, :] = (o * lax.rsqrt(ms + NORM_EPS) * ng_ref[...]
                                   * (zg * _sigmoid(zg))).astype(o_ref.dtype)


def _hgrn(z, lower_bounds, norm_gain, *, layer, batch, seq, heads, col0, rows=256):
    nblk = seq // rows
    D = HEAD_DIM
    depth = lower_bounds.shape[0]

    def zspec(group):
        off = col0 // D + group * heads
        return pl.BlockSpec((rows, D), lambda b, h, s, off=off: (b * nblk + s, off + h))

    return pl.pallas_call(
        functools.partial(_hgrn_kernel, rows=rows, layer=layer),
        grid=(batch, heads, nblk),
        in_specs=[zspec(0), zspec(1), zspec(2), zspec(3),
                  pl.BlockSpec((depth, D), lambda b, h, s: (0, h)),
                  pl.BlockSpec((1, D), lambda b, h, s: (0, 0))],
        out_specs=pl.BlockSpec((rows, D), lambda b, h, s: (b * nblk + s, h)),
        out_shape=jax.ShapeDtypeStruct((batch * seq, heads * D), BF16),
        scratch_shapes=[pltpu.VMEM((rows, D), F32), pltpu.VMEM((rows, D), F32),
                        pltpu.VMEM((HGRN_CHUNK, D), F32), pltpu.VMEM((HGRN_CHUNK, D), F32),
                        pltpu.VMEM((D, D), F32)],
        compiler_params=_params(("parallel", "parallel", "arbitrary")),
        name="hgrn2",
    )(z, z, z, z, lower_bounds, norm_gain.reshape(1, D))


def _qk_prep_kernel(zq_ref, zk_ref, zv_ref, cos_ref, sin_ref, qg_ref, kg_ref,
                    q_ref, k_ref, v_ref, *, nsub):
    D = HEAD_DIM
    cos = cos_ref[...]
    sin = sin_ref[...]

    def norm_rope(x, gain, scale):
        ms = jnp.mean(x * x, axis=-1, keepdims=True)
        xn = x * lax.rsqrt(ms + NORM_EPS) * gain
        return (xn * cos + pltpu.roll(xn, D // 2, 1) * sin) * scale

    for c in range(nsub):
        sl = slice(c * D, (c + 1) * D)
        q_ref[:, sl] = norm_rope(zq_ref[:, sl].astype(F32), qg_ref[...], D ** -0.5).astype(q_ref.dtype)
        k_ref[:, sl] = norm_rope(zk_ref[:, sl].astype(F32), kg_ref[...], 1.0).astype(k_ref.dtype)
    v_ref[...] = zv_ref[...].astype(v_ref.dtype)


def _qk_prep(z, cos, sin_signed, q_gain, k_gain, *, batch, seq, width, col0, rows=256):
    nblk = seq // rows
    D = HEAD_DIM
    cb = col0 // width
    zspec = lambda g: pl.BlockSpec((rows, width), lambda i, g=g: (i, cb + g))
    tab = pl.BlockSpec((rows, D), lambda i: (i % nblk, 0))
    vec = pl.BlockSpec((1, D), lambda i: (0, 0))
    out = pl.BlockSpec((rows, width), lambda i: (i, 0))
    shp = jax.ShapeDtypeStruct((batch * seq, width), BF16)
    return pl.pallas_call(
        functools.partial(_qk_prep_kernel, nsub=width // D),
        grid=(batch * nblk,),
        in_specs=[zspec(0), zspec(1), zspec(2), tab, tab, vec, vec],
        out_specs=[out, out, out],
        out_shape=[shp, shp, shp],
        compiler_params=_params(("parallel",)),
        name="qk_prep",
    )(z, z, z, cos, sin_signed, q_gain.reshape(1, D), k_gain.reshape(1, D))


def _flash_kernel(q_ref, k_ref, v_ref, lp_ref, g_ref, o_ref, m_s, l_s, acc_s, *, blk, lam_init):
    i = pl.program_id(2)
    j = pl.program_id(3)
    D = HEAD_DIM

    @pl.when(j == 0)
    def _():
        m_s[...] = jnp.full_like(m_s, -jnp.inf)
        l_s[...] = jnp.zeros_like(l_s)
        acc_s[...] = jnp.zeros_like(acc_s)

    def step(masked):
        v = v_ref[...]
        for c in range(2):
            q = q_ref[:, c * D:(c + 1) * D]
            k = k_ref[:, c * D:(c + 1) * D]
            s = lax.dot_general(q, k, (((1,), (1,)), ((), ())), preferred_element_type=F32)
            if masked:
                row = lax.broadcasted_iota(jnp.int32, (blk, blk), 0)
                col = lax.broadcasted_iota(jnp.int32, (blk, blk), 1)
                s = jnp.where(col <= row, s, -jnp.inf)
            m_prev = m_s[c]
            m_new = jnp.maximum(m_prev, jnp.max(s, axis=-1, keepdims=True))
            alpha = jnp.exp(m_prev - m_new)
            p = jnp.exp(s - m_new)
            l_s[c] = alpha * l_s[c] + jnp.sum(p, axis=-1, keepdims=True)
            acc_s[c] = alpha * acc_s[c] + jnp.dot(p.astype(BF16), v, preferred_element_type=F32)
            m_s[c] = m_new

    @pl.when(j < i)
    def _():
        step(False)

    @pl.when(j == i)
    def _():
        step(True)
        lp = lp_ref[...].astype(F32)
        lam = (jnp.exp(jnp.sum(lp[0:1] * lp[1:2], axis=-1, keepdims=True))
               - jnp.exp(jnp.sum(lp[2:3] * lp[3:4], axis=-1, keepdims=True)) + lam_init)
        o = acc_s[0] / l_s[0] - lam * (acc_s[1] / l_s[1])
        ms = jnp.mean(o * o, axis=-1, keepdims=True)
        o_ref[...] = (o * lax.rsqrt(ms + NORM_EPS) * g_ref[...] * (1.0 - lam_init)).astype(o_ref.dtype)


def _flash(q, k, v, lam_params, subln_gain, *, layer, batch, seq, heads, blk=512):
    nblk = seq // blk
    D = HEAD_DIM
    lam_init = 0.8 - 0.6 * math.exp(-0.3 * layer)
    qspec = pl.BlockSpec((blk, 2 * D), lambda b, h, i, j: (b * nblk + i, h))
    kvspec = pl.BlockSpec((blk, 2 * D), lambda b, h, i, j: (b * nblk + jnp.minimum(j, i), h))
    return pl.pallas_call(
        functools.partial(_flash_kernel, blk=blk, lam_init=lam_init),
        grid=(batch, heads, nblk, nblk),
        in_specs=[qspec, kvspec, kvspec,
                  pl.BlockSpec((4, D), lambda b, h, i, j: (0, 0)),
                  pl.BlockSpec((1, 2 * D), lambda b, h, i, j: (0, 0))],
        out_specs=qspec,
        out_shape=jax.ShapeDtypeStruct((batch * seq, heads * 2 * D), BF16),
        scratch_shapes=[pltpu.VMEM((2, blk, 1), F32), pltpu.VMEM((2, blk, 1), F32),
                        pltpu.VMEM((2, blk, 2 * D), F32)],
        compiler_params=_params(("parallel", "parallel", "parallel", "arbitrary")),
        name="diff_flash",
    )(q, k, v, lam_params, subln_gain.reshape(1, 2 * D))


def kernel(x, ln1_gain, w_in, conv_w, conv_b, lru_w_r, lru_b_r, lru_w_i, lru_b_i, lru_lambda, lru_norm_gain, hgrn_lower_bounds, hgrn_norm_gain, q_norm_gain, k_norm_gain, diff_lambda, diff_subln_gain, w_out, ln2_gain, w_ff1, w_ff2):
    batch, seq, d_model = x.shape
    depth = w_in.shape[0]
    D = HEAD_DIM
    lru_width = conv_w.shape[-1]
    key_width = hgrn_lower_bounds.shape[-1]
    in_width = w_in.shape[-1]
    mix_width = w_out.shape[1]
    diff_width = in_width - 2 * key_width - 2 * mix_width
    val_width = mix_width - lru_width - diff_width
    assert q_norm_gain.shape[-1] == D and hgrn_norm_gain.shape[-1] == D
    assert lru_w_r.shape[-1] == D and key_width == val_width
    hgrn_heads = key_width // D
    diff_heads = diff_width // (2 * D)
    off_b = 2 * lru_width
    off_c = off_b + 2 * key_width + 2 * val_width

    inv = 1.0 / (ROPE_THETA ** (jnp.arange(0, D, 2, dtype=F32) / D))
    ang = jnp.arange(seq, dtype=F32)[:, None] * inv[None, :]
    ang = jnp.concatenate([ang, ang], axis=-1)
    cos = jnp.cos(ang)
    sign = jnp.concatenate([-jnp.ones((D // 2,), F32), jnp.ones((D // 2,), F32)])
    sin_signed = jnp.sin(ang) * sign[None, :]

    xf = x.reshape(batch * seq, d_model)
    for l in range(depth):
        h = _rmsnorm(xf, ln1_gain[l])
        z = _matmul(h, w_in[l].astype(BF16), out_dtype=F32)

        ya = _lru(z, conv_w[l], conv_b[l], lru_w_r[l], lru_b_r[l], lru_w_i[l], lru_b_i[l],
                  lru_lambda[l], lru_norm_gain[l], batch=batch, seq=seq)
        yb = _hgrn(z, hgrn_lower_bounds, hgrn_norm_gain[l], layer=l, batch=batch, seq=seq,
                   heads=hgrn_heads, col0=off_b)
        qn, kn, vn = _qk_prep(z, cos, sin_signed, q_norm_gain[l], k_norm_gain[l],
                              batch=batch, seq=seq, width=diff_width, col0=off_c)
        yc = _flash(qn, kn, vn, diff_lambda[l], diff_subln_gain[l], layer=l, batch=batch, seq=seq,
                    heads=diff_heads)

        y = jnp.concatenate([ya, yb, yc], axis=-1)
        xf = _matmul(y, w_out[l].astype(BF16), mode="residual", residual=xf)

        h2 = _rmsnorm(xf, ln2_gain[l])
        u = _matmul(h2, w_ff1[l].astype(BF16), mode="relu2", out_dtype=BF16)
        xf = _matmul(u, w_ff2[l].astype(BF16), mode="residual", residual=xf)
    return xf.reshape(batch, seq, d_model)
```

```python
import functools
import math

import jax
import jax.numpy as jnp
from jax import lax
from jax.experimental import pallas as pl
from jax.experimental.pallas import tpu as pltpu

F32 = jnp.float32
BF16 = jnp.bfloat16

NORM_EPS = 1e-6
LRU_C = 8.0
ROPE_THETA = 10000.0
CONV_WIDTH = 4
LOG2_E = 1.4426950408889634
HEAD_DIM = 128
HGRN_CHUNK = 128
HGRN_SUB = 8
VMEM_LIMIT = 56 * 1024 * 1024


def _params(semantics):
    return pltpu.CompilerParams(dimension_semantics=semantics, vmem_limit_bytes=VMEM_LIMIT)


def _sigmoid(x):
    return 1.0 / (1.0 + jnp.exp(-x))


def _log_sigmoid(x):
    return jnp.minimum(x, 0.0) - jnp.log1p(jnp.exp(-jnp.abs(x)))


def _rmsnorm_kernel(x_ref, g_ref, o_ref):
    x = x_ref[...]
    ms = jnp.mean(x * x, axis=-1, keepdims=True)
    o_ref[...] = (x * lax.rsqrt(ms + NORM_EPS) * g_ref[...]).astype(o_ref.dtype)


def _rmsnorm(x, gain, tm=256):
    n, d = x.shape
    return pl.pallas_call(
        _rmsnorm_kernel,
        grid=(n // tm,),
        in_specs=[pl.BlockSpec((tm, d), lambda i: (i, 0)),
                  pl.BlockSpec((1, d), lambda i: (0, 0))],
        out_specs=pl.BlockSpec((tm, d), lambda i: (i, 0)),
        out_shape=jax.ShapeDtypeStruct((n, d), BF16),
        compiler_params=_params(("parallel",)),
        name="rmsnorm",
    )(x, gain.reshape(1, d))


def _mm_kernel(*refs, nk, mode):
    if mode == "residual":
        a_ref, w_ref, r_ref, o_ref, acc_ref = refs
    else:
        a_ref, w_ref, o_ref, acc_ref = refs
        r_ref = None
    k = pl.program_id(2)

    @pl.when(k == 0)
    def _():
        acc_ref[...] = jnp.zeros_like(acc_ref)

    acc_ref[...] += jnp.dot(a_ref[...], w_ref[...], preferred_element_type=F32)

    @pl.when(k == nk - 1)
    def _():
        o_ref[...] = _epilogue(acc_ref[...], mode, r_ref).astype(o_ref.dtype)


def _epilogue(acc, mode, r_ref):
    if mode == "relu2":
        acc = jnp.square(jnp.maximum(acc, 0.0))
    elif mode == "residual":
        acc = acc + r_ref[...]
    return acc


def _mm_fullk_kernel(*refs, mode):
    if mode == "residual":
        a_ref, w_ref, r_ref, o_ref = refs
    else:
        a_ref, w_ref, o_ref = refs
        r_ref = None
    acc = jnp.dot(a_ref[...], w_ref[...], preferred_element_type=F32)
    o_ref[...] = _epilogue(acc, mode, r_ref).astype(o_ref.dtype)


def _matmul_fullk(a, w, *, mode, residual, out_dtype, tm, tn):
    m, kdim = a.shape
    _, n = w.shape
    in_specs = [pl.BlockSpec((tm, kdim), lambda i, j: (i, 0)),
                pl.BlockSpec((kdim, tn), lambda i, j: (0, j))]
    args = [a, w]
    if mode == "residual":
        in_specs.append(pl.BlockSpec((tm, tn), lambda i, j: (i, j)))
        args.append(residual)
    return pl.pallas_call(
        functools.partial(_mm_fullk_kernel, mode=mode),
        grid=(m // tm, n // tn),
        in_specs=in_specs,
        out_specs=pl.BlockSpec((tm, tn), lambda i, j: (i, j)),
        out_shape=jax.ShapeDtypeStruct((m, n), out_dtype),
        compiler_params=_params(("parallel", "arbitrary")),
        name="matmul_fullk_" + mode,
    )(*args)


def _matmul(a, w, *, mode="plain", residual=None, out_dtype=F32, tm=1024, tn=1024, tk=2048):
    m, kdim = a.shape
    _, n = w.shape
    tm, tn, tk = min(tm, m), min(tn, n), min(tk, kdim)
    if kdim <= 4096:
        return _matmul_fullk(a, w, mode=mode, residual=residual, out_dtype=out_dtype, tm=tm, tn=tn)
    nk = kdim // tk
    in_specs = [pl.BlockSpec((tm, tk), lambda i, j, k: (i, k)),
                pl.BlockSpec((tk, tn), lambda i, j, k: (k, j))]
    args = [a, w]
    if mode == "residual":
        in_specs.append(pl.BlockSpec((tm, tn), lambda i, j, k: (i, j)))
        args.append(residual)
    return pl.pallas_call(
        functools.partial(_mm_kernel, nk=nk, mode=mode),
        grid=(m // tm, n // tn, nk),
        in_specs=in_specs,
        out_specs=pl.BlockSpec((tm, tn), lambda i, j, k: (i, j)),
        out_shape=jax.ShapeDtypeStruct((m, n), out_dtype),
        scratch_shapes=[pltpu.VMEM((tm, tn), F32)],
        compiler_params=_params(("parallel", "parallel", "arbitrary")),
        name="matmul_" + mode,
    )(*args)


def _lru_kernel(zg_ref, zx_ref, cw_ref, cb_ref, wr_ref, br_ref, wi_ref, bi_ref, lam_ref, ng_ref,
                o_ref, xprev_ref, hprev_ref, *, rows, heads):
    s = pl.program_id(1)
    width = heads * HEAD_DIM

    @pl.when(s == 0)
    def _():
        xprev_ref[...] = jnp.zeros_like(xprev_ref)
        hprev_ref[...] = jnp.zeros_like(hprev_ref)

    xz = zx_ref[...].astype(F32)
    prev8 = xprev_ref[...]
    row8 = lax.broadcasted_iota(jnp.int32, (8, width), 0)
    cw = cw_ref[...]
    xa = cw[CONV_WIDTH - 1:CONV_WIDTH] * xz + cb_ref[...]
    for d in range(1, CONV_WIDTH):
        rolled = pltpu.roll(xz, d, 0)
        head = jnp.where(row8 < d, pltpu.roll(prev8, d, 0), rolled[0:8])
        shifted = jnp.concatenate([head, rolled[8:]], axis=0)
        xa = xa + cw[CONV_WIDTH - 1 - d:CONV_WIDTH - d] * shifted
    xprev_ref[...] = xz[rows - 8:rows]

    xab = xa.astype(BF16)
    r_parts, i_parts = [], []
    for h in range(heads):
        xh = xab[:, h * HEAD_DIM:(h + 1) * HEAD_DIM]
        r_parts.append(jnp.dot(xh, wr_ref[h], preferred_element_type=F32))
        i_parts.append(jnp.dot(xh, wi_ref[h], preferred_element_type=F32))
    r = _sigmoid(jnp.concatenate(r_parts, axis=1) + br_ref[...])
    gate_i = _sigmoid(jnp.concatenate(i_parts, axis=1) + bi_ref[...])

    lam = lam_ref[...]
    softplus_neg_lam = jnp.maximum(-lam, 0.0) + jnp.log1p(jnp.exp(-jnp.abs(lam)))
    log_a = (-LRU_C) * r * softplus_neg_lam
    a = jnp.exp(log_a)
    mult = jnp.sqrt(1.0 - jnp.exp(2.0 * log_a))
    row = lax.broadcasted_iota(jnp.int32, (rows, width), 0)
    mult = jnp.where(jnp.logical_and(row == 0, s == 0), 1.0, mult)
    u = mult * (gate_i * xa)

    d = 1
    while d < rows:
        a_sh = pltpu.roll(a, d, 0)
        u_sh = pltpu.roll(u, d, 0)
        valid = row >= d
        u = jnp.where(valid, u + a * u_sh, u)
        a = jnp.where(valid, a * a_sh, a)
        d *= 2
    h = u + a * hprev_ref[0:1, :]
    hprev_ref[...] = jnp.broadcast_to(h[rows - 1:rows], hprev_ref.shape)

    zg = zg_ref[...].astype(F32)
    gelu = 0.5 * zg * (1.0 + jnp.tanh(math.sqrt(2.0 / math.pi) * (zg + 0.044715 * (zg * zg * zg))))
    y = h * gelu
    ms = jnp.mean(y * y, axis=-1, keepdims=True)
    o_ref[...] = (y * lax.rsqrt(ms + NORM_EPS) * ng_ref[...]).astype(o_ref.dtype)


def _lru(z, conv_w, conv_b, w_r, b_r, w_i, b_i, lam, norm_gain, *, batch, seq, rows=256):
    heads = w_r.shape[0]
    width = heads * HEAD_DIM
    nblk = seq // rows
    row_spec = lambda c: pl.BlockSpec((rows, width), lambda b, s, c=c: (b * nblk + s, c))
    vec = pl.BlockSpec((1, width), lambda b, s: (0, 0))
    wspec = pl.BlockSpec((heads, HEAD_DIM, HEAD_DIM), lambda b, s: (0, 0, 0))
    return pl.pallas_call(
        functools.partial(_lru_kernel, rows=rows, heads=heads),
        grid=(batch, nblk),
        in_specs=[row_spec(0), row_spec(1),
                  pl.BlockSpec((CONV_WIDTH, width), lambda b, s: (0, 0)), vec,
                  wspec, vec, wspec, vec, vec, vec],
        out_specs=pl.BlockSpec((rows, width), lambda b, s: (b * nblk + s, 0)),
        out_shape=jax.ShapeDtypeStruct((batch * seq, width), BF16),
        scratch_shapes=[pltpu.VMEM((8, width), F32), pltpu.VMEM((8, width), F32)],
        compiler_params=_params(("parallel", "arbitrary")),
        name="rg_lru",
    )(z, z, conv_w, conv_b.reshape(1, width), w_r.astype(BF16), b_r.reshape(1, width),
      w_i.astype(BF16), b_i.reshape(1, width), lam.reshape(1, width), norm_gain.reshape(1, width))


def _split3(x):
    hi = x.astype(BF16)
    r1 = x - hi.astype(F32)
    mid = r1.astype(BF16)
    lo = (r1 - mid.astype(F32)).astype(BF16)
    return hi, mid, lo


def _tile_roll(x, shift):
    rows, d = x.shape
    x3 = x.reshape(rows // HGRN_SUB, HGRN_SUB, d)
    return pltpu.roll(x3, shift, 1).reshape(rows, d)


def _hgrn_kernel(zq_ref, zf_ref, zv_ref, zg_ref, lbp_ref, ng_ref, o_ref, st_s, *, rows, layer):
    s = pl.program_id(2)
    C, c = HGRN_CHUNK, HGRN_SUB
    D = HEAD_DIM

    @pl.when(s == 0)
    def _():
        st_s[...] = jnp.zeros_like(st_s)

    lbp = lbp_ref[...].astype(F32)
    e = jnp.exp(lbp - jnp.max(lbp, axis=0, keepdims=True))
    sm = e / jnp.sum(e, axis=0, keepdims=True)
    lb = jnp.zeros((1, D), F32)
    for r_ in range(1, layer + 1):
        lb = lb + sm[r_:r_ + 1]

    row = lax.broadcasted_iota(jnp.int32, (C, C), 0)
    col = lax.broadcasted_iota(jnp.int32, (C, C), 1)
    tri = (row >= col).astype(BF16)
    differ = jnp.bitwise_xor(row, col)
    levels = []
    m = C // 2
    while m >= c:
        levels.append((m, (row > col) & (differ >= m) & (differ < 2 * m)))
        m //= 2
    diag_masks = [(differ < c) & (row - col == dlt) for dlt in range(c)]

    for ch in range(rows // C):
        sl = slice(ch * C, (ch + 1) * C)
        zq = zq_ref[sl, :].astype(F32)
        zf = zf_ref[sl, :].astype(F32)
        q = zq * _sigmoid(zq) * (D ** -0.5)
        ez = jnp.exp(-jnp.abs(zf))
        inv = 1.0 / (1.0 + ez)
        sig_pos = jnp.where(zf >= 0.0, inv, ez * inv)
        sig_neg = jnp.where(zf >= 0.0, ez * inv, inv)
        f = lb + (1.0 - lb) * sig_pos
        k = (1.0 - lb) * sig_neg
        y = jnp.log1p(-lb) + (jnp.minimum(zf, 0.0) - jnp.log(1.0 + ez))
        if layer == 0:
            log_f = y
        else:
            la = jnp.log(lb)
            log_f = jnp.maximum(la, y) + jnp.log(1.0 + jnp.exp(-jnp.abs(la - y)))

        hi, mid, lo = _split3(log_f)
        g3 = jnp.dot(tri, jnp.concatenate([hi, mid, lo], axis=1), preferred_element_type=F32)
        G = g3[:, 0:D] + g3[:, D:2 * D] + g3[:, 2 * D:3 * D]
        vb = zv_ref[sl, :].astype(BF16)
        g_last = G[C - 1:C]

        a = jnp.zeros((C, C), F32)
        dec = None
        for dlt in range(c):
            if dlt == 0:
                w = q * k
            else:
                f_sh = f if dlt == 1 else _tile_roll(f, dlt - 1)
                dec = f_sh if dlt == 1 else dec * f_sh
                w = q * _tile_roll(k, dlt) * dec
            a = jnp.where(diag_masks[dlt], jnp.sum(w, axis=-1, keepdims=True), a)

        for m, mask in levels:
            g_bnd = jnp.concatenate(
                [jnp.broadcast_to(G[p * 2 * m + m - 1:p * 2 * m + m], (2 * m, D))
                 for p in range(C // (2 * m))], axis=0)
            e_ref = jnp.exp(-jnp.abs(G - g_bnd))
            sc = lax.dot_general((q * e_ref).astype(BF16), (k * e_ref).astype(BF16),
                                 (((1,), (1,)), ((), ())), preferred_element_type=F32)
            a = jnp.where(mask, sc, a)

        st = st_s[...]
        qg = (q * jnp.exp(G)).astype(BF16)
        o = (jnp.dot(a.astype(BF16), vb, preferred_element_type=F32)
             + lax.dot_general(qg, st.astype(BF16), (((1,), (1,)), ((), ())),
                               preferred_element_type=F32))
        kd = (k * jnp.exp(g_last - G)).astype(BF16)
        st_s[...] = st * jnp.exp(g_last) + lax.dot_general(
            vb, kd, (((0,), (0,)), ((), ())), preferred_element_type=F32)

        ms = jnp.mean(o * o, axis=-1, keepdims=True)
        zg = zg_ref[sl, :].astype(F32)
        o_ref[sl, :] = (o * lax.rsqrt(ms + NORM_EPS) * ng_ref[...]
                        * (zg * _sigmoid(zg))).astype(o_ref.dtype)


def _hgrn(z, lower_bounds, norm_gain, *, layer, batch, seq, heads, col0, rows=512):
    nblk = seq // rows
    D = HEAD_DIM
    depth = lower_bounds.shape[0]

    def zspec(group):
        off = col0 // D + group * heads
        return pl.BlockSpec((rows, D), lambda b, h, s, off=off: (b * nblk + s, off + h))

    return pl.pallas_call(
        functools.partial(_hgrn_kernel, rows=rows, layer=layer),
        grid=(batch, heads, nblk),
        in_specs=[zspec(0), zspec(1), zspec(2), zspec(3),
                  pl.BlockSpec((depth, D), lambda b, h, s: (0, h)),
                  pl.BlockSpec((1, D), lambda b, h, s: (0, 0))],
        out_specs=pl.BlockSpec((rows, D), lambda b, h, s: (b * nblk + s, h)),
        out_shape=jax.ShapeDtypeStruct((batch * seq, heads * D), BF16),
        scratch_shapes=[pltpu.VMEM((D, D), F32)],
        compiler_params=_params(("parallel", "parallel", "arbitrary")),
        name="hgrn2",
    )(z, z, z, z, lower_bounds, norm_gain.reshape(1, D))


def _qk_prep_kernel(zq_ref, zk_ref, zv_ref, cos_ref, sin_ref, qg_ref, kg_ref,
                    q_ref, k_ref, v_ref, *, nsub):
    D = HEAD_DIM
    cos = cos_ref[...]
    sin = sin_ref[...]

    def norm_rope(x, gain, scale):
        ms = jnp.mean(x * x, axis=-1, keepdims=True)
        xn = x * lax.rsqrt(ms + NORM_EPS) * gain
        return (xn * cos + pltpu.roll(xn, D // 2, 1) * sin) * scale

    for c in range(nsub):
        sl = slice(c * D, (c + 1) * D)
        q_ref[:, sl] = norm_rope(zq_ref[:, sl].astype(F32), qg_ref[...],
                                 D ** -0.5 * LOG2_E).astype(q_ref.dtype)
        k_ref[:, sl] = norm_rope(zk_ref[:, sl].astype(F32), kg_ref[...], 1.0).astype(k_ref.dtype)
    v_ref[...] = zv_ref[...].astype(v_ref.dtype)


def _qk_prep(z, cos, sin_signed, q_gain, k_gain, *, batch, seq, width, col0, rows=256):
    nblk = seq // rows
    D = HEAD_DIM
    cb = col0 // width
    zspec = lambda g: pl.BlockSpec((rows, width), lambda i, g=g: (i, cb + g))
    tab = pl.BlockSpec((rows, D), lambda i: (i % nblk, 0))
    vec = pl.BlockSpec((1, D), lambda i: (0, 0))
    out = pl.BlockSpec((rows, width), lambda i: (i, 0))
    shp = jax.ShapeDtypeStruct((batch * seq, width), BF16)
    return pl.pallas_call(
        functools.partial(_qk_prep_kernel, nsub=width // D),
        grid=(batch * nblk,),
        in_specs=[zspec(0), zspec(1), zspec(2), tab, tab, vec, vec],
        out_specs=[out, out, out],
        out_shape=[shp, shp, shp],
        compiler_params=_params(("parallel",)),
        name="qk_prep",
    )(z, z, z, cos, sin_signed, q_gain.reshape(1, D), k_gain.reshape(1, D))


def _flash_kernel(q_ref, k_ref, v_ref, lp_ref, g_ref, o_ref, m_s, l_s, acc_s, *, blk, lam_init):
    i = pl.program_id(2)
    D = HEAD_DIM
    lanes = blk // D

    m_s[...] = jnp.full_like(m_s, -jnp.inf)
    l_s[...] = jnp.zeros_like(l_s)
    acc_s[...] = jnp.zeros_like(acc_s)

    def block(kv_start, masked):
        k = k_ref[pl.ds(kv_start, blk), :]
        v = v_ref[pl.ds(kv_start, blk), :]
        for c in range(2):
            q = q_ref[:, c * D:(c + 1) * D]
            s = lax.dot_general(q, k[:, c * D:(c + 1) * D], (((1,), (1,)), ((), ())),
                                preferred_element_type=F32)
            if masked:
                row = lax.broadcasted_iota(jnp.int32, (blk, blk), 0)
                col = lax.broadcasted_iota(jnp.int32, (blk, blk), 1)
                s = jnp.where(col <= row, s, -jnp.inf)
            m_prev = m_s[c]
            m_new = jnp.maximum(m_prev, jnp.max(s, axis=-1, keepdims=True))
            alpha = jnp.exp2(m_prev - m_new)
            p = jnp.exp2(s - jnp.tile(m_new, (1, lanes)))
            l_s[c] = alpha * l_s[c] + jnp.sum(p, axis=-1, keepdims=True)
            acc_s[c] = jnp.tile(alpha, (1, 2)) * acc_s[c] + jnp.dot(
                p.astype(BF16), v, preferred_element_type=F32)
            m_s[c] = m_new

    def full_block(j, carry):
        block(pl.multiple_of(j * blk, blk), False)
        return carry

    lax.fori_loop(0, i, full_block, 0)
    block(pl.multiple_of(i * blk, blk), True)

    lp = lp_ref[...].astype(F32)
    lam = (jnp.exp(jnp.sum(lp[0:1] * lp[1:2], axis=-1, keepdims=True))
           - jnp.exp(jnp.sum(lp[2:3] * lp[3:4], axis=-1, keepdims=True)) + lam_init)
    o = (acc_s[0] / jnp.tile(l_s[0], (1, 2))) - lam * (acc_s[1] / jnp.tile(l_s[1], (1, 2)))
    ms = jnp.mean(o * o, axis=-1, keepdims=True)
    o_ref[...] = (o * lax.rsqrt(ms + NORM_EPS) * g_ref[...] * (1.0 - lam_init)).astype(o_ref.dtype)


def _flash(q, k, v, lam_params, subln_gain, *, layer, batch, seq, heads, blk=512):
    nblk = seq // blk
    D = HEAD_DIM
    lam_init = 0.8 - 0.6 * math.exp(-0.3 * layer)
    qspec = pl.BlockSpec((blk, 2 * D), lambda b, h, i: (b * nblk + i, h))
    kvspec = pl.BlockSpec((seq, 2 * D), lambda b, h, i: (b, h))
    return pl.pallas_call(
        functools.partial(_flash_kernel, blk=blk, lam_init=lam_init),
        grid=(batch, heads, nblk),
        in_specs=[qspec, kvspec, kvspec,
                  pl.BlockSpec((4, D), lambda b, h, i: (0, 0)),
                  pl.BlockSpec((1, 2 * D), lambda b, h, i: (0, 0))],
        out_specs=qspec,
        out_shape=jax.ShapeDtypeStruct((batch * seq, heads * 2 * D), BF16),
        scratch_shapes=[pltpu.VMEM((2, blk, D), F32), pltpu.VMEM((2, blk, D), F32),
                        pltpu.VMEM((2, blk, 2 * D), F32)],
        compiler_params=_params(("parallel", "parallel", "parallel")),
        name="diff_flash",
    )(q, k, v, lam_params, subln_gain.reshape(1, 2 * D))


def kernel(x, ln1_gain, w_in, conv_w, conv_b, lru_w_r, lru_b_r, lru_w_i, lru_b_i, lru_lambda, lru_norm_gain, hgrn_lower_bounds, hgrn_norm_gain, q_norm_gain, k_norm_gain, diff_lambda, diff_subln_gain, w_out, ln2_gain, w_ff1, w_ff2):
    batch, seq, d_model = x.shape
    depth = w_in.shape[0]
    D = HEAD_DIM
    lru_width = conv_w.shape[-1]
    key_width = hgrn_lower_bounds.shape[-1]
    in_width = w_in.shape[-1]
    mix_width = w_out.shape[1]
    diff_width = in_width - 2 * key_width - 2 * mix_width
    val_width = mix_width - lru_width - diff_width
    assert q_norm_gain.shape[-1] == D and hgrn_norm_gain.shape[-1] == D
    assert lru_w_r.shape[-1] == D and key_width == val_width
    hgrn_heads = key_width // D
    diff_heads = diff_width // (2 * D)
    off_b = 2 * lru_width
    off_c = off_b + 2 * key_width + 2 * val_width

    inv = 1.0 / (ROPE_THETA ** (jnp.arange(0, D, 2, dtype=F32) / D))
    ang = jnp.arange(seq, dtype=F32)[:, None] * inv[None, :]
    ang = jnp.concatenate([ang, ang], axis=-1)
    cos = jnp.cos(ang)
    sign = jnp.concatenate([-jnp.ones((D // 2,), F32), jnp.ones((D // 2,), F32)])
    sin_signed = jnp.sin(ang) * sign[None, :]

    xf = x.reshape(batch * seq, d_model)
    for l in range(depth):
        h = _rmsnorm(xf, ln1_gain[l])
        z = _matmul(h, w_in[l].astype(BF16), out_dtype=F32)

        ya = _lru(z, conv_w[l], conv_b[l], lru_w_r[l], lru_b_r[l], lru_w_i[l], lru_b_i[l],
                  lru_lambda[l], lru_norm_gain[l], batch=batch, seq=seq)
        yb = _hgrn(z, hgrn_lower_bounds, hgrn_norm_gain[l], layer=l, batch=batch, seq=seq,
                   heads=hgrn_heads, col0=off_b)
        qn, kn, vn = _qk_prep(z, cos, sin_signed, q_norm_gain[l], k_norm_gain[l],
                              batch=batch, seq=seq, width=diff_width, col0=off_c)
        yc = _flash(qn, kn, vn, diff_lambda[l], diff_subln_gain[l], layer=l, batch=batch, seq=seq,
                    heads=diff_heads)

        y = jnp.concatenate([ya, yb, yc], axis=-1)
        xf = _matmul(y, w_out[l].astype(BF16), mode="residual", residual=xf)

        h2 = _rmsnorm(xf, ln2_gain[l])
        u = _matmul(h2, w_ff1[l].astype(BF16), mode="relu2", out_dtype=BF16)
        xf = _matmul(u, w_ff2[l].astype(BF16), mode="residual", residual=xf)
    return xf.reshape(batch, seq, d_model)
```

```python
import functools
import math

import jax
import jax.numpy as jnp
from jax import lax
from jax.experimental import pallas as pl
from jax.experimental.pallas import tpu as pltpu

F32 = jnp.float32
BF16 = jnp.bfloat16

NORM_EPS = 1e-6
LRU_C = 8.0
ROPE_THETA = 10000.0
CONV_WIDTH = 4
LOG2_E = 1.4426950408889634
HEAD_DIM = 128
HGRN_CHUNK = 128
HGRN_SUB = 8
VMEM_LIMIT = 56 * 1024 * 1024


def _params(semantics):
    return pltpu.CompilerParams(dimension_semantics=semantics, vmem_limit_bytes=VMEM_LIMIT)


def _sigmoid(x):
    return 1.0 / (1.0 + jnp.exp(-x))


def _log_sigmoid(x):
    return jnp.minimum(x, 0.0) - jnp.log1p(jnp.exp(-jnp.abs(x)))


def _rmsnorm_kernel(x_ref, g_ref, o_ref):
    x = x_ref[...]
    ms = jnp.mean(x * x, axis=-1, keepdims=True)
    o_ref[...] = (x * lax.rsqrt(ms + NORM_EPS) * g_ref[...]).astype(o_ref.dtype)


def _rmsnorm(x, gain, tm=256):
    n, d = x.shape
    return pl.pallas_call(
        _rmsnorm_kernel,
        grid=(n // tm,),
        in_specs=[pl.BlockSpec((tm, d), lambda i: (i, 0)),
                  pl.BlockSpec((1, d), lambda i: (0, 0))],
        out_specs=pl.BlockSpec((tm, d), lambda i: (i, 0)),
        out_shape=jax.ShapeDtypeStruct((n, d), BF16),
        compiler_params=_params(("parallel",)),
        name="rmsnorm",
    )(x, gain.reshape(1, d))


LANES = 128


def _row_factor(ss_ref, width, tn):
    rstd = lax.rsqrt(ss_ref[...] * (1.0 / width) + NORM_EPS)
    return jnp.tile(rstd, (1, tn // LANES))


def _finish(acc, j, *, mode, r_ref, g_ref, o_ref, xg_ref, ss_ref):
    if mode == "relu2":
        acc = jnp.square(jnp.maximum(acc, 0.0))
    elif mode == "residual":
        acc = acc + r_ref[...]
    o_ref[...] = acc.astype(o_ref.dtype)
    if xg_ref is not None:
        xg_ref[...] = (acc * g_ref[...]).astype(xg_ref.dtype)

        @pl.when(j == 0)
        def _():
            ss_ref[...] = jnp.zeros_like(ss_ref)

        ss_ref[...] += jnp.sum(acc * acc, axis=-1, keepdims=True)


def _unpack(refs, n_a, row_scale, mode, emit_norm):
    it = iter(refs)
    a_refs = [next(it) for _ in range(n_a)]
    w_ref = next(it)
    ss_in = next(it) if row_scale else None
    r_ref = next(it) if mode == "residual" else None
    g_ref = next(it) if emit_norm else None
    o_ref = next(it)
    xg_ref = next(it) if emit_norm else None
    ss_ref = next(it) if emit_norm else None
    return a_refs, w_ref, ss_in, r_ref, g_ref, o_ref, xg_ref, ss_ref, list(it)


def _mm_fullk_kernel(*refs, splits, mode, row_scale, emit_norm):
    a_refs, w_ref, ss_in, r_ref, g_ref, o_ref, xg_ref, ss_ref, _ = _unpack(
        refs, len(splits), row_scale, mode, emit_norm)
    acc, off = None, 0
    for a_ref, kp in zip(a_refs, splits):
        part = jnp.dot(a_ref[...], w_ref[off:off + kp, :], preferred_element_type=F32)
        acc = part if acc is None else acc + part
        off += kp
    if row_scale:
        acc = acc * _row_factor(ss_in, off, o_ref.shape[-1])
    _finish(acc, pl.program_id(1), mode=mode, r_ref=r_ref, g_ref=g_ref, o_ref=o_ref,
            xg_ref=xg_ref, ss_ref=ss_ref)


def _mm_kloop_kernel(*refs, nk, mode, emit_norm):
    a_refs, w_ref, _, r_ref, g_ref, o_ref, xg_ref, ss_ref, (acc_ref,) = _unpack(
        refs, 1, False, mode, emit_norm)
    k = pl.program_id(2)

    @pl.when(k == 0)
    def _():
        acc_ref[...] = jnp.zeros_like(acc_ref)

    acc_ref[...] += jnp.dot(a_refs[0][...], w_ref[...], preferred_element_type=F32)

    @pl.when(k == nk - 1)
    def _():
        _finish(acc_ref[...], pl.program_id(1), mode=mode, r_ref=r_ref, g_ref=g_ref, o_ref=o_ref,
                xg_ref=xg_ref, ss_ref=ss_ref)


def _matmul(a_parts, w_stack, layer, *, mode="plain", residual=None, row_ss=None, next_gain=None,
            out_dtype=F32, tm=1024, tn=1024, tk=2048):
    m = a_parts[0].shape[0]
    splits = tuple(p.shape[1] for p in a_parts)
    kdim = sum(splits)
    n = w_stack.shape[-1]
    fullk = kdim <= 4096
    row_scale = row_ss is not None
    emit_norm = next_gain is not None
    tm, tn = min(tm, m), min(tn, n)

    if fullk:
        grid = (m // tm, n // tn)
        tile = lambda i, j: (i, j)
        rows = lambda i, j: (i, 0)
        cols = lambda i, j: (0, j)
        in_specs = [pl.BlockSpec((tm, kp), rows) for kp in splits]
        in_specs.append(pl.BlockSpec((None, kdim, tn), lambda i, j: (layer, 0, j)))
        body = functools.partial(_mm_fullk_kernel, splits=splits, mode=mode, row_scale=row_scale,
                                 emit_norm=emit_norm)
        scratch = []
        semantics = ("parallel", "arbitrary")
    else:
        assert len(a_parts) == 1 and not row_scale
        tk = min(tk, kdim)
        nk = kdim // tk
        grid = (m // tm, n // tn, nk)
        tile = lambda i, j, k: (i, j)
        rows = lambda i, j, k: (i, 0)
        cols = lambda i, j, k: (0, j)
        in_specs = [pl.BlockSpec((tm, tk), lambda i, j, k: (i, k)),
                    pl.BlockSpec((None, tk, tn), lambda i, j, k: (layer, k, j))]
        body = functools.partial(_mm_kloop_kernel, nk=nk, mode=mode, emit_norm=emit_norm)
        scratch = [pltpu.VMEM((tm, tn), F32)]
        semantics = ("parallel", "arbitrary", "arbitrary")

    args = list(a_parts) + [w_stack]
    if row_scale:
        in_specs.append(pl.BlockSpec((tm, LANES), rows))
        args.append(row_ss)
    if mode == "residual":
        in_specs.append(pl.BlockSpec((tm, tn), tile))
        args.append(residual)
    out_specs = [pl.BlockSpec((tm, tn), tile)]
    out_shape = [jax.ShapeDtypeStruct((m, n), out_dtype)]
    if emit_norm:
        in_specs.append(pl.BlockSpec((1, tn), cols))
        args.append(next_gain.reshape(1, n))
        out_specs += [pl.BlockSpec((tm, tn), tile), pl.BlockSpec((tm, LANES), rows)]
        out_shape += [jax.ShapeDtypeStruct((m, n), BF16), jax.ShapeDtypeStruct((m, LANES), F32)]
    outs = pl.pallas_call(
        body, grid=grid, in_specs=in_specs, out_specs=out_specs, out_shape=out_shape,
        scratch_shapes=scratch, compiler_params=_params(semantics),
        name=("matmul_fullk_" if fullk else "matmul_kloop_") + mode,
    )(*args)
    return outs if emit_norm else outs[0]


def _lru_kernel(zg_ref, zx_ref, cw_ref, cb_ref, wr_ref, br_ref, wi_ref, bi_ref, lam_ref, ng_ref,
                o_ref, xprev_ref, hprev_ref, *, rows, heads):
    s = pl.program_id(1)
    width = heads * HEAD_DIM

    @pl.when(s == 0)
    def _():
        xprev_ref[...] = jnp.zeros_like(xprev_ref)
        hprev_ref[...] = jnp.zeros_like(hprev_ref)

    xz = zx_ref[...].astype(F32)
    prev8 = xprev_ref[...]
    row8 = lax.broadcasted_iota(jnp.int32, (8, width), 0)
    cw = cw_ref[...]
    xa = cw[CONV_WIDTH - 1:CONV_WIDTH] * xz + cb_ref[...]
    for d in range(1, CONV_WIDTH):
        rolled = pltpu.roll(xz, d, 0)
        head = jnp.where(row8 < d, pltpu.roll(prev8, d, 0), rolled[0:8])
        shifted = jnp.concatenate([head, rolled[8:]], axis=0)
        xa = xa + cw[CONV_WIDTH - 1 - d:CONV_WIDTH - d] * shifted
    xprev_ref[...] = xz[rows - 8:rows]

    xab = xa.astype(BF16)
    r_parts, i_parts = [], []
    for h in range(heads):
        xh = xab[:, h * HEAD_DIM:(h + 1) * HEAD_DIM]
        r_parts.append(jnp.dot(xh, wr_ref[h], preferred_element_type=F32))
        i_parts.append(jnp.dot(xh, wi_ref[h], preferred_element_type=F32))
    r = _sigmoid(jnp.concatenate(r_parts, axis=1) + br_ref[...])
    gate_i = _sigmoid(jnp.concatenate(i_parts, axis=1) + bi_ref[...])

    lam = lam_ref[...]
    softplus_neg_lam = jnp.maximum(-lam, 0.0) + jnp.log1p(jnp.exp(-jnp.abs(lam)))
    log_a = (-LRU_C) * r * softplus_neg_lam
    a = jnp.exp(log_a)
    mult = jnp.sqrt(1.0 - jnp.exp(2.0 * log_a))
    row = lax.broadcasted_iota(jnp.int32, (rows, width), 0)
    mult = jnp.where(jnp.logical_and(row == 0, s == 0), 1.0, mult)
    u = mult * (gate_i * xa)

    d = 1
    while d < rows:
        a_sh = pltpu.roll(a, d, 0)
        u_sh = pltpu.roll(u, d, 0)
        valid = row >= d
        u = jnp.where(valid, u + a * u_sh, u)
        a = jnp.where(valid, a * a_sh, a)
        d *= 2
    h = u + a * hprev_ref[0:1, :]
    hprev_ref[...] = jnp.broadcast_to(h[rows - 1:rows], hprev_ref.shape)

    zg = zg_ref[...].astype(F32)
    gelu = 0.5 * zg * (1.0 + jnp.tanh(math.sqrt(2.0 / math.pi) * (zg + 0.044715 * (zg * zg * zg))))
    y = h * gelu
    ms = jnp.mean(y * y, axis=-1, keepdims=True)
    o_ref[...] = (y * lax.rsqrt(ms + NORM_EPS) * ng_ref[...]).astype(o_ref.dtype)


def _lru(z, conv_w, conv_b, w_r, b_r, w_i, b_i, lam, norm_gain, *, batch, seq, rows=256):
    heads = w_r.shape[0]
    width = heads * HEAD_DIM
    nblk = seq // rows
    row_spec = lambda c: pl.BlockSpec((rows, width), lambda b, s, c=c: (b * nblk + s, c))
    vec = pl.BlockSpec((1, width), lambda b, s: (0, 0))
    wspec = pl.BlockSpec((heads, HEAD_DIM, HEAD_DIM), lambda b, s: (0, 0, 0))
    return pl.pallas_call(
        functools.partial(_lru_kernel, rows=rows, heads=heads),
        grid=(batch, nblk),
        in_specs=[row_spec(0), row_spec(1),
                  pl.BlockSpec((CONV_WIDTH, width), lambda b, s: (0, 0)), vec,
                  wspec, vec, wspec, vec, vec, vec],
        out_specs=pl.BlockSpec((rows, width), lambda b, s: (b * nblk + s, 0)),
        out_shape=jax.ShapeDtypeStruct((batch * seq, width), BF16),
        scratch_shapes=[pltpu.VMEM((8, width), F32), pltpu.VMEM((8, width), F32)],
        compiler_params=_params(("parallel", "arbitrary")),
        name="rg_lru",
    )(z, z, conv_w, conv_b.reshape(1, width), w_r.astype(BF16), b_r.reshape(1, width),
      w_i.astype(BF16), b_i.reshape(1, width), lam.reshape(1, width), norm_gain.reshape(1, width))


def _split3(x):
    hi = x.astype(BF16)
    r1 = x - hi.astype(F32)
    mid = r1.astype(BF16)
    lo = (r1 - mid.astype(F32)).astype(BF16)
    return hi, mid, lo


def _tile_roll(x, shift):
    rows, d = x.shape
    x3 = x.reshape(rows // HGRN_SUB, HGRN_SUB, d)
    return pltpu.roll(x3, shift, 1).reshape(rows, d)


def _hgrn_kernel(zq_ref, zf_ref, zv_ref, zg_ref, lbp_ref, ng_ref, o_ref, st_s, *, rows, layer):
    s = pl.program_id(2)
    C, c = HGRN_CHUNK, HGRN_SUB
    D = HEAD_DIM

    @pl.when(s == 0)
    def _():
        st_s[...] = jnp.zeros_like(st_s)

    lbp = lbp_ref[...].astype(F32)
    e = jnp.exp(lbp - jnp.max(lbp, axis=0, keepdims=True))
    sm = e / jnp.sum(e, axis=0, keepdims=True)
    lb = jnp.zeros((1, D), F32)
    for r_ in range(1, layer + 1):
        lb = lb + sm[r_:r_ + 1]

    row = lax.broadcasted_iota(jnp.int32, (C, C), 0)
    col = lax.broadcasted_iota(jnp.int32, (C, C), 1)
    tri = (row >= col).astype(BF16)
    differ = jnp.bitwise_xor(row, col)
    levels = []
    m = C // 2
    while m >= c:
        levels.append((m, (row > col) & (differ >= m) & (differ < 2 * m)))
        m //= 2
    diag_masks = [(differ < c) & (row - col == dlt) for dlt in range(c)]

    for ch in range(rows // C):
        sl = slice(ch * C, (ch + 1) * C)
        zq = zq_ref[sl, :].astype(F32)
        zf = zf_ref[sl, :].astype(F32)
        q = zq * _sigmoid(zq) * (D ** -0.5)
        ez = jnp.exp(-jnp.abs(zf))
        inv = 1.0 / (1.0 + ez)
        sig_pos = jnp.where(zf >= 0.0, inv, ez * inv)
        sig_neg = jnp.where(zf >= 0.0, ez * inv, inv)
        f = lb + (1.0 - lb) * sig_pos
        k = (1.0 - lb) * sig_neg
        y = jnp.log1p(-lb) + (jnp.minimum(zf, 0.0) - jnp.log(1.0 + ez))
        if layer == 0:
            log_f = y
        else:
            la = jnp.log(lb)
            log_f = jnp.maximum(la, y) + jnp.log(1.0 + jnp.exp(-jnp.abs(la - y)))

        hi, mid, lo = _split3(log_f)
        g3 = jnp.dot(tri, jnp.concatenate([hi, mid, lo], axis=1), preferred_element_type=F32)
        G = g3[:, 0:D] + g3[:, D:2 * D] + g3[:, 2 * D:3 * D]
        vb = zv_ref[sl, :].astype(BF16)
        g_last = G[C - 1:C]

        a = jnp.zeros((C, C), F32)
        dec = None
        for dlt in range(c):
            if dlt == 0:
                w = q * k
            else:
                f_sh = f if dlt == 1 else _tile_roll(f, dlt - 1)
                dec = f_sh if dlt == 1 else dec * f_sh
                w = q * _tile_roll(k, dlt) * dec
            a = jnp.where(diag_masks[dlt], jnp.sum(w, axis=-1, keepdims=True), a)

        for m, mask in levels:
            g_bnd = jnp.concatenate(
                [jnp.broadcast_to(G[p * 2 * m + m - 1:p * 2 * m + m], (2 * m, D))
                 for p in range(C // (2 * m))], axis=0)
            e_ref = jnp.exp(-jnp.abs(G - g_bnd))
            sc = lax.dot_general((q * e_ref).astype(BF16), (k * e_ref).astype(BF16),
                                 (((1,), (1,)), ((), ())), preferred_element_type=F32)
            a = jnp.where(mask, sc, a)

        st = st_s[...]
        qg = (q * jnp.exp(G)).astype(BF16)
        o = (jnp.dot(a.astype(BF16), vb, preferred_element_type=F32)
             + lax.dot_general(qg, st.astype(BF16), (((1,), (1,)), ((), ())),
                               preferred_element_type=F32))
        kd = (k * jnp.exp(g_last - G)).astype(BF16)
        st_s[...] = st * jnp.exp(g_last) + lax.dot_general(
            vb, kd, (((0,), (0,)), ((), ())), preferred_element_type=F32)

        ms = jnp.mean(o * o, axis=-1, keepdims=True)
        zg = zg_ref[sl, :].astype(F32)
        o_ref[sl, :] = (o * lax.rsqrt(ms + NORM_EPS) * ng_ref[...]
                        * (zg * _sigmoid(zg))).astype(o_ref.dtype)


def _hgrn(z, lower_bounds, norm_gain, *, layer, batch, seq, heads, col0, rows=512):
    nblk = seq // rows
    D = HEAD_DIM
    depth = lower_bounds.shape[0]

    def zspec(group):
        off = col0 // D + group * heads
        return pl.BlockSpec((rows, D), lambda b, h, s, off=off: (b * nblk + s, off + h))

    return pl.pallas_call(
        functools.partial(_hgrn_kernel, rows=rows, layer=layer),
        grid=(batch, heads, nblk),
        in_specs=[zspec(0), zspec(1), zspec(2), zspec(3),
                  pl.BlockSpec((depth, D), lambda b, h, s: (0, h)),
                  pl.BlockSpec((1, D), lambda b, h, s: (0, 0))],
        out_specs=pl.BlockSpec((rows, D), lambda b, h, s: (b * nblk + s, h)),
        out_shape=jax.ShapeDtypeStruct((batch * seq, heads * D), BF16),
        scratch_shapes=[pltpu.VMEM((D, D), F32)],
        compiler_params=_params(("parallel", "parallel", "arbitrary")),
        name="hgrn2",
    )(z, z, z, z, lower_bounds, norm_gain.reshape(1, D))


def _qk_prep_kernel(zq_ref, zk_ref, zv_ref, cos_ref, sin_ref, qg_ref, kg_ref,
                    q_ref, k_ref, v_ref, *, nsub):
    D = HEAD_DIM
    cos = cos_ref[...]
    sin = sin_ref[...]

    def norm_rope(x, gain, scale):
        ms = jnp.mean(x * x, axis=-1, keepdims=True)
        xn = x * lax.rsqrt(ms + NORM_EPS) * gain
        return (xn * cos + pltpu.roll(xn, D // 2, 1) * sin) * scale

    for c in range(nsub):
        sl = slice(c * D, (c + 1) * D)
        q_ref[:, sl] = norm_rope(zq_ref[:, sl].astype(F32), qg_ref[...],
                                 D ** -0.5 * LOG2_E).astype(q_ref.dtype)
        k_ref[:, sl] = norm_rope(zk_ref[:, sl].astype(F32), kg_ref[...], 1.0).astype(k_ref.dtype)
    v_ref[...] = zv_ref[...].astype(v_ref.dtype)


def _qk_prep(z, cos, sin_signed, q_gain, k_gain, *, batch, seq, width, col0, rows=256):
    nblk = seq // rows
    D = HEAD_DIM
    cb = col0 // width
    zspec = lambda g: pl.BlockSpec((rows, width), lambda i, g=g: (i, cb + g))
    tab = pl.BlockSpec((rows, D), lambda i: (i % nblk, 0))
    vec = pl.BlockSpec((1, D), lambda i: (0, 0))
    out = pl.BlockSpec((rows, width), lambda i: (i, 0))
    shp = jax.ShapeDtypeStruct((batch * seq, width), BF16)
    return pl.pallas_call(
        functools.partial(_qk_prep_kernel, nsub=width // D),
        grid=(batch * nblk,),
        in_specs=[zspec(0), zspec(1), zspec(2), tab, tab, vec, vec],
        out_specs=[out, out, out],
        out_shape=[shp, shp, shp],
        compiler_params=_params(("parallel",)),
        name="qk_prep",
    )(z, z, z, cos, sin_signed, q_gain.reshape(1, D), k_gain.reshape(1, D))


def _flash_kernel(q_ref, k_ref, v_ref, lp_ref, g_ref, o_ref, m_s, l_s, acc_s, *, blk, lam_init):
    i = pl.program_id(2)
    D = HEAD_DIM
    lanes = blk // D

    m_s[...] = jnp.full_like(m_s, -jnp.inf)
    l_s[...] = jnp.zeros_like(l_s)
    acc_s[...] = jnp.zeros_like(acc_s)

    def block(kv_start, masked):
        k = k_ref[pl.ds(kv_start, blk), :]
        v = v_ref[pl.ds(kv_start, blk), :]
        for c in range(2):
            q = q_ref[:, c * D:(c + 1) * D]
            s = lax.dot_general(q, k[:, c * D:(c + 1) * D], (((1,), (1,)), ((), ())),
                                preferred_element_type=F32)
            if masked:
                row = lax.broadcasted_iota(jnp.int32, (blk, blk), 0)
                col = lax.broadcasted_iota(jnp.int32, (blk, blk), 1)
                s = jnp.where(col <= row, s, -jnp.inf)
            m_prev = m_s[c]
            m_new = jnp.maximum(m_prev, jnp.max(s, axis=-1, keepdims=True))
            alpha = jnp.exp2(m_prev - m_new)
            p = jnp.exp2(s - jnp.tile(m_new, (1, lanes)))
            l_s[c] = alpha * l_s[c] + jnp.sum(p, axis=-1, keepdims=True)
            acc_s[c] = jnp.tile(alpha, (1, 2)) * acc_s[c] + jnp.dot(
                p.astype(BF16), v, preferred_element_type=F32)
            m_s[c] = m_new

    def full_block(j, carry):
        block(pl.multiple_of(j * blk, blk), False)
        return carry

    lax.fori_loop(0, i, full_block, 0)
    block(pl.multiple_of(i * blk, blk), True)

    lp = lp_ref[...].astype(F32)
    lam = (jnp.exp(jnp.sum(lp[0:1] * lp[1:2], axis=-1, keepdims=True))
           - jnp.exp(jnp.sum(lp[2:3] * lp[3:4], axis=-1, keepdims=True)) + lam_init)
    o = (acc_s[0] / jnp.tile(l_s[0], (1, 2))) - lam * (acc_s[1] / jnp.tile(l_s[1], (1, 2)))
    ms = jnp.mean(o * o, axis=-1, keepdims=True)
    o_ref[...] = (o * lax.rsqrt(ms + NORM_EPS) * g_ref[...] * (1.0 - lam_init)).astype(o_ref.dtype)


def _flash(q, k, v, lam_params, subln_gain, *, layer, batch, seq, heads, blk=512):
    nblk = seq // blk
    D = HEAD_DIM
    lam_init = 0.8 - 0.6 * math.exp(-0.3 * layer)
    qspec = pl.BlockSpec((blk, 2 * D), lambda b, h, i: (b * nblk + i, h))
    kvspec = pl.BlockSpec((seq, 2 * D), lambda b, h, i: (b, h))
    return pl.pallas_call(
        functools.partial(_flash_kernel, blk=blk, lam_init=lam_init),
        grid=(batch, heads, nblk),
        in_specs=[qspec, kvspec, kvspec,
                  pl.BlockSpec((4, D), lambda b, h, i: (0, 0)),
                  pl.BlockSpec((1, 2 * D), lambda b, h, i: (0, 0))],
        out_specs=qspec,
        out_shape=jax.ShapeDtypeStruct((batch * seq, heads * 2 * D), BF16),
        scratch_shapes=[pltpu.VMEM((2, blk, D), F32), pltpu.VMEM((2, blk, D), F32),
                        pltpu.VMEM((2, blk, 2 * D), F32)],
        compiler_params=_params(("parallel", "parallel", "parallel")),
        name="diff_flash",
    )(q, k, v, lam_params, subln_gain.reshape(1, 2 * D))


def kernel(x, ln1_gain, w_in, conv_w, conv_b, lru_w_r, lru_b_r, lru_w_i, lru_b_i, lru_lambda, lru_norm_gain, hgrn_lower_bounds, hgrn_norm_gain, q_norm_gain, k_norm_gain, diff_lambda, diff_subln_gain, w_out, ln2_gain, w_ff1, w_ff2):
    batch, seq, d_model = x.shape
    depth = w_in.shape[0]
    D = HEAD_DIM
    lru_width = conv_w.shape[-1]
    key_width = hgrn_lower_bounds.shape[-1]
    in_width = w_in.shape[-1]
    mix_width = w_out.shape[1]
    diff_width = in_width - 2 * key_width - 2 * mix_width
    val_width = mix_width - lru_width - diff_width
    assert q_norm_gain.shape[-1] == D and hgrn_norm_gain.shape[-1] == D
    assert lru_w_r.shape[-1] == D and key_width == val_width
    hgrn_heads = key_width // D
    diff_heads = diff_width // (2 * D)
    off_b = 2 * lru_width
    off_c = off_b + 2 * key_width + 2 * val_width

    inv = 1.0 / (ROPE_THETA ** (jnp.arange(0, D, 2, dtype=F32) / D))
    ang = jnp.arange(seq, dtype=F32)[:, None] * inv[None, :]
    ang = jnp.concatenate([ang, ang], axis=-1)
    cos = jnp.cos(ang)
    sign = jnp.concatenate([-jnp.ones((D // 2,), F32), jnp.ones((D // 2,), F32)])
    sin_signed = jnp.sin(ang) * sign[None, :]

    xf = x.reshape(batch * seq, d_model)
    w_in_b, w_out_b = w_in.astype(BF16), w_out.astype(BF16)
    w_ff1_b, w_ff2_b = w_ff1.astype(BF16), w_ff2.astype(BF16)
    xg, ss = _rmsnorm(xf, ln1_gain[0]), None
    for l in range(depth):
        z = _matmul([xg], w_in_b, l, row_ss=ss)

        ya = _lru(z, conv_w[l], conv_b[l], lru_w_r[l], lru_b_r[l], lru_w_i[l], lru_b_i[l],
                  lru_lambda[l], lru_norm_gain[l], batch=batch, seq=seq)
        yb = _hgrn(z, hgrn_lower_bounds, hgrn_norm_gain[l], layer=l, batch=batch, seq=seq,
                   heads=hgrn_heads, col0=off_b)
        qn, kn, vn = _qk_prep(z, cos, sin_signed, q_norm_gain[l], k_norm_gain[l],
                              batch=batch, seq=seq, width=diff_width, col0=off_c)
        yc = _flash(qn, kn, vn, diff_lambda[l], diff_subln_gain[l], layer=l, batch=batch, seq=seq,
                    heads=diff_heads)

        xf, xg, ss = _matmul([ya, yb, yc], w_out_b, l, mode="residual", residual=xf,
                             next_gain=ln2_gain[l], tn=512)
        u = _matmul([xg], w_ff1_b, l, mode="relu2", row_ss=ss, out_dtype=BF16)
        if l + 1 < depth:
            xf, xg, ss = _matmul([u], w_ff2_b, l, mode="residual", residual=xf,
                                 next_gain=ln1_gain[l + 1])
        else:
            xf = _matmul([u], w_ff2_b, l, mode="residual", residual=xf)
    return xf.reshape(batch, seq, d_model)
```

```python
import functools
import math

import jax
import jax.numpy as jnp
from jax import lax
from jax.experimental import pallas as pl
from jax.experimental.pallas import tpu as pltpu

F32 = jnp.float32
BF16 = jnp.bfloat16

NORM_EPS = 1e-6
LRU_C = 8.0
ROPE_THETA = 10000.0
CONV_WIDTH = 4
LOG2_E = 1.4426950408889634
HEAD_DIM = 128
HGRN_CHUNK = 128
HGRN_SUB = 8
VMEM_LIMIT = 56 * 1024 * 1024


def _params(semantics):
    return pltpu.CompilerParams(dimension_semantics=semantics, vmem_limit_bytes=VMEM_LIMIT)


def _sigmoid(x):
    return 1.0 / (1.0 + jnp.exp(-x))


def _log_sigmoid(x):
    return jnp.minimum(x, 0.0) - jnp.log1p(jnp.exp(-jnp.abs(x)))


def _rmsnorm_kernel(x_ref, g_ref, o_ref):
    x = x_ref[...]
    ms = jnp.mean(x * x, axis=-1, keepdims=True)
    o_ref[...] = (x * lax.rsqrt(ms + NORM_EPS) * g_ref[...]).astype(o_ref.dtype)


def _rmsnorm(x, gain, tm=256):
    n, d = x.shape
    return pl.pallas_call(
        _rmsnorm_kernel,
        grid=(n // tm,),
        in_specs=[pl.BlockSpec((tm, d), lambda i: (i, 0)),
                  pl.BlockSpec((1, d), lambda i: (0, 0))],
        out_specs=pl.BlockSpec((tm, d), lambda i: (i, 0)),
        out_shape=jax.ShapeDtypeStruct((n, d), BF16),
        compiler_params=_params(("parallel",)),
        name="rmsnorm",
    )(x, gain.reshape(1, d))


LANES = 128


def _row_factor(ss_ref, width, tn):
    rstd = lax.rsqrt(ss_ref[...] * (1.0 / width) + NORM_EPS)
    return jnp.tile(rstd, (1, tn // LANES))


def _finish(acc, j, *, mode, r_ref, g_ref, o_ref, xg_ref, ss_ref):
    if mode == "relu2":
        acc = jnp.square(jnp.maximum(acc, 0.0))
    elif mode == "residual":
        acc = acc + r_ref[...]
    o_ref[...] = acc.astype(o_ref.dtype)
    if xg_ref is not None:
        xg_ref[...] = (acc * g_ref[...]).astype(xg_ref.dtype)

        @pl.when(j == 0)
        def _():
            ss_ref[...] = jnp.zeros_like(ss_ref)

        ss_ref[...] += jnp.sum(acc * acc, axis=-1, keepdims=True)


def _unpack(refs, n_a, row_scale, mode, emit_norm):
    it = iter(refs)
    a_refs = [next(it) for _ in range(n_a)]
    w_ref = next(it)
    ss_in = next(it) if row_scale else None
    r_ref = next(it) if mode == "residual" else None
    g_ref = next(it) if emit_norm else None
    o_ref = next(it)
    xg_ref = next(it) if emit_norm else None
    ss_ref = next(it) if emit_norm else None
    return a_refs, w_ref, ss_in, r_ref, g_ref, o_ref, xg_ref, ss_ref, list(it)


def _mm_fullk_kernel(*refs, splits, mode, row_scale, emit_norm):
    a_refs, w_ref, ss_in, r_ref, g_ref, o_ref, xg_ref, ss_ref, _ = _unpack(
        refs, len(splits), row_scale, mode, emit_norm)
    acc, off = None, 0
    for a_ref, kp in zip(a_refs, splits):
        part = jnp.dot(a_ref[...], w_ref[off:off + kp, :], preferred_element_type=F32)
        acc = part if acc is None else acc + part
        off += kp
    if row_scale:
        acc = acc * _row_factor(ss_in, off, o_ref.shape[-1])
    _finish(acc, pl.program_id(1), mode=mode, r_ref=r_ref, g_ref=g_ref, o_ref=o_ref,
            xg_ref=xg_ref, ss_ref=ss_ref)


def _mm_kloop_kernel(*refs, nk, mode, emit_norm):
    a_refs, w_ref, _, r_ref, g_ref, o_ref, xg_ref, ss_ref, (acc_ref,) = _unpack(
        refs, 1, False, mode, emit_norm)
    k = pl.program_id(2)

    @pl.when(k == 0)
    def _():
        acc_ref[...] = jnp.zeros_like(acc_ref)

    acc_ref[...] += jnp.dot(a_refs[0][...], w_ref[...], preferred_element_type=F32)

    @pl.when(k == nk - 1)
    def _():
        _finish(acc_ref[...], pl.program_id(1), mode=mode, r_ref=r_ref, g_ref=g_ref, o_ref=o_ref,
                xg_ref=xg_ref, ss_ref=ss_ref)


def _matmul(a_parts, w_stack, layer, *, mode="plain", residual=None, row_ss=None, next_gain=None,
            out_dtype=F32, tm=1024, tn=1024, tk=2048):
    m = a_parts[0].shape[0]
    splits = tuple(p.shape[1] for p in a_parts)
    kdim = sum(splits)
    n = w_stack.shape[-1]
    fullk = kdim <= 4096
    row_scale = row_ss is not None
    emit_norm = next_gain is not None
    tm, tn = min(tm, m), min(tn, n)

    if fullk:
        grid = (m // tm, n // tn)
        tile = lambda i, j: (i, j)
        rows = lambda i, j: (i, 0)
        cols = lambda i, j: (0, j)
        in_specs = [pl.BlockSpec((tm, kp), rows) for kp in splits]
        in_specs.append(pl.BlockSpec((None, kdim, tn), lambda i, j: (layer, 0, j)))
        body = functools.partial(_mm_fullk_kernel, splits=splits, mode=mode, row_scale=row_scale,
                                 emit_norm=emit_norm)
        scratch = []
        semantics = ("parallel", "arbitrary")
    else:
        assert len(a_parts) == 1 and not row_scale
        tk = min(tk, kdim)
        nk = kdim // tk
        grid = (m // tm, n // tn, nk)
        tile = lambda i, j, k: (i, j)
        rows = lambda i, j, k: (i, 0)
        cols = lambda i, j, k: (0, j)
        in_specs = [pl.BlockSpec((tm, tk), lambda i, j, k: (i, k)),
                    pl.BlockSpec((None, tk, tn), lambda i, j, k: (layer, k, j))]
        body = functools.partial(_mm_kloop_kernel, nk=nk, mode=mode, emit_norm=emit_norm)
        scratch = [pltpu.VMEM((tm, tn), F32)]
        semantics = ("parallel", "arbitrary", "arbitrary")

    args = list(a_parts) + [w_stack]
    if row_scale:
        in_specs.append(pl.BlockSpec((tm, LANES), rows))
        args.append(row_ss)
    if mode == "residual":
        in_specs.append(pl.BlockSpec((tm, tn), tile))
        args.append(residual)
    out_specs = [pl.BlockSpec((tm, tn), tile)]
    out_shape = [jax.ShapeDtypeStruct((m, n), out_dtype)]
    if emit_norm:
        in_specs.append(pl.BlockSpec((1, tn), cols))
        args.append(next_gain.reshape(1, n))
        out_specs += [pl.BlockSpec((tm, tn), tile), pl.BlockSpec((tm, LANES), rows)]
        out_shape += [jax.ShapeDtypeStruct((m, n), BF16), jax.ShapeDtypeStruct((m, LANES), F32)]
    outs = pl.pallas_call(
        body, grid=grid, in_specs=in_specs, out_specs=out_specs, out_shape=out_shape,
        scratch_shapes=scratch, compiler_params=_params(semantics),
        name=("matmul_fullk_" if fullk else "matmul_kloop_") + mode,
    )(*args)
    return outs if emit_norm else outs[0]


def _lru_kernel(zg_ref, zx_ref, cw_ref, cb_ref, wr_ref, br_ref, wi_ref, bi_ref, lam_ref, ng_ref,
                o_ref, xprev_ref, hprev_ref, *, rows, heads):
    s = pl.program_id(1)
    width = heads * HEAD_DIM

    @pl.when(s == 0)
    def _():
        xprev_ref[...] = jnp.zeros_like(xprev_ref)
        hprev_ref[...] = jnp.zeros_like(hprev_ref)

    xz = zx_ref[...].astype(F32)
    prev8 = xprev_ref[...]
    row8 = lax.broadcasted_iota(jnp.int32, (8, width), 0)
    cw = cw_ref[...]
    xa = cw[CONV_WIDTH - 1:CONV_WIDTH] * xz + cb_ref[...]
    for d in range(1, CONV_WIDTH):
        rolled = pltpu.roll(xz, d, 0)
        head = jnp.where(row8 < d, pltpu.roll(prev8, d, 0), rolled[0:8])
        shifted = jnp.concatenate([head, rolled[8:]], axis=0)
        xa = xa + cw[CONV_WIDTH - 1 - d:CONV_WIDTH - d] * shifted
    xprev_ref[...] = xz[rows - 8:rows]

    xab = xa.astype(BF16)
    r_parts, i_parts = [], []
    for h in range(heads):
        xh = xab[:, h * HEAD_DIM:(h + 1) * HEAD_DIM]
        r_parts.append(jnp.dot(xh, wr_ref[h], preferred_element_type=F32))
        i_parts.append(jnp.dot(xh, wi_ref[h], preferred_element_type=F32))
    r = _sigmoid(jnp.concatenate(r_parts, axis=1) + br_ref[...])
    gate_i = _sigmoid(jnp.concatenate(i_parts, axis=1) + bi_ref[...])

    lam = lam_ref[...]
    softplus_neg_lam = jnp.maximum(-lam, 0.0) + jnp.log1p(jnp.exp(-jnp.abs(lam)))
    log_a = (-LRU_C) * r * softplus_neg_lam
    a = jnp.exp(log_a)
    mult = jnp.sqrt(1.0 - jnp.exp(2.0 * log_a))
    row = lax.broadcasted_iota(jnp.int32, (rows, width), 0)
    mult = jnp.where(jnp.logical_and(row == 0, s == 0), 1.0, mult)
    u = mult * (gate_i * xa)

    d = 1
    while d < rows:
        a_sh = pltpu.roll(a, d, 0)
        u_sh = pltpu.roll(u, d, 0)
        valid = row >= d
        u = jnp.where(valid, u + a * u_sh, u)
        a = jnp.where(valid, a * a_sh, a)
        d *= 2
    h = u + a * hprev_ref[0:1, :]
    hprev_ref[...] = jnp.broadcast_to(h[rows - 1:rows], hprev_ref.shape)

    zg = zg_ref[...].astype(F32)
    gelu = 0.5 * zg * (1.0 + jnp.tanh(math.sqrt(2.0 / math.pi) * (zg + 0.044715 * (zg * zg * zg))))
    y = h * gelu
    ms = jnp.mean(y * y, axis=-1, keepdims=True)
    o_ref[...] = (y * lax.rsqrt(ms + NORM_EPS) * ng_ref[...]).astype(o_ref.dtype)


def _lru(z, conv_w, conv_b, w_r, b_r, w_i, b_i, lam, norm_gain, *, batch, seq, rows=256):
    heads = w_r.shape[0]
    width = heads * HEAD_DIM
    nblk = seq // rows
    row_spec = lambda c: pl.BlockSpec((rows, width), lambda b, s, c=c: (b * nblk + s, c))
    vec = pl.BlockSpec((1, width), lambda b, s: (0, 0))
    wspec = pl.BlockSpec((heads, HEAD_DIM, HEAD_DIM), lambda b, s: (0, 0, 0))
    return pl.pallas_call(
        functools.partial(_lru_kernel, rows=rows, heads=heads),
        grid=(batch, nblk),
        in_specs=[row_spec(0), row_spec(1),
                  pl.BlockSpec((CONV_WIDTH, width), lambda b, s: (0, 0)), vec,
                  wspec, vec, wspec, vec, vec, vec],
        out_specs=pl.BlockSpec((rows, width), lambda b, s: (b * nblk + s, 0)),
        out_shape=jax.ShapeDtypeStruct((batch * seq, width), BF16),
        scratch_shapes=[pltpu.VMEM((8, width), F32), pltpu.VMEM((8, width), F32)],
        compiler_params=_params(("parallel", "arbitrary")),
        name="rg_lru",
    )(z, z, conv_w, conv_b.reshape(1, width), w_r.astype(BF16), b_r.reshape(1, width),
      w_i.astype(BF16), b_i.reshape(1, width), lam.reshape(1, width), norm_gain.reshape(1, width))


def _split3(x):
    hi = x.astype(BF16)
    r1 = x - hi.astype(F32)
    mid = r1.astype(BF16)
    lo = (r1 - mid.astype(F32)).astype(BF16)
    return hi, mid, lo


def _tile_roll(x, shift):
    rows, d = x.shape
    x3 = x.reshape(rows // HGRN_SUB, HGRN_SUB, d)
    return pltpu.roll(x3, shift, 1).reshape(rows, d)


def _hgrn_kernel(zq_ref, zf_ref, zv_ref, zg_ref, lbp_ref, ng_ref, o_ref, st_s, *, rows, layer):
    s = pl.program_id(2)
    C, c = HGRN_CHUNK, HGRN_SUB
    D = HEAD_DIM

    @pl.when(s == 0)
    def _():
        st_s[...] = jnp.zeros_like(st_s)

    lbp = lbp_ref[...].astype(F32)
    e = jnp.exp(lbp - jnp.max(lbp, axis=0, keepdims=True))
    sm = e / jnp.sum(e, axis=0, keepdims=True)
    lb = jnp.zeros((1, D), F32)
    for r_ in range(1, layer + 1):
        lb = lb + sm[r_:r_ + 1]

    row = lax.broadcasted_iota(jnp.int32, (C, C), 0)
    col = lax.broadcasted_iota(jnp.int32, (C, C), 1)
    tri = (row >= col).astype(BF16)
    differ = jnp.bitwise_xor(row, col)
    levels = []
    m = C // 2
    while m >= c:
        levels.append((m, (row > col) & (differ >= m) & (differ < 2 * m)))
        m //= 2
    diag_masks = [(differ < c) & (row - col == dlt) for dlt in range(c)]

    for ch in range(rows // C):
        sl = slice(ch * C, (ch + 1) * C)
        zq = zq_ref[sl, :].astype(F32)
        zf = zf_ref[sl, :].astype(F32)
        q = zq * _sigmoid(zq) * (D ** -0.5)
        ez = jnp.exp(-jnp.abs(zf))
        inv = 1.0 / (1.0 + ez)
        sig_pos = jnp.where(zf >= 0.0, inv, ez * inv)
        sig_neg = jnp.where(zf >= 0.0, ez * inv, inv)
        f = lb + (1.0 - lb) * sig_pos
        k = (1.0 - lb) * sig_neg
        y = jnp.log1p(-lb) + (jnp.minimum(zf, 0.0) - jnp.log(1.0 + ez))
        if layer == 0:
            log_f = y
        else:
            la = jnp.log(lb)
            log_f = jnp.maximum(la, y) + jnp.log(1.0 + jnp.exp(-jnp.abs(la - y)))

        hi, mid, lo = _split3(log_f)
        g3 = jnp.dot(tri, jnp.concatenate([hi, mid, lo], axis=1), preferred_element_type=F32)
        G = g3[:, 0:D] + g3[:, D:2 * D] + g3[:, 2 * D:3 * D]
        vb = zv_ref[sl, :].astype(BF16)
        g_last = G[C - 1:C]

        a = jnp.zeros((C, C), F32)
        dec = None
        for dlt in range(c):
            if dlt == 0:
                w = q * k
            else:
                f_sh = f if dlt == 1 else _tile_roll(f, dlt - 1)
                dec = f_sh if dlt == 1 else dec * f_sh
                w = q * _tile_roll(k, dlt) * dec
            a = jnp.where(diag_masks[dlt], jnp.sum(w, axis=-1, keepdims=True), a)

        for m, mask in levels:
            g_bnd = jnp.concatenate(
                [jnp.broadcast_to(G[p * 2 * m + m - 1:p * 2 * m + m], (2 * m, D))
                 for p in range(C // (2 * m))], axis=0)
            e_ref = jnp.exp(-jnp.abs(G - g_bnd))
            sc = lax.dot_general((q * e_ref).astype(BF16), (k * e_ref).astype(BF16),
                                 (((1,), (1,)), ((), ())), preferred_element_type=F32)
            a = jnp.where(mask, sc, a)

        st = st_s[...]
        qg = (q * jnp.exp(G)).astype(BF16)
        o = (jnp.dot(a.astype(BF16), vb, preferred_element_type=F32)
             + lax.dot_general(qg, st.astype(BF16), (((1,), (1,)), ((), ())),
                               preferred_element_type=F32))
        kd = (k * jnp.exp(g_last - G)).astype(BF16)
        st_s[...] = st * jnp.exp(g_last) + lax.dot_general(
            vb, kd, (((0,), (0,)), ((), ())), preferred_element_type=F32)

        ms = jnp.mean(o * o, axis=-1, keepdims=True)
        zg = zg_ref[sl, :].astype(F32)
        o_ref[sl, :] = (o * lax.rsqrt(ms + NORM_EPS) * ng_ref[...]
                        * (zg * _sigmoid(zg))).astype(o_ref.dtype)


def _hgrn(z, lower_bounds, norm_gain, *, layer, batch, seq, heads, col0, rows=512):
    nblk = seq // rows
    D = HEAD_DIM
    depth = lower_bounds.shape[0]

    def zspec(group):
        off = col0 // D + group * heads
        return pl.BlockSpec((rows, D), lambda b, h, s, off=off: (b * nblk + s, off + h))

    return pl.pallas_call(
        functools.partial(_hgrn_kernel, rows=rows, layer=layer),
        grid=(batch, heads, nblk),
        in_specs=[zspec(0), zspec(1), zspec(2), zspec(3),
                  pl.BlockSpec((depth, D), lambda b, h, s: (0, h)),
                  pl.BlockSpec((1, D), lambda b, h, s: (0, 0))],
        out_specs=pl.BlockSpec((rows, D), lambda b, h, s: (b * nblk + s, h)),
        out_shape=jax.ShapeDtypeStruct((batch * seq, heads * D), BF16),
        scratch_shapes=[pltpu.VMEM((D, D), F32)],
        compiler_params=_params(("parallel", "parallel", "arbitrary")),
        name="hgrn2",
    )(z, z, z, z, lower_bounds, norm_gain.reshape(1, D))


def _qk_prep_kernel(zq_ref, zk_ref, zv_ref, cos_ref, sin_ref, qg_ref, kg_ref,
                    q_ref, k_ref, v_ref, *, nsub):
    D = HEAD_DIM
    cos = cos_ref[...]
    sin = sin_ref[...]

    def norm_rope(x, gain, scale):
        ms = jnp.mean(x * x, axis=-1, keepdims=True)
        xn = x * lax.rsqrt(ms + NORM_EPS) * gain
        return (xn * cos + pltpu.roll(xn, D // 2, 1) * sin) * scale

    for c in range(nsub):
        sl = slice(c * D, (c + 1) * D)
        q_ref[:, sl] = norm_rope(zq_ref[:, sl].astype(F32), qg_ref[...],
                                 D ** -0.5 * LOG2_E).astype(q_ref.dtype)
        k_ref[:, sl] = norm_rope(zk_ref[:, sl].astype(F32), kg_ref[...], 1.0).astype(k_ref.dtype)
    v_ref[...] = zv_ref[...].astype(v_ref.dtype)


def _qk_prep(z, cos, sin_signed, q_gain, k_gain, *, batch, seq, width, col0, rows=256):
    nblk = seq // rows
    D = HEAD_DIM
    cb = col0 // width
    zspec = lambda g: pl.BlockSpec((rows, width), lambda i, g=g: (i, cb + g))
    tab = pl.BlockSpec((rows, D), lambda i: (i % nblk, 0))
    vec = pl.BlockSpec((1, D), lambda i: (0, 0))
    out = pl.BlockSpec((rows, width), lambda i: (i, 0))
    shp = jax.ShapeDtypeStruct((batch * seq, width), BF16)
    return pl.pallas_call(
        functools.partial(_qk_prep_kernel, nsub=width // D),
        grid=(batch * nblk,),
        in_specs=[zspec(0), zspec(1), zspec(2), tab, tab, vec, vec],
        out_specs=[out, out, out],
        out_shape=[shp, shp, shp],
        compiler_params=_params(("parallel",)),
        name="qk_prep",
    )(z, z, z, cos, sin_signed, q_gain.reshape(1, D), k_gain.reshape(1, D))


def _flash_kernel(q_ref, k_ref, v_ref, lp_ref, g_ref, o_ref, m_s, l_s, acc_s, a_s, p_s, *, blk,
                  lam_init):
    i = pl.program_id(2)
    D = HEAD_DIM
    lanes = blk // D

    m_s[...] = jnp.full_like(m_s, -jnp.inf)
    l_s[...] = jnp.zeros_like(l_s)
    acc_s[...] = jnp.zeros_like(acc_s)

    def scores(t, masked):
        k = k_ref[pl.ds(pl.multiple_of(t * blk, blk), blk), :]
        for c in range(2):
            q = q_ref[:, c * D:(c + 1) * D]
            s = lax.dot_general(q, k[:, c * D:(c + 1) * D], (((1,), (1,)), ((), ())),
                                preferred_element_type=F32)
            if masked:
                row = lax.broadcasted_iota(jnp.int32, (blk, blk), 0)
                col = lax.broadcasted_iota(jnp.int32, (blk, blk), 1)
                s = jnp.where(col <= row, s, -jnp.inf)
            m_prev = m_s[c]
            m_new = jnp.maximum(m_prev, jnp.max(s, axis=-1, keepdims=True))
            alpha = jnp.exp2(m_prev - m_new)
            p = jnp.exp2(s - jnp.tile(m_new, (1, lanes)))
            l_s[c] = alpha * l_s[c] + jnp.sum(p, axis=-1, keepdims=True)
            m_s[c] = m_new
            a_s[c] = alpha
            p_s[c] = p.astype(BF16)

    def weighted_values(t):
        v = v_ref[pl.ds(pl.multiple_of(t * blk, blk), blk), :]
        for c in range(2):
            acc_s[c] = jnp.tile(a_s[c], (1, 2)) * acc_s[c] + jnp.dot(
                p_s[c], v, preferred_element_type=F32)

    @pl.when(i == 0)
    def _():
        scores(0, True)

    @pl.when(i > 0)
    def _():
        scores(0, False)

        def pipelined(t, carry):
            weighted_values(t - 1)
            scores(t, False)
            return carry

        lax.fori_loop(1, i, pipelined, 0)
        weighted_values(i - 1)
        scores(i, True)

    weighted_values(i)

    lp = lp_ref[...].astype(F32)
    lam = (jnp.exp(jnp.sum(lp[0:1] * lp[1:2], axis=-1, keepdims=True))
           - jnp.exp(jnp.sum(lp[2:3] * lp[3:4], axis=-1, keepdims=True)) + lam_init)
    o = (acc_s[0] / jnp.tile(l_s[0], (1, 2))) - lam * (acc_s[1] / jnp.tile(l_s[1], (1, 2)))
    ms = jnp.mean(o * o, axis=-1, keepdims=True)
    o_ref[...] = (o * lax.rsqrt(ms + NORM_EPS) * g_ref[...] * (1.0 - lam_init)).astype(o_ref.dtype)


def _flash(q, k, v, lam_params, subln_gain, *, layer, batch, seq, heads, blk=512):
    nblk = seq // blk
    D = HEAD_DIM
    lam_init = 0.8 - 0.6 * math.exp(-0.3 * layer)
    qspec = pl.BlockSpec((blk, 2 * D), lambda b, h, i: (b * nblk + i, h))
    kvspec = pl.BlockSpec((seq, 2 * D), lambda b, h, i: (b, h))
    return pl.pallas_call(
        functools.partial(_flash_kernel, blk=blk, lam_init=lam_init),
        grid=(batch, heads, nblk),
        in_specs=[qspec, kvspec, kvspec,
                  pl.BlockSpec((4, D), lambda b, h, i: (0, 0)),
                  pl.BlockSpec((1, 2 * D), lambda b, h, i: (0, 0))],
        out_specs=qspec,
        out_shape=jax.ShapeDtypeStruct((batch * seq, heads * 2 * D), BF16),
        scratch_shapes=[pltpu.VMEM((2, blk, D), F32), pltpu.VMEM((2, blk, D), F32),
                        pltpu.VMEM((2, blk, 2 * D), F32), pltpu.VMEM((2, blk, D), F32),
                        pltpu.VMEM((2, blk, blk), BF16)],
        compiler_params=_params(("parallel", "parallel", "parallel")),
        name="diff_flash",
    )(q, k, v, lam_params, subln_gain.reshape(1, 2 * D))


def kernel(x, ln1_gain, w_in, conv_w, conv_b, lru_w_r, lru_b_r, lru_w_i, lru_b_i, lru_lambda, lru_norm_gain, hgrn_lower_bounds, hgrn_norm_gain, q_norm_gain, k_norm_gain, diff_lambda, diff_subln_gain, w_out, ln2_gain, w_ff1, w_ff2):
    batch, seq, d_model = x.shape
    depth = w_in.shape[0]
    D = HEAD_DIM
    lru_width = conv_w.shape[-1]
    key_width = hgrn_lower_bounds.shape[-1]
    in_width = w_in.shape[-1]
    mix_width = w_out.shape[1]
    diff_width = in_width - 2 * key_width - 2 * mix_width
    val_width = mix_width - lru_width - diff_width
    assert q_norm_gain.shape[-1] == D and hgrn_norm_gain.shape[-1] == D
    assert lru_w_r.shape[-1] == D and key_width == val_width
    hgrn_heads = key_width // D
    diff_heads = diff_width // (2 * D)
    off_b = 2 * lru_width
    off_c = off_b + 2 * key_width + 2 * val_width

    inv = 1.0 / (ROPE_THETA ** (jnp.arange(0, D, 2, dtype=F32) / D))
    ang = jnp.arange(seq, dtype=F32)[:, None] * inv[None, :]
    ang = jnp.concatenate([ang, ang], axis=-1)
    cos = jnp.cos(ang)
    sign = jnp.concatenate([-jnp.ones((D // 2,), F32), jnp.ones((D // 2,), F32)])
    sin_signed = jnp.sin(ang) * sign[None, :]

    xf = x.reshape(batch * seq, d_model)
    w_in_b, w_out_b = w_in.astype(BF16), w_out.astype(BF16)
    w_ff1_b, w_ff2_b = w_ff1.astype(BF16), w_ff2.astype(BF16)
    xg, ss = _rmsnorm(xf, ln1_gain[0]), None
    for l in range(depth):
        z = _matmul([xg], w_in_b, l, row_ss=ss)

        ya = _lru(z, conv_w[l], conv_b[l], lru_w_r[l], lru_b_r[l], lru_w_i[l], lru_b_i[l],
                  lru_lambda[l], lru_norm_gain[l], batch=batch, seq=seq)
        yb = _hgrn(z, hgrn_lower_bounds, hgrn_norm_gain[l], layer=l, batch=batch, seq=seq,
                   heads=hgrn_heads, col0=off_b)
        qn, kn, vn = _qk_prep(z, cos, sin_signed, q_norm_gain[l], k_norm_gain[l],
                              batch=batch, seq=seq, width=diff_width, col0=off_c)
        yc = _flash(qn, kn, vn, diff_lambda[l], diff_subln_gain[l], layer=l, batch=batch, seq=seq,
                    heads=diff_heads)

        xf, xg, ss = _matmul([ya, yb, yc], w_out_b, l, mode="residual", residual=xf,
                             next_gain=ln2_gain[l], tn=512)
        u = _matmul([xg], w_ff1_b, l, mode="relu2", row_ss=ss, out_dtype=BF16)
        if l + 1 < depth:
            xf, xg, ss = _matmul([u], w_ff2_b, l, mode="residual", residual=xf,
                                 next_gain=ln1_gain[l + 1])
        else:
            xf = _matmul([u], w_ff2_b, l, mode="residual", residual=xf)
    return xf.reshape(batch, seq, d_model)
```

```python
import functools
import math

import jax
import jax.numpy as jnp
from jax import lax
from jax.experimental import pallas as pl
from jax.experimental.pallas import tpu as pltpu

F32 = jnp.float32
BF16 = jnp.bfloat16

NORM_EPS = 1e-6
LRU_C = 8.0
ROPE_THETA = 10000.0
CONV_WIDTH = 4
LOG2_E = 1.4426950408889634
HEAD_DIM = 128
HGRN_CHUNK = 128
HGRN_SUB = 8
VMEM_LIMIT = 56 * 1024 * 1024


def _params(semantics):
    return pltpu.CompilerParams(dimension_semantics=semantics, vmem_limit_bytes=VMEM_LIMIT)


def _sigmoid(x):
    return 1.0 / (1.0 + jnp.exp(-x))


def _log_sigmoid(x):
    return jnp.minimum(x, 0.0) - jnp.log1p(jnp.exp(-jnp.abs(x)))


def _rmsnorm_kernel(x_ref, g_ref, o_ref):
    x = x_ref[...]
    ms = jnp.mean(x * x, axis=-1, keepdims=True)
    o_ref[...] = (x * lax.rsqrt(ms + NORM_EPS) * g_ref[...]).astype(o_ref.dtype)


def _rmsnorm(x, gain, tm=256):
    n, d = x.shape
    return pl.pallas_call(
        _rmsnorm_kernel,
        grid=(n // tm,),
        in_specs=[pl.BlockSpec((tm, d), lambda i: (i, 0)),
                  pl.BlockSpec((1, d), lambda i: (0, 0))],
        out_specs=pl.BlockSpec((tm, d), lambda i: (i, 0)),
        out_shape=jax.ShapeDtypeStruct((n, d), BF16),
        compiler_params=_params(("parallel",)),
        name="rmsnorm",
    )(x, gain.reshape(1, d))


LANES = 128


def _row_factor(ss_ref, width, tn):
    rstd = lax.rsqrt(ss_ref[...] * (1.0 / width) + NORM_EPS)
    return jnp.tile(rstd, (1, tn // LANES))


def _finish(acc, j, *, mode, r_ref, g_ref, o_ref, xg_ref, ss_ref):
    if mode == "relu2":
        acc = jnp.square(jnp.maximum(acc, 0.0))
    elif mode == "residual":
        acc = acc + r_ref[...]
    o_ref[...] = acc.astype(o_ref.dtype)
    if xg_ref is not None:
        xg_ref[...] = (acc * g_ref[...]).astype(xg_ref.dtype)

        @pl.when(j == 0)
        def _():
            ss_ref[...] = jnp.zeros_like(ss_ref)

        ss_ref[...] += jnp.sum(acc * acc, axis=-1, keepdims=True)


def _unpack(refs, n_a, row_scale, mode, emit_norm):
    it = iter(refs)
    a_refs = [next(it) for _ in range(n_a)]
    w_ref = next(it)
    ss_in = next(it) if row_scale else None
    r_ref = next(it) if mode == "residual" else None
    g_ref = next(it) if emit_norm else None
    o_ref = next(it)
    xg_ref = next(it) if emit_norm else None
    ss_ref = next(it) if emit_norm else None
    return a_refs, w_ref, ss_in, r_ref, g_ref, o_ref, xg_ref, ss_ref, list(it)


def _mm_fullk_kernel(*refs, splits, mode, row_scale, emit_norm):
    a_refs, w_ref, ss_in, r_ref, g_ref, o_ref, xg_ref, ss_ref, _ = _unpack(
        refs, len(splits), row_scale, mode, emit_norm)
    acc, off = None, 0
    for a_ref, kp in zip(a_refs, splits):
        part = jnp.dot(a_ref[...], w_ref[off:off + kp, :], preferred_element_type=F32)
        acc = part if acc is None else acc + part
        off += kp
    if row_scale:
        acc = acc * _row_factor(ss_in, off, o_ref.shape[-1])
    _finish(acc, pl.program_id(1), mode=mode, r_ref=r_ref, g_ref=g_ref, o_ref=o_ref,
            xg_ref=xg_ref, ss_ref=ss_ref)


def _mm_kloop_kernel(*refs, nk, mode, emit_norm):
    a_refs, w_ref, _, r_ref, g_ref, o_ref, xg_ref, ss_ref, (acc_ref,) = _unpack(
        refs, 1, False, mode, emit_norm)
    k = pl.program_id(2)

    @pl.when(k == 0)
    def _():
        acc_ref[...] = jnp.zeros_like(acc_ref)

    acc_ref[...] += jnp.dot(a_refs[0][...], w_ref[...], preferred_element_type=F32)

    @pl.when(k == nk - 1)
    def _():
        _finish(acc_ref[...], pl.program_id(1), mode=mode, r_ref=r_ref, g_ref=g_ref, o_ref=o_ref,
                xg_ref=xg_ref, ss_ref=ss_ref)


def _matmul(a_parts, w_stack, layer, *, mode="plain", residual=None, row_ss=None, next_gain=None,
            out_dtype=F32, tm=1024, tn=1024, tk=2048):
    m = a_parts[0].shape[0]
    splits = tuple(p.shape[1] for p in a_parts)
    kdim = sum(splits)
    n = w_stack.shape[-1]
    fullk = kdim <= 4096
    row_scale = row_ss is not None
    emit_norm = next_gain is not None
    tm, tn = min(tm, m), min(tn, n)

    if fullk:
        grid = (m // tm, n // tn)
        tile = lambda i, j: (i, j)
        rows = lambda i, j: (i, 0)
        cols = lambda i, j: (0, j)
        in_specs = [pl.BlockSpec((tm, kp), rows) for kp in splits]
        in_specs.append(pl.BlockSpec((None, kdim, tn), lambda i, j: (layer, 0, j)))
        body = functools.partial(_mm_fullk_kernel, splits=splits, mode=mode, row_scale=row_scale,
                                 emit_norm=emit_norm)
        scratch = []
        semantics = ("parallel", "arbitrary")
    else:
        assert len(a_parts) == 1 and not row_scale
        tk = min(tk, kdim)
        nk = kdim // tk
        grid = (m // tm, n // tn, nk)
        tile = lambda i, j, k: (i, j)
        rows = lambda i, j, k: (i, 0)
        cols = lambda i, j, k: (0, j)
        in_specs = [pl.BlockSpec((tm, tk), lambda i, j, k: (i, k)),
                    pl.BlockSpec((None, tk, tn), lambda i, j, k: (layer, k, j))]
        body = functools.partial(_mm_kloop_kernel, nk=nk, mode=mode, emit_norm=emit_norm)
        scratch = [pltpu.VMEM((tm, tn), F32)]
        semantics = ("parallel", "arbitrary", "arbitrary")

    args = list(a_parts) + [w_stack]
    if row_scale:
        in_specs.append(pl.BlockSpec((tm, LANES), rows))
        args.append(row_ss)
    if mode == "residual":
        in_specs.append(pl.BlockSpec((tm, tn), tile))
        args.append(residual)
    out_specs = [pl.BlockSpec((tm, tn), tile)]
    out_shape = [jax.ShapeDtypeStruct((m, n), out_dtype)]
    if emit_norm:
        in_specs.append(pl.BlockSpec((1, tn), cols))
        args.append(next_gain.reshape(1, n))
        out_specs += [pl.BlockSpec((tm, tn), tile), pl.BlockSpec((tm, LANES), rows)]
        out_shape += [jax.ShapeDtypeStruct((m, n), BF16), jax.ShapeDtypeStruct((m, LANES), F32)]
    outs = pl.pallas_call(
        body, grid=grid, in_specs=in_specs, out_specs=out_specs, out_shape=out_shape,
        scratch_shapes=scratch, compiler_params=_params(semantics),
        name=("matmul_fullk_" if fullk else "matmul_kloop_") + mode,
    )(*args)
    return outs if emit_norm else outs[0]


def _lru_kernel(zg_ref, zx_ref, cw_ref, cb_ref, wr_ref, br_ref, wi_ref, bi_ref, lam_ref, ng_ref,
                o_ref, xprev_ref, hprev_ref, *, rows, heads):
    s = pl.program_id(1)
    width = heads * HEAD_DIM

    @pl.when(s == 0)
    def _():
        xprev_ref[...] = jnp.zeros_like(xprev_ref)
        hprev_ref[...] = jnp.zeros_like(hprev_ref)

    xz = zx_ref[...].astype(F32)
    prev8 = xprev_ref[...]
    row8 = lax.broadcasted_iota(jnp.int32, (8, width), 0)
    cw = cw_ref[...]
    xa = cw[CONV_WIDTH - 1:CONV_WIDTH] * xz + cb_ref[...]
    for d in range(1, CONV_WIDTH):
        rolled = pltpu.roll(xz, d, 0)
        head = jnp.where(row8 < d, pltpu.roll(prev8, d, 0), rolled[0:8])
        shifted = jnp.concatenate([head, rolled[8:]], axis=0)
        xa = xa + cw[CONV_WIDTH - 1 - d:CONV_WIDTH - d] * shifted
    xprev_ref[...] = xz[rows - 8:rows]

    xab = xa.astype(BF16)
    r_parts, i_parts = [], []
    for h in range(heads):
        xh = xab[:, h * HEAD_DIM:(h + 1) * HEAD_DIM]
        r_parts.append(jnp.dot(xh, wr_ref[h], preferred_element_type=F32))
        i_parts.append(jnp.dot(xh, wi_ref[h], preferred_element_type=F32))
    r = _sigmoid(jnp.concatenate(r_parts, axis=1) + br_ref[...])
    gate_i = _sigmoid(jnp.concatenate(i_parts, axis=1) + bi_ref[...])

    lam = lam_ref[...]
    softplus_neg_lam = jnp.maximum(-lam, 0.0) + jnp.log1p(jnp.exp(-jnp.abs(lam)))
    log_a = (-LRU_C) * r * softplus_neg_lam
    a = jnp.exp(log_a)
    mult = jnp.sqrt(1.0 - jnp.exp(2.0 * log_a))
    row = lax.broadcasted_iota(jnp.int32, (rows, width), 0)
    mult = jnp.where(jnp.logical_and(row == 0, s == 0), 1.0, mult)
    u = mult * (gate_i * xa)

    d = 1
    while d < rows:
        a_sh = pltpu.roll(a, d, 0)
        u_sh = pltpu.roll(u, d, 0)
        valid = row >= d
        u = jnp.where(valid, u + a * u_sh, u)
        a = jnp.where(valid, a * a_sh, a)
        d *= 2
    h = u + a * hprev_ref[0:1, :]
    hprev_ref[...] = jnp.broadcast_to(h[rows - 1:rows], hprev_ref.shape)

    zg = zg_ref[...].astype(F32)
    gelu = 0.5 * zg * (1.0 + jnp.tanh(math.sqrt(2.0 / math.pi) * (zg + 0.044715 * (zg * zg * zg))))
    y = h * gelu
    ms = jnp.mean(y * y, axis=-1, keepdims=True)
    o_ref[...] = (y * lax.rsqrt(ms + NORM_EPS) * ng_ref[...]).astype(o_ref.dtype)


def _lru(z, conv_w, conv_b, w_r, b_r, w_i, b_i, lam, norm_gain, *, batch, seq, rows=256):
    heads = w_r.shape[0]
    width = heads * HEAD_DIM
    nblk = seq // rows
    row_spec = lambda c: pl.BlockSpec((rows, width), lambda b, s, c=c: (b * nblk + s, c))
    vec = pl.BlockSpec((1, width), lambda b, s: (0, 0))
    wspec = pl.BlockSpec((heads, HEAD_DIM, HEAD_DIM), lambda b, s: (0, 0, 0))
    return pl.pallas_call(
        functools.partial(_lru_kernel, rows=rows, heads=heads),
        grid=(batch, nblk),
        in_specs=[row_spec(0), row_spec(1),
                  pl.BlockSpec((CONV_WIDTH, width), lambda b, s: (0, 0)), vec,
                  wspec, vec, wspec, vec, vec, vec],
        out_specs=pl.BlockSpec((rows, width), lambda b, s: (b * nblk + s, 0)),
        out_shape=jax.ShapeDtypeStruct((batch * seq, width), BF16),
        scratch_shapes=[pltpu.VMEM((8, width), F32), pltpu.VMEM((8, width), F32)],
        compiler_params=_params(("parallel", "arbitrary")),
        name="rg_lru",
    )(z, z, conv_w, conv_b.reshape(1, width), w_r.astype(BF16), b_r.reshape(1, width),
      w_i.astype(BF16), b_i.reshape(1, width), lam.reshape(1, width), norm_gain.reshape(1, width))


def _split3(x):
    hi = x.astype(BF16)
    r1 = x - hi.astype(F32)
    mid = r1.astype(BF16)
    lo = (r1 - mid.astype(F32)).astype(BF16)
    return hi, mid, lo


def _tile_roll(x, shift):
    rows, d = x.shape
    x3 = x.reshape(rows // HGRN_SUB, HGRN_SUB, d)
    return pltpu.roll(x3, shift, 1).reshape(rows, d)


def _hgrn_kernel(zq_ref, zf_ref, zv_ref, zg_ref, lbp_ref, ng_ref, o_ref, st_s, *, rows, layer):
    s = pl.program_id(2)
    C, c = HGRN_CHUNK, HGRN_SUB
    D = HEAD_DIM

    @pl.when(s == 0)
    def _():
        st_s[...] = jnp.zeros_like(st_s)

    lbp = lbp_ref[...].astype(F32)
    e = jnp.exp(lbp - jnp.max(lbp, axis=0, keepdims=True))
    sm = e / jnp.sum(e, axis=0, keepdims=True)
    lb = jnp.zeros((1, D), F32)
    for r_ in range(1, layer + 1):
        lb = lb + sm[r_:r_ + 1]

    row = lax.broadcasted_iota(jnp.int32, (C, C), 0)
    col = lax.broadcasted_iota(jnp.int32, (C, C), 1)
    tri = (row >= col).astype(BF16)
    differ = jnp.bitwise_xor(row, col)
    levels = []
    m = C // 2
    while m >= c:
        levels.append((m, (row > col) & (differ >= m) & (differ < 2 * m)))
        m //= 2
    diag_masks = [(differ < c) & (row - col == dlt) for dlt in range(c)]

    for ch in range(rows // C):
        sl = slice(ch * C, (ch + 1) * C)
        zq = zq_ref[sl, :].astype(F32)
        zf = zf_ref[sl, :].astype(F32)
        q = zq * _sigmoid(zq) * (D ** -0.5)
        ez = jnp.exp(-jnp.abs(zf))
        inv = 1.0 / (1.0 + ez)
        sig_pos = jnp.where(zf >= 0.0, inv, ez * inv)
        sig_neg = jnp.where(zf >= 0.0, ez * inv, inv)
        f = lb + (1.0 - lb) * sig_pos
        k = (1.0 - lb) * sig_neg
        y = jnp.log1p(-lb) + (jnp.minimum(zf, 0.0) - jnp.log(1.0 + ez))
        if layer == 0:
            log_f = y
        else:
            la = jnp.log(lb)
            log_f = jnp.maximum(la, y) + jnp.log(1.0 + jnp.exp(-jnp.abs(la - y)))

        hi, mid, lo = _split3(log_f)
        g3 = jnp.dot(tri, jnp.concatenate([hi, mid, lo], axis=1), preferred_element_type=F32)
        G = g3[:, 0:D] + g3[:, D:2 * D] + g3[:, 2 * D:3 * D]
        vb = zv_ref[sl, :].astype(BF16)
        g_last = G[C - 1:C]

        a = jnp.zeros((C, C), F32)
        dec = None
        for dlt in range(c):
            if dlt == 0:
                w = q * k
            else:
                f_sh = f if dlt == 1 else _tile_roll(f, dlt - 1)
                dec = f_sh if dlt == 1 else dec * f_sh
                w = q * _tile_roll(k, dlt) * dec
            a = jnp.where(diag_masks[dlt], jnp.sum(w, axis=-1, keepdims=True), a)

        for m, mask in levels:
            g_bnd = jnp.concatenate(
                [jnp.broadcast_to(G[p * 2 * m + m - 1:p * 2 * m + m], (2 * m, D))
                 for p in range(C // (2 * m))], axis=0)
            e_ref = jnp.exp(-jnp.abs(G - g_bnd))
            sc = lax.dot_general((q * e_ref).astype(BF16), (k * e_ref).astype(BF16),
                                 (((1,), (1,)), ((), ())), preferred_element_type=F32)
            a = jnp.where(mask, sc, a)

        st = st_s[...]
        qg = (q * jnp.exp(G)).astype(BF16)
        o = (jnp.dot(a.astype(BF16), vb, preferred_element_type=F32)
             + lax.dot_general(qg, st.astype(BF16), (((1,), (1,)), ((), ())),
                               preferred_element_type=F32))
        kd = (k * jnp.exp(g_last - G)).astype(BF16)
        st_s[...] = st * jnp.exp(g_last) + lax.dot_general(
            vb, kd, (((0,), (0,)), ((), ())), preferred_element_type=F32)

        ms = jnp.mean(o * o, axis=-1, keepdims=True)
        zg = zg_ref[sl, :].astype(F32)
        o_ref[sl, :] = (o * lax.rsqrt(ms + NORM_EPS) * ng_ref[...]
                        * (zg * _sigmoid(zg))).astype(o_ref.dtype)


def _hgrn(z, lower_bounds, norm_gain, *, layer, batch, seq, heads, col0, rows=512):
    nblk = seq // rows
    D = HEAD_DIM
    depth = lower_bounds.shape[0]

    def zspec(group):
        off = col0 // D + group * heads
        return pl.BlockSpec((rows, D), lambda b, h, s, off=off: (b * nblk + s, off + h))

    return pl.pallas_call(
        functools.partial(_hgrn_kernel, rows=rows, layer=layer),
        grid=(batch, heads, nblk),
        in_specs=[zspec(0), zspec(1), zspec(2), zspec(3),
                  pl.BlockSpec((depth, D), lambda b, h, s: (0, h)),
                  pl.BlockSpec((1, D), lambda b, h, s: (0, 0))],
        out_specs=pl.BlockSpec((rows, D), lambda b, h, s: (b * nblk + s, h)),
        out_shape=jax.ShapeDtypeStruct((batch * seq, heads * D), BF16),
        scratch_shapes=[pltpu.VMEM((D, D), F32)],
        compiler_params=_params(("parallel", "parallel", "arbitrary")),
        name="hgrn2",
    )(z, z, z, z, lower_bounds, norm_gain.reshape(1, D))


def _norm_rope(x, gain, cos, sin_signed, scale):
    ms = jnp.mean(x * x, axis=-1, keepdims=True)
    xn = x * lax.rsqrt(ms + NORM_EPS) * gain
    return (xn * cos + pltpu.roll(xn, HEAD_DIM // 2, 1) * sin_signed) * scale


def _flash_kernel(zq_ref, zk_ref, v_ref, cos_ref, sin_ref, qg_ref, kg_ref, lp_ref, g_ref, o_ref,
                  q_s, k_s, m_s, l_s, acc_s, a_s, p_s, *, blk, seq, lam_init):
    i = pl.program_id(2)
    D = HEAD_DIM
    lanes = blk // D

    @pl.when(i == 0)
    def _():
        def prep_keys(r, carry):
            rows = pl.ds(pl.multiple_of(r * blk, blk), blk)
            for c in range(2):
                k_s[rows, c * D:(c + 1) * D] = _norm_rope(
                    zk_ref[rows, c * D:(c + 1) * D].astype(F32), kg_ref[...],
                    cos_ref[rows, :], sin_ref[rows, :], 1.0).astype(k_s.dtype)
            return carry

        lax.fori_loop(0, seq // blk, prep_keys, 0)

    q_rows = pl.ds(pl.multiple_of(i * blk, blk), blk)
    for c in range(2):
        q_s[:, c * D:(c + 1) * D] = _norm_rope(
            zq_ref[:, c * D:(c + 1) * D].astype(F32), qg_ref[...],
            cos_ref[q_rows, :], sin_ref[q_rows, :], D ** -0.5 * LOG2_E).astype(q_s.dtype)
    q_ref, k_ref = q_s, k_s

    m_s[...] = jnp.full_like(m_s, -jnp.inf)
    l_s[...] = jnp.zeros_like(l_s)
    acc_s[...] = jnp.zeros_like(acc_s)

    def scores(t, masked):
        k = k_ref[pl.ds(pl.multiple_of(t * blk, blk), blk), :]
        for c in range(2):
            q = q_ref[:, c * D:(c + 1) * D]
            s = lax.dot_general(q, k[:, c * D:(c + 1) * D], (((1,), (1,)), ((), ())),
                                preferred_element_type=F32)
            if masked:
                row = lax.broadcasted_iota(jnp.int32, (blk, blk), 0)
                col = lax.broadcasted_iota(jnp.int32, (blk, blk), 1)
                s = jnp.where(col <= row, s, -jnp.inf)
            m_prev = m_s[c]
            m_new = jnp.maximum(m_prev, jnp.max(s, axis=-1, keepdims=True))
            alpha = jnp.exp2(m_prev - m_new)
            p = jnp.exp2(s - jnp.tile(m_new, (1, lanes)))
            l_s[c] = alpha * l_s[c] + jnp.sum(p, axis=-1, keepdims=True)
            m_s[c] = m_new
            a_s[c] = alpha
            p_s[c] = p.astype(BF16)

    def weighted_values(t):
        v = v_ref[pl.ds(pl.multiple_of(t * blk, blk), blk), :]
        for c in range(2):
            acc_s[c] = jnp.tile(a_s[c], (1, 2)) * acc_s[c] + jnp.dot(
                p_s[c], v, preferred_element_type=F32)

    @pl.when(i == 0)
    def _():
        scores(0, True)

    @pl.when(i > 0)
    def _():
        scores(0, False)

        def pipelined(t, carry):
            weighted_values(t - 1)
            scores(t, False)
            return carry

        lax.fori_loop(1, i, pipelined, 0)
        weighted_values(i - 1)
        scores(i, True)

    weighted_values(i)

    lp = lp_ref[...].astype(F32)
    lam = (jnp.exp(jnp.sum(lp[0:1] * lp[1:2], axis=-1, keepdims=True))
           - jnp.exp(jnp.sum(lp[2:3] * lp[3:4], axis=-1, keepdims=True)) + lam_init)
    o = (acc_s[0] / jnp.tile(l_s[0], (1, 2))) - lam * (acc_s[1] / jnp.tile(l_s[1], (1, 2)))
    ms = jnp.mean(o * o, axis=-1, keepdims=True)
    o_ref[...] = (o * lax.rsqrt(ms + NORM_EPS) * g_ref[...] * (1.0 - lam_init)).astype(o_ref.dtype)


def _flash(z, cos, sin_signed, q_gain, k_gain, lam_params, subln_gain, *, layer, batch, seq, heads,
           col0, blk=512):
    nblk = seq // blk
    D = HEAD_DIM
    lam_init = 0.8 - 0.6 * math.exp(-0.3 * layer)
    cb = col0 // (2 * D)
    qspec = pl.BlockSpec((blk, 2 * D), lambda b, h, i: (b * nblk + i, cb + h))
    kspec = pl.BlockSpec((seq, 2 * D), lambda b, h, i: (b, cb + heads + h))
    vspec = pl.BlockSpec((seq, 2 * D), lambda b, h, i: (b, cb + 2 * heads + h))
    table = pl.BlockSpec((seq, D), lambda b, h, i: (0, 0))
    vec = pl.BlockSpec((1, D), lambda b, h, i: (0, 0))
    return pl.pallas_call(
        functools.partial(_flash_kernel, blk=blk, seq=seq, lam_init=lam_init),
        grid=(batch, heads, nblk),
        in_specs=[qspec, kspec, vspec, table, table, vec, vec,
                  pl.BlockSpec((4, D), lambda b, h, i: (0, 0)),
                  pl.BlockSpec((1, 2 * D), lambda b, h, i: (0, 0))],
        out_specs=pl.BlockSpec((blk, 2 * D), lambda b, h, i: (b * nblk + i, h)),
        out_shape=jax.ShapeDtypeStruct((batch * seq, heads * 2 * D), BF16),
        scratch_shapes=[pltpu.VMEM((blk, 2 * D), BF16), pltpu.VMEM((seq, 2 * D), BF16),
                        pltpu.VMEM((2, blk, D), F32), pltpu.VMEM((2, blk, D), F32),
                        pltpu.VMEM((2, blk, 2 * D), F32), pltpu.VMEM((2, blk, D), F32),
                        pltpu.VMEM((2, blk, blk), BF16)],
        compiler_params=_params(("parallel", "parallel", "arbitrary")),
        name="diff_flash",
    )(z, z, z, cos, sin_signed, q_gain.reshape(1, D), k_gain.reshape(1, D), lam_params,
      subln_gain.reshape(1, 2 * D))


def kernel(x, ln1_gain, w_in, conv_w, conv_b, lru_w_r, lru_b_r, lru_w_i, lru_b_i, lru_lambda, lru_norm_gain, hgrn_lower_bounds, hgrn_norm_gain, q_norm_gain, k_norm_gain, diff_lambda, diff_subln_gain, w_out, ln2_gain, w_ff1, w_ff2):
    batch, seq, d_model = x.shape
    depth = w_in.shape[0]
    D = HEAD_DIM
    lru_width = conv_w.shape[-1]
    key_width = hgrn_lower_bounds.shape[-1]
    in_width = w_in.shape[-1]
    mix_width = w_out.shape[1]
    diff_width = in_width - 2 * key_width - 2 * mix_width
    val_width = mix_width - lru_width - diff_width
    assert q_norm_gain.shape[-1] == D and hgrn_norm_gain.shape[-1] == D
    assert lru_w_r.shape[-1] == D and key_width == val_width
    hgrn_heads = key_width // D
    diff_heads = diff_width // (2 * D)
    off_b = 2 * lru_width
    off_c = off_b + 2 * key_width + 2 * val_width

    inv = 1.0 / (ROPE_THETA ** (jnp.arange(0, D, 2, dtype=F32) / D))
    ang = jnp.arange(seq, dtype=F32)[:, None] * inv[None, :]
    ang = jnp.concatenate([ang, ang], axis=-1)
    cos = jnp.cos(ang)
    sign = jnp.concatenate([-jnp.ones((D // 2,), F32), jnp.ones((D // 2,), F32)])
    sin_signed = jnp.sin(ang) * sign[None, :]

    xf = x.reshape(batch * seq, d_model)
    w_in_b, w_out_b = w_in.astype(BF16), w_out.astype(BF16)
    w_ff1_b, w_ff2_b = w_ff1.astype(BF16), w_ff2.astype(BF16)
    xg, ss = _rmsnorm(xf, ln1_gain[0]), None
    for l in range(depth):
        z = _matmul([xg], w_in_b, l, row_ss=ss, out_dtype=BF16)

        ya = _lru(z, conv_w[l], conv_b[l], lru_w_r[l], lru_b_r[l], lru_w_i[l], lru_b_i[l],
                  lru_lambda[l], lru_norm_gain[l], batch=batch, seq=seq)
        yb = _hgrn(z, hgrn_lower_bounds, hgrn_norm_gain[l], layer=l, batch=batch, seq=seq,
                   heads=hgrn_heads, col0=off_b)
        yc = _flash(z, cos, sin_signed, q_norm_gain[l], k_norm_gain[l], diff_lambda[l],
                    diff_subln_gain[l], layer=l, batch=batch, seq=seq, heads=diff_heads, col0=off_c)

        xf, xg, ss = _matmul([ya, yb, yc], w_out_b, l, mode="residual", residual=xf,
                             next_gain=ln2_gain[l], tn=512)
        u = _matmul([xg], w_ff1_b, l, mode="relu2", row_ss=ss, out_dtype=BF16)
        if l + 1 < depth:
            xf, xg, ss = _matmul([u], w_ff2_b, l, mode="residual", residual=xf,
                                 next_gain=ln1_gain[l + 1])
        else:
            xf = _matmul([u], w_ff2_b, l, mode="residual", residual=xf)
    return xf.reshape(batch, seq, d_model)
```

```python
import functools
import math

import jax
import jax.numpy as jnp
from jax import lax
from jax.experimental import pallas as pl
from jax.experimental.pallas import tpu as pltpu

F32 = jnp.float32
BF16 = jnp.bfloat16

NORM_EPS = 1e-6
LRU_C = 8.0
ROPE_THETA = 10000.0
CONV_WIDTH = 4
LOG2_E = 1.4426950408889634
HEAD_DIM = 128
HGRN_CHUNK = 128
HGRN_SUB = 8
VMEM_LIMIT = 56 * 1024 * 1024


def _params(semantics):
    return pltpu.CompilerParams(dimension_semantics=semantics, vmem_limit_bytes=VMEM_LIMIT)


def _sigmoid(x):
    return 1.0 / (1.0 + jnp.exp(-x))


def _log_sigmoid(x):
    return jnp.minimum(x, 0.0) - jnp.log1p(jnp.exp(-jnp.abs(x)))


def _rmsnorm_kernel(x_ref, g_ref, o_ref):
    x = x_ref[...]
    ms = jnp.mean(x * x, axis=-1, keepdims=True)
    o_ref[...] = (x * lax.rsqrt(ms + NORM_EPS) * g_ref[...]).astype(o_ref.dtype)


def _rmsnorm(x, gain, tm=256):
    n, d = x.shape
    return pl.pallas_call(
        _rmsnorm_kernel,
        grid=(n // tm,),
        in_specs=[pl.BlockSpec((tm, d), lambda i: (i, 0)),
                  pl.BlockSpec((1, d), lambda i: (0, 0))],
        out_specs=pl.BlockSpec((tm, d), lambda i: (i, 0)),
        out_shape=jax.ShapeDtypeStruct((n, d), BF16),
        compiler_params=_params(("parallel",)),
        name="rmsnorm",
    )(x, gain.reshape(1, d))


LANES = 128


def _row_factor(ss_ref, width, tn):
    rstd = lax.rsqrt(ss_ref[...] * (1.0 / width) + NORM_EPS)
    return jnp.tile(rstd, (1, tn // LANES))


def _finish(acc, j, *, mode, r_ref, g_ref, o_ref, xg_ref, ss_ref):
    if mode == "relu2":
        acc = jnp.square(jnp.maximum(acc, 0.0))
    elif mode == "residual":
        acc = acc + r_ref[...]
    o_ref[...] = acc.astype(o_ref.dtype)
    if xg_ref is not None:
        xg_ref[...] = (acc * g_ref[...]).astype(xg_ref.dtype)

        @pl.when(j == 0)
        def _():
            ss_ref[...] = jnp.zeros_like(ss_ref)

        ss_ref[...] += jnp.sum(acc * acc, axis=-1, keepdims=True)


def _unpack(refs, n_a, row_scale, mode, emit_norm, n_cast):
    it = iter(refs)
    a_refs = [next(it) for _ in range(n_a)]
    w_ref = next(it)
    ss_in = next(it) if row_scale else None
    r_ref = next(it) if mode == "residual" else None
    g_ref = next(it) if emit_norm else None
    cast_src = [next(it) for _ in range(n_cast)]
    o_ref = next(it)
    xg_ref = next(it) if emit_norm else None
    ss_ref = next(it) if emit_norm else None
    for src_ref in cast_src:
        dst_ref = next(it)
        dst_ref[...] = src_ref[...].astype(dst_ref.dtype)
    return a_refs, w_ref, ss_in, r_ref, g_ref, o_ref, xg_ref, ss_ref, list(it)


def _mm_fullk_kernel(*refs, splits, mode, row_scale, emit_norm, n_cast):
    a_refs, w_ref, ss_in, r_ref, g_ref, o_ref, xg_ref, ss_ref, _ = _unpack(
        refs, len(splits), row_scale, mode, emit_norm, n_cast)
    acc, off = None, 0
    for a_ref, kp in zip(a_refs, splits):
        part = jnp.dot(a_ref[...], w_ref[off:off + kp, :], preferred_element_type=F32)
        acc = part if acc is None else acc + part
        off += kp
    if row_scale:
        acc = acc * _row_factor(ss_in, off, o_ref.shape[-1])
    _finish(acc, pl.program_id(1), mode=mode, r_ref=r_ref, g_ref=g_ref, o_ref=o_ref,
            xg_ref=xg_ref, ss_ref=ss_ref)


def _mm_kloop_kernel(*refs, nk, mode, emit_norm):
    a_refs, w_ref, _, r_ref, g_ref, o_ref, xg_ref, ss_ref, (acc_ref,) = _unpack(
        refs, 1, False, mode, emit_norm, 0)
    k = pl.program_id(2)

    @pl.when(k == 0)
    def _():
        acc_ref[...] = jnp.zeros_like(acc_ref)

    acc_ref[...] += jnp.dot(a_refs[0][...], w_ref[...], preferred_element_type=F32)

    @pl.when(k == nk - 1)
    def _():
        _finish(acc_ref[...], pl.program_id(1), mode=mode, r_ref=r_ref, g_ref=g_ref, o_ref=o_ref,
                xg_ref=xg_ref, ss_ref=ss_ref)


def _matmul(a_parts, w, *, mode="plain", residual=None, row_ss=None, next_gain=None, casts=(),
            out_dtype=F32, tm=1024, tn=1024, tk=2048):
    m = a_parts[0].shape[0]
    splits = tuple(p.shape[1] for p in a_parts)
    kdim = sum(splits)
    n = w.shape[-1]
    fullk = kdim <= 4096
    row_scale = row_ss is not None
    emit_norm = next_gain is not None
    tm, tn = min(tm, m), min(tn, n)

    if fullk:
        grid = (m // tm, n // tn)
        tile = lambda i, j: (i, j)
        rows = lambda i, j: (i, 0)
        cols = lambda i, j: (0, j)
        in_specs = [pl.BlockSpec((tm, kp), rows) for kp in splits]
        in_specs.append(pl.BlockSpec((kdim, tn), cols))
        body = functools.partial(_mm_fullk_kernel, splits=splits, mode=mode, row_scale=row_scale,
                                 emit_norm=emit_norm, n_cast=len(casts))
        scratch = []
        semantics = ("parallel", "arbitrary")
    else:
        assert len(a_parts) == 1 and not row_scale and not casts
        tk = min(tk, kdim)
        nk = kdim // tk
        grid = (m // tm, n // tn, nk)
        tile = lambda i, j, k: (i, j)
        rows = lambda i, j, k: (i, 0)
        cols = lambda i, j, k: (0, j)
        in_specs = [pl.BlockSpec((tm, tk), lambda i, j, k: (i, k)),
                    pl.BlockSpec((tk, tn), lambda i, j, k: (k, j))]
        body = functools.partial(_mm_kloop_kernel, nk=nk, mode=mode, emit_norm=emit_norm)
        scratch = [pltpu.VMEM((tm, tn), F32)]
        semantics = ("parallel", "arbitrary", "arbitrary")

    args = list(a_parts) + [w]
    if row_scale:
        in_specs.append(pl.BlockSpec((tm, LANES), rows))
        args.append(row_ss)
    if mode == "residual":
        in_specs.append(pl.BlockSpec((tm, tn), tile))
        args.append(residual)
    out_specs = [pl.BlockSpec((tm, tn), tile)]
    out_shape = [jax.ShapeDtypeStruct((m, n), out_dtype)]
    if emit_norm:
        in_specs.append(pl.BlockSpec((1, tn), cols))
        args.append(next_gain.reshape(1, n))
        out_specs += [pl.BlockSpec((tm, tn), tile), pl.BlockSpec((tm, LANES), rows)]
        out_shape += [jax.ShapeDtypeStruct((m, n), BF16), jax.ShapeDtypeStruct((m, LANES), F32)]
    for w_f32, layer in casts:
        _, wr, wc = w_f32.shape
        col_blocks = max(c for c in range(1, grid[1] + 1)
                         if grid[1] % c == 0 and wc % (c * LANES) == 0)
        br, bc = wr // grid[0], wc // col_blocks
        assert wr % grid[0] == 0 and br % 8 == 0
        col = lambda j, last=col_blocks - 1: jnp.minimum(j, last)
        in_specs.append(pl.BlockSpec((None, br, bc),
                                     lambda i, j, layer=layer, col=col: (layer, i, col(j))))
        args.append(w_f32)
        out_specs.append(pl.BlockSpec((br, bc), lambda i, j, col=col: (i, col(j))))
        out_shape.append(jax.ShapeDtypeStruct((wr, wc), BF16))
    outs = pl.pallas_call(
        body, grid=grid, in_specs=in_specs, out_specs=out_specs, out_shape=out_shape,
        scratch_shapes=scratch, compiler_params=_params(semantics),
        name=("matmul_fullk_" if fullk else "matmul_kloop_") + mode,
    )(*args)
    return outs if len(outs) > 1 else outs[0]


def _lru_kernel(zg_ref, zx_ref, cw_ref, cb_ref, wr_ref, br_ref, wi_ref, bi_ref, lam_ref, ng_ref,
                o_ref, xprev_ref, hprev_ref, *, rows, heads):
    s = pl.program_id(1)
    width = heads * HEAD_DIM

    @pl.when(s == 0)
    def _():
        xprev_ref[...] = jnp.zeros_like(xprev_ref)
        hprev_ref[...] = jnp.zeros_like(hprev_ref)

    xz = zx_ref[...].astype(F32)
    prev8 = xprev_ref[...]
    row8 = lax.broadcasted_iota(jnp.int32, (8, width), 0)
    cw = cw_ref[...]
    xa = cw[CONV_WIDTH - 1:CONV_WIDTH] * xz + cb_ref[...]
    for d in range(1, CONV_WIDTH):
        rolled = pltpu.roll(xz, d, 0)
        head = jnp.where(row8 < d, pltpu.roll(prev8, d, 0), rolled[0:8])
        shifted = jnp.concatenate([head, rolled[8:]], axis=0)
        xa = xa + cw[CONV_WIDTH - 1 - d:CONV_WIDTH - d] * shifted
    xprev_ref[...] = xz[rows - 8:rows]

    xab = xa.astype(BF16)
    r_parts, i_parts = [], []
    for h in range(heads):
        xh = xab[:, h * HEAD_DIM:(h + 1) * HEAD_DIM]
        r_parts.append(jnp.dot(xh, wr_ref[h], preferred_element_type=F32))
        i_parts.append(jnp.dot(xh, wi_ref[h], preferred_element_type=F32))
    r = _sigmoid(jnp.concatenate(r_parts, axis=1) + br_ref[...])
    gate_i = _sigmoid(jnp.concatenate(i_parts, axis=1) + bi_ref[...])

    lam = lam_ref[...]
    softplus_neg_lam = jnp.maximum(-lam, 0.0) + jnp.log1p(jnp.exp(-jnp.abs(lam)))
    log_a = (-LRU_C) * r * softplus_neg_lam
    a = jnp.exp(log_a)
    mult = jnp.sqrt(1.0 - jnp.exp(2.0 * log_a))
    row = lax.broadcasted_iota(jnp.int32, (rows, width), 0)
    mult = jnp.where(jnp.logical_and(row == 0, s == 0), 1.0, mult)
    u = mult * (gate_i * xa)

    d = 1
    while d < rows:
        a_sh = pltpu.roll(a, d, 0)
        u_sh = pltpu.roll(u, d, 0)
        valid = row >= d
        u = jnp.where(valid, u + a * u_sh, u)
        a = jnp.where(valid, a * a_sh, a)
        d *= 2
    h = u + a * hprev_ref[0:1, :]
    hprev_ref[...] = jnp.broadcast_to(h[rows - 1:rows], hprev_ref.shape)

    zg = zg_ref[...].astype(F32)
    gelu = 0.5 * zg * (1.0 + jnp.tanh(math.sqrt(2.0 / math.pi) * (zg + 0.044715 * (zg * zg * zg))))
    y = h * gelu
    ms = jnp.mean(y * y, axis=-1, keepdims=True)
    o_ref[...] = (y * lax.rsqrt(ms + NORM_EPS) * ng_ref[...]).astype(o_ref.dtype)


def _lru(z, conv_w, conv_b, w_r, b_r, w_i, b_i, lam, norm_gain, *, batch, seq, rows=256):
    heads = w_r.shape[0]
    width = heads * HEAD_DIM
    nblk = seq // rows
    row_spec = lambda c: pl.BlockSpec((rows, width), lambda b, s, c=c: (b * nblk + s, c))
    vec = pl.BlockSpec((1, width), lambda b, s: (0, 0))
    wspec = pl.BlockSpec((heads, HEAD_DIM, HEAD_DIM), lambda b, s: (0, 0, 0))
    return pl.pallas_call(
        functools.partial(_lru_kernel, rows=rows, heads=heads),
        grid=(batch, nblk),
        in_specs=[row_spec(0), row_spec(1),
                  pl.BlockSpec((CONV_WIDTH, width), lambda b, s: (0, 0)), vec,
                  wspec, vec, wspec, vec, vec, vec],
        out_specs=pl.BlockSpec((rows, width), lambda b, s: (b * nblk + s, 0)),
        out_shape=jax.ShapeDtypeStruct((batch * seq, width), BF16),
        scratch_shapes=[pltpu.VMEM((8, width), F32), pltpu.VMEM((8, width), F32)],
        compiler_params=_params(("parallel", "arbitrary")),
        name="rg_lru",
    )(z, z, conv_w, conv_b.reshape(1, width), w_r.astype(BF16), b_r.reshape(1, width),
      w_i.astype(BF16), b_i.reshape(1, width), lam.reshape(1, width), norm_gain.reshape(1, width))


def _split3(x):
    hi = x.astype(BF16)
    r1 = x - hi.astype(F32)
    mid = r1.astype(BF16)
    lo = (r1 - mid.astype(F32)).astype(BF16)
    return hi, mid, lo


def _tile_roll(x, shift):
    rows, d = x.shape
    x3 = x.reshape(rows // HGRN_SUB, HGRN_SUB, d)
    return pltpu.roll(x3, shift, 1).reshape(rows, d)


def _hgrn_kernel(zq_ref, zf_ref, zv_ref, zg_ref, lbp_ref, ng_ref, o_ref, st_s, *, rows, layer):
    s = pl.program_id(2)
    C, c = HGRN_CHUNK, HGRN_SUB
    D = HEAD_DIM

    @pl.when(s == 0)
    def _():
        st_s[...] = jnp.zeros_like(st_s)

    lbp = lbp_ref[...].astype(F32)
    e = jnp.exp(lbp - jnp.max(lbp, axis=0, keepdims=True))
    sm = e / jnp.sum(e, axis=0, keepdims=True)
    lb = jnp.zeros((1, D), F32)
    for r_ in range(1, layer + 1):
        lb = lb + sm[r_:r_ + 1]

    row = lax.broadcasted_iota(jnp.int32, (C, C), 0)
    col = lax.broadcasted_iota(jnp.int32, (C, C), 1)
    tri = (row >= col).astype(BF16)
    differ = jnp.bitwise_xor(row, col)
    levels = []
    m = C // 2
    while m >= c:
        levels.append((m, (row > col) & (differ >= m) & (differ < 2 * m)))
        m //= 2
    diag_masks = [(differ < c) & (row - col == dlt) for dlt in range(c)]

    for ch in range(rows // C):
        sl = slice(ch * C, (ch + 1) * C)
        zq = zq_ref[sl, :].astype(F32)
        zf = zf_ref[sl, :].astype(F32)
        q = zq * _sigmoid(zq) * (D ** -0.5)
        ez = jnp.exp(-jnp.abs(zf))
        inv = 1.0 / (1.0 + ez)
        sig_pos = jnp.where(zf >= 0.0, inv, ez * inv)
        sig_neg = jnp.where(zf >= 0.0, ez * inv, inv)
        f = lb + (1.0 - lb) * sig_pos
        k = (1.0 - lb) * sig_neg
        y = jnp.log1p(-lb) + (jnp.minimum(zf, 0.0) - jnp.log(1.0 + ez))
        if layer == 0:
            log_f = y
        else:
            la = jnp.log(lb)
            log_f = jnp.maximum(la, y) + jnp.log(1.0 + jnp.exp(-jnp.abs(la - y)))

        hi, mid, lo = _split3(log_f)
        g3 = jnp.dot(tri, jnp.concatenate([hi, mid, lo], axis=1), preferred_element_type=F32)
        G = g3[:, 0:D] + g3[:, D:2 * D] + g3[:, 2 * D:3 * D]
        vb = zv_ref[sl, :].astype(BF16)
        g_last = G[C - 1:C]

        a = jnp.zeros((C, C), F32)
        dec = None
        for dlt in range(c):
            if dlt == 0:
                w = q * k
            else:
                f_sh = f if dlt == 1 else _tile_roll(f, dlt - 1)
                dec = f_sh if dlt == 1 else dec * f_sh
                w = q * _tile_roll(k, dlt) * dec
            a = jnp.where(diag_masks[dlt], jnp.sum(w, axis=-1, keepdims=True), a)

        for m, mask in levels:
            g_bnd = jnp.concatenate(
                [jnp.broadcast_to(G[p * 2 * m + m - 1:p * 2 * m + m], (2 * m, D))
                 for p in range(C // (2 * m))], axis=0)
            e_ref = jnp.exp(-jnp.abs(G - g_bnd))
            sc = lax.dot_general((q * e_ref).astype(BF16), (k * e_ref).astype(BF16),
                                 (((1,), (1,)), ((), ())), preferred_element_type=F32)
            a = jnp.where(mask, sc, a)

        st = st_s[...]
        qg = (q * jnp.exp(G)).astype(BF16)
        o = (jnp.dot(a.astype(BF16), vb, preferred_element_type=F32)
             + lax.dot_general(qg, st.astype(BF16), (((1,), (1,)), ((), ())),
                               preferred_element_type=F32))
        kd = (k * jnp.exp(g_last - G)).astype(BF16)
        st_s[...] = st * jnp.exp(g_last) + lax.dot_general(
            vb, kd, (((0,), (0,)), ((), ())), preferred_element_type=F32)

        ms = jnp.mean(o * o, axis=-1, keepdims=True)
        zg = zg_ref[sl, :].astype(F32)
        o_ref[sl, :] = (o * lax.rsqrt(ms + NORM_EPS) * ng_ref[...]
                        * (zg * _sigmoid(zg))).astype(o_ref.dtype)


def _hgrn(z, lower_bounds, norm_gain, *, layer, batch, seq, heads, col0, rows=512):
    nblk = seq // rows
    D = HEAD_DIM
    depth = lower_bounds.shape[0]

    def zspec(group):
        off = col0 // D + group * heads
        return pl.BlockSpec((rows, D), lambda b, h, s, off=off: (b * nblk + s, off + h))

    return pl.pallas_call(
        functools.partial(_hgrn_kernel, rows=rows, layer=layer),
        grid=(batch, heads, nblk),
        in_specs=[zspec(0), zspec(1), zspec(2), zspec(3),
                  pl.BlockSpec((depth, D), lambda b, h, s: (0, h)),
                  pl.BlockSpec((1, D), lambda b, h, s: (0, 0))],
        out_specs=pl.BlockSpec((rows, D), lambda b, h, s: (b * nblk + s, h)),
        out_shape=jax.ShapeDtypeStruct((batch * seq, heads * D), BF16),
        scratch_shapes=[pltpu.VMEM((D, D), F32)],
        compiler_params=_params(("parallel", "parallel", "arbitrary")),
        name="hgrn2",
    )(z, z, z, z, lower_bounds, norm_gain.reshape(1, D))


def _norm_rope(x, gain, cos, sin_signed, scale):
    ms = jnp.mean(x * x, axis=-1, keepdims=True)
    xn = x * lax.rsqrt(ms + NORM_EPS) * gain
    return (xn * cos + pltpu.roll(xn, HEAD_DIM // 2, 1) * sin_signed) * scale


def _flash_kernel(zq_ref, zk_ref, v_ref, cos_ref, sin_ref, qg_ref, kg_ref, lp_ref, g_ref, o_ref,
                  q_s, k_s, m_s, l_s, acc_s, a_s, p_s, *, blk, seq, lam_init):
    i = pl.program_id(2)
    D = HEAD_DIM
    lanes = blk // D

    @pl.when(i == 0)
    def _():
        def prep_keys(r, carry):
            rows = pl.ds(pl.multiple_of(r * blk, blk), blk)
            for c in range(2):
                k_s[rows, c * D:(c + 1) * D] = _norm_rope(
                    zk_ref[rows, c * D:(c + 1) * D].astype(F32), kg_ref[...],
                    cos_ref[rows, :], sin_ref[rows, :], 1.0).astype(k_s.dtype)
            return carry

        lax.fori_loop(0, seq // blk, prep_keys, 0)

    q_rows = pl.ds(pl.multiple_of(i * blk, blk), blk)
    for c in range(2):
        q_s[:, c * D:(c + 1) * D] = _norm_rope(
            zq_ref[:, c * D:(c + 1) * D].astype(F32), qg_ref[...],
            cos_ref[q_rows, :], sin_ref[q_rows, :], D ** -0.5 * LOG2_E).astype(q_s.dtype)
    q_ref, k_ref = q_s, k_s

    m_s[...] = jnp.full_like(m_s, -jnp.inf)
    l_s[...] = jnp.zeros_like(l_s)
    acc_s[...] = jnp.zeros_like(acc_s)

    def scores(t, masked):
        k = k_ref[pl.ds(pl.multiple_of(t * blk, blk), blk), :]
        for c in range(2):
            q = q_ref[:, c * D:(c + 1) * D]
            s = lax.dot_general(q, k[:, c * D:(c + 1) * D], (((1,), (1,)), ((), ())),
                                preferred_element_type=F32)
            if masked:
                row = lax.broadcasted_iota(jnp.int32, (blk, blk), 0)
                col = lax.broadcasted_iota(jnp.int32, (blk, blk), 1)
                s = jnp.where(col <= row, s, -jnp.inf)
            m_prev = m_s[c]
            m_new = jnp.maximum(m_prev, jnp.max(s, axis=-1, keepdims=True))
            alpha = jnp.exp2(m_prev - m_new)
            p = jnp.exp2(s - jnp.tile(m_new, (1, lanes)))
            l_s[c] = alpha * l_s[c] + jnp.sum(p, axis=-1, keepdims=True)
            m_s[c] = m_new
            a_s[c] = alpha
            p_s[c] = p.astype(BF16)

    def weighted_values(t):
        v = v_ref[pl.ds(pl.multiple_of(t * blk, blk), blk), :]
        for c in range(2):
            acc_s[c] = jnp.tile(a_s[c], (1, 2)) * acc_s[c] + jnp.dot(
                p_s[c], v, preferred_element_type=F32)

    @pl.when(i == 0)
    def _():
        scores(0, True)

    @pl.when(i > 0)
    def _():
        scores(0, False)

        def pipelined(t, carry):
            weighted_values(t - 1)
            scores(t, False)
            return carry

        lax.fori_loop(1, i, pipelined, 0)
        weighted_values(i - 1)
        scores(i, True)

    weighted_values(i)

    lp = lp_ref[...].astype(F32)
    lam = (jnp.exp(jnp.sum(lp[0:1] * lp[1:2], axis=-1, keepdims=True))
           - jnp.exp(jnp.sum(lp[2:3] * lp[3:4], axis=-1, keepdims=True)) + lam_init)
    o = (acc_s[0] / jnp.tile(l_s[0], (1, 2))) - lam * (acc_s[1] / jnp.tile(l_s[1], (1, 2)))
    ms = jnp.mean(o * o, axis=-1, keepdims=True)
    o_ref[...] = (o * lax.rsqrt(ms + NORM_EPS) * g_ref[...] * (1.0 - lam_init)).astype(o_ref.dtype)


def _flash(z, cos, sin_signed, q_gain, k_gain, lam_params, subln_gain, *, layer, batch, seq, heads,
           col0, blk=512):
    nblk = seq // blk
    D = HEAD_DIM
    lam_init = 0.8 - 0.6 * math.exp(-0.3 * layer)
    cb = col0 // (2 * D)
    qspec = pl.BlockSpec((blk, 2 * D), lambda b, h, i: (b * nblk + i, cb + h))
    kspec = pl.BlockSpec((seq, 2 * D), lambda b, h, i: (b, cb + heads + h))
    vspec = pl.BlockSpec((seq, 2 * D), lambda b, h, i: (b, cb + 2 * heads + h))
    table = pl.BlockSpec((seq, D), lambda b, h, i: (0, 0))
    vec = pl.BlockSpec((1, D), lambda b, h, i: (0, 0))
    return pl.pallas_call(
        functools.partial(_flash_kernel, blk=blk, seq=seq, lam_init=lam_init),
        grid=(batch, heads, nblk),
        in_specs=[qspec, kspec, vspec, table, table, vec, vec,
                  pl.BlockSpec((4, D), lambda b, h, i: (0, 0)),
                  pl.BlockSpec((1, 2 * D), lambda b, h, i: (0, 0))],
        out_specs=pl.BlockSpec((blk, 2 * D), lambda b, h, i: (b * nblk + i, h)),
        out_shape=jax.ShapeDtypeStruct((batch * seq, heads * 2 * D), BF16),
        scratch_shapes=[pltpu.VMEM((blk, 2 * D), BF16), pltpu.VMEM((seq, 2 * D), BF16),
                        pltpu.VMEM((2, blk, D), F32), pltpu.VMEM((2, blk, D), F32),
                        pltpu.VMEM((2, blk, 2 * D), F32), pltpu.VMEM((2, blk, D), F32),
                        pltpu.VMEM((2, blk, blk), BF16)],
        compiler_params=_params(("parallel", "parallel", "arbitrary")),
        name="diff_flash",
    )(z, z, z, cos, sin_signed, q_gain.reshape(1, D), k_gain.reshape(1, D), lam_params,
      subln_gain.reshape(1, 2 * D))


def kernel(x, ln1_gain, w_in, conv_w, conv_b, lru_w_r, lru_b_r, lru_w_i, lru_b_i, lru_lambda, lru_norm_gain, hgrn_lower_bounds, hgrn_norm_gain, q_norm_gain, k_norm_gain, diff_lambda, diff_subln_gain, w_out, ln2_gain, w_ff1, w_ff2):
    batch, seq, d_model = x.shape
    depth = w_in.shape[0]
    D = HEAD_DIM
    lru_width = conv_w.shape[-1]
    key_width = hgrn_lower_bounds.shape[-1]
    in_width = w_in.shape[-1]
    mix_width = w_out.shape[1]
    diff_width = in_width - 2 * key_width - 2 * mix_width
    val_width = mix_width - lru_width - diff_width
    assert q_norm_gain.shape[-1] == D and hgrn_norm_gain.shape[-1] == D
    assert lru_w_r.shape[-1] == D and key_width == val_width
    hgrn_heads = key_width // D
    diff_heads = diff_width // (2 * D)
    off_b = 2 * lru_width
    off_c = off_b + 2 * key_width + 2 * val_width

    inv = 1.0 / (ROPE_THETA ** (jnp.arange(0, D, 2, dtype=F32) / D))
    ang = jnp.arange(seq, dtype=F32)[:, None] * inv[None, :]
    ang = jnp.concatenate([ang, ang], axis=-1)
    cos = jnp.cos(ang)
    sign = jnp.concatenate([-jnp.ones((D // 2,), F32), jnp.ones((D // 2,), F32)])
    sin_signed = jnp.sin(ang) * sign[None, :]

    xf = x.reshape(batch * seq, d_model)
    xg, ss = _rmsnorm(xf, ln1_gain[0]), None
    w_in_b = w_in[0].astype(BF16)
    for l in range(depth):
        z, w_out_b = _matmul([xg], w_in_b, row_ss=ss, out_dtype=BF16, casts=[(w_out, l)])

        ya = _lru(z, conv_w[l], conv_b[l], lru_w_r[l], lru_b_r[l], lru_w_i[l], lru_b_i[l],
                  lru_lambda[l], lru_norm_gain[l], batch=batch, seq=seq)
        yb = _hgrn(z, hgrn_lower_bounds, hgrn_norm_gain[l], layer=l, batch=batch, seq=seq,
                   heads=hgrn_heads, col0=off_b)
        yc = _flash(z, cos, sin_signed, q_norm_gain[l], k_norm_gain[l], diff_lambda[l],
                    diff_subln_gain[l], layer=l, batch=batch, seq=seq, heads=diff_heads, col0=off_c)

        xf, xg, ss, w_ff1_b = _matmul([ya, yb, yc], w_out_b, mode="residual", residual=xf,
                                      next_gain=ln2_gain[l], tn=512, casts=[(w_ff1, l)])
        ff1_casts = [(w_ff2, l)] + ([(w_in, l + 1)] if l + 1 < depth else [])
        u, w_ff2_b, *rest = _matmul([xg], w_ff1_b, mode="relu2", row_ss=ss, out_dtype=BF16,
                                    casts=ff1_casts)
        if l + 1 < depth:
            w_in_b = rest[0]
            xf, xg, ss = _matmul([u], w_ff2_b, mode="residual", residual=xf,
                                 next_gain=ln1_gain[l + 1])
        else:
            xf = _matmul([u], w_ff2_b, mode="residual", residual=xf)
    return xf.reshape(batch, seq, d_model)
```

```python
import functools
import math

import jax
import jax.numpy as jnp
from jax import lax
from jax.experimental import pallas as pl
from jax.experimental.pallas import tpu as pltpu

F32 = jnp.float32
BF16 = jnp.bfloat16

NORM_EPS = 1e-6
LRU_C = 8.0
ROPE_THETA = 10000.0
CONV_WIDTH = 4
LOG2_E = 1.4426950408889634
HEAD_DIM = 128
HGRN_CHUNK = 128
SUBLANES = 8
HGRN_SUB = SUBLANES
VMEM_LIMIT = 56 * 1024 * 1024


def _params(semantics):
    return pltpu.CompilerParams(dimension_semantics=semantics, vmem_limit_bytes=VMEM_LIMIT)


def _sigmoid(x):
    return 0.5 * jnp.tanh(0.5 * x) + 0.5


def _rmsnorm_kernel(x_ref, g_ref, o_ref):
    x = x_ref[...]
    ms = jnp.mean(x * x, axis=-1, keepdims=True)
    o_ref[...] = (x * lax.rsqrt(ms + NORM_EPS) * g_ref[...]).astype(o_ref.dtype)


def _rmsnorm(x, gain, tm=256):
    n, d = x.shape
    return pl.pallas_call(
        _rmsnorm_kernel,
        grid=(n // tm,),
        in_specs=[pl.BlockSpec((tm, d), lambda i: (i, 0)),
                  pl.BlockSpec((1, d), lambda i: (0, 0))],
        out_specs=pl.BlockSpec((tm, d), lambda i: (i, 0)),
        out_shape=jax.ShapeDtypeStruct((n, d), BF16),
        compiler_params=_params(("parallel",)),
        name="rmsnorm",
    )(x, gain.reshape(1, d))


LANES = 128


def _row_factor(ss_ref, width, tn):
    rstd = lax.rsqrt(ss_ref[...] * (1.0 / width) + NORM_EPS)
    return jnp.tile(rstd, (1, tn // LANES))


def _finish(acc, j, *, mode, r_ref, g_ref, o_ref, xg_ref, ss_ref):
    if mode == "relu2":
        acc = jnp.square(jnp.maximum(acc, 0.0))
    elif mode == "residual":
        acc = acc + r_ref[...]
    o_ref[...] = acc.astype(o_ref.dtype)
    if xg_ref is not None:
        xg_ref[...] = (acc * g_ref[...]).astype(xg_ref.dtype)

        @pl.when(j == 0)
        def _():
            ss_ref[...] = jnp.zeros_like(ss_ref)

        ss_ref[...] += jnp.sum(acc * acc, axis=-1, keepdims=True)


def _unpack(refs, n_a, row_scale, mode, emit_norm, n_cast):
    it = iter(refs)
    a_refs = [next(it) for _ in range(n_a)]
    w_ref = next(it)
    ss_in = next(it) if row_scale else None
    r_ref = next(it) if mode == "residual" else None
    g_ref = next(it) if emit_norm else None
    cast_src = [next(it) for _ in range(n_cast)]
    o_ref = next(it)
    xg_ref = next(it) if emit_norm else None
    ss_ref = next(it) if emit_norm else None
    for src_ref in cast_src:
        dst_ref = next(it)
        dst_ref[...] = src_ref[...].astype(dst_ref.dtype)
    return a_refs, w_ref, ss_in, r_ref, g_ref, o_ref, xg_ref, ss_ref, list(it)


def _mm_fullk_kernel(*refs, splits, mode, row_scale, emit_norm, n_cast):
    a_refs, w_ref, ss_in, r_ref, g_ref, o_ref, xg_ref, ss_ref, _ = _unpack(
        refs, len(splits), row_scale, mode, emit_norm, n_cast)
    acc, off = None, 0
    for a_ref, kp in zip(a_refs, splits):
        part = jnp.dot(a_ref[...], w_ref[off:off + kp, :], preferred_element_type=F32)
        acc = part if acc is None else acc + part
        off += kp
    if row_scale:
        acc = acc * _row_factor(ss_in, off, o_ref.shape[-1])
    _finish(acc, pl.program_id(1), mode=mode, r_ref=r_ref, g_ref=g_ref, o_ref=o_ref,
            xg_ref=xg_ref, ss_ref=ss_ref)


def _mm_kloop_kernel(*refs, nk, mode, emit_norm):
    a_refs, w_ref, _, r_ref, g_ref, o_ref, xg_ref, ss_ref, (acc_ref,) = _unpack(
        refs, 1, False, mode, emit_norm, 0)
    k = pl.program_id(2)

    @pl.when(k == 0)
    def _():
        acc_ref[...] = jnp.zeros_like(acc_ref)

    acc_ref[...] += jnp.dot(a_refs[0][...], w_ref[...], preferred_element_type=F32)

    @pl.when(k == nk - 1)
    def _():
        _finish(acc_ref[...], pl.program_id(1), mode=mode, r_ref=r_ref, g_ref=g_ref, o_ref=o_ref,
                xg_ref=xg_ref, ss_ref=ss_ref)


def _matmul(a_parts, w, *, mode="plain", residual=None, row_ss=None, next_gain=None, casts=(),
            out_dtype=F32, tm=1024, tn=1024, tk=2048):
    m = a_parts[0].shape[0]
    splits = tuple(p.shape[1] for p in a_parts)
    kdim = sum(splits)
    n = w.shape[-1]
    fullk = kdim <= 4096
    row_scale = row_ss is not None
    emit_norm = next_gain is not None
    tm, tn = min(tm, m), min(tn, n)

    if fullk:
        grid = (m // tm, n // tn)
        tile = lambda i, j: (i, j)
        rows = lambda i, j: (i, 0)
        cols = lambda i, j: (0, j)
        in_specs = [pl.BlockSpec((tm, kp), rows) for kp in splits]
        in_specs.append(pl.BlockSpec((kdim, tn), cols))
        body = functools.partial(_mm_fullk_kernel, splits=splits, mode=mode, row_scale=row_scale,
                                 emit_norm=emit_norm, n_cast=len(casts))
        scratch = []
        semantics = ("parallel", "arbitrary")
    else:
        assert len(a_parts) == 1 and not row_scale and not casts
        tk = min(tk, kdim)
        nk = kdim // tk
        grid = (m // tm, n // tn, nk)
        tile = lambda i, j, k: (i, j)
        rows = lambda i, j, k: (i, 0)
        cols = lambda i, j, k: (0, j)
        in_specs = [pl.BlockSpec((tm, tk), lambda i, j, k: (i, k)),
                    pl.BlockSpec((tk, tn), lambda i, j, k: (k, j))]
        body = functools.partial(_mm_kloop_kernel, nk=nk, mode=mode, emit_norm=emit_norm)
        scratch = [pltpu.VMEM((tm, tn), F32)]
        semantics = ("parallel", "arbitrary", "arbitrary")

    args = list(a_parts) + [w]
    if row_scale:
        in_specs.append(pl.BlockSpec((tm, LANES), rows))
        args.append(row_ss)
    if mode == "residual":
        in_specs.append(pl.BlockSpec((tm, tn), tile))
        args.append(residual)
    out_specs = [pl.BlockSpec((tm, tn), tile)]
    out_shape = [jax.ShapeDtypeStruct((m, n), out_dtype)]
    if emit_norm:
        in_specs.append(pl.BlockSpec((1, tn), cols))
        args.append(next_gain.reshape(1, n))
        out_specs += [pl.BlockSpec((tm, tn), tile), pl.BlockSpec((tm, LANES), rows)]
        out_shape += [jax.ShapeDtypeStruct((m, n), BF16), jax.ShapeDtypeStruct((m, LANES), F32)]
    for w_f32, layer in casts:
        _, wr, wc = w_f32.shape
        col_blocks = max(c for c in range(1, grid[1] + 1)
                         if grid[1] % c == 0 and wc % (c * LANES) == 0)
        br, bc = wr // grid[0], wc // col_blocks
        assert wr % grid[0] == 0 and br % 8 == 0
        col = lambda j, last=col_blocks - 1: jnp.minimum(j, last)
        in_specs.append(pl.BlockSpec((None, br, bc),
                                     lambda i, j, layer=layer, col=col: (layer, i, col(j))))
        args.append(w_f32)
        out_specs.append(pl.BlockSpec((br, bc), lambda i, j, col=col: (i, col(j))))
        out_shape.append(jax.ShapeDtypeStruct((wr, wc), BF16))
    outs = pl.pallas_call(
        body, grid=grid, in_specs=in_specs, out_specs=out_specs, out_shape=out_shape,
        scratch_shapes=scratch, compiler_params=_params(semantics),
        name=("matmul_fullk_" if fullk else "matmul_kloop_") + mode,
    )(*args)
    return outs if len(outs) > 1 else outs[0]


def _tile_roll(x, shift):
    rows, d = x.shape
    x3 = x.reshape(rows // SUBLANES, SUBLANES, d)
    return pltpu.roll(x3, shift, 1).reshape(rows, d)


def _lru_kernel(zg_ref, zx_ref, cw_ref, cb_ref, wr_ref, br_ref, wi_ref, bi_ref, lam_ref, ng_ref,
                o_ref, xprev_ref, hprev_ref, a_s, u_s, *, rows, heads):
    s = pl.program_id(1)
    width = heads * HEAD_DIM
    tiles = rows // SUBLANES

    @pl.when(s == 0)
    def _():
        xprev_ref[...] = jnp.zeros_like(xprev_ref)
        hprev_ref[...] = jnp.zeros_like(hprev_ref)

    assert zx_ref.dtype == BF16
    xzb = zx_ref[...]
    xz = xzb.astype(F32)
    prev8 = xprev_ref[...]
    row8 = lax.broadcasted_iota(jnp.int32, (SUBLANES, width), 0)
    lag = (lax.broadcasted_iota(jnp.int32, (rows, rows), 0)
           - lax.broadcasted_iota(jnp.int32, (rows, rows), 1))
    cw = cw_ref[...]
    xa = cw[CONV_WIDTH - 1:CONV_WIDTH] * xz + cb_ref[...]
    for d in range(1, CONV_WIDTH):
        moved = jnp.dot((lag == d).astype(BF16), xzb, preferred_element_type=F32)
        head = jnp.where(row8 < d, pltpu.roll(prev8, d, 0), moved[0:SUBLANES])
        shifted = jnp.concatenate([head, moved[SUBLANES:]], axis=0)
        xa = xa + cw[CONV_WIDTH - 1 - d:CONV_WIDTH - d] * shifted
    xprev_ref[...] = xz[rows - SUBLANES:rows]

    xab = xa.astype(BF16)
    r_parts, i_parts = [], []
    for h in range(heads):
        xh = xab[:, h * HEAD_DIM:(h + 1) * HEAD_DIM]
        r_parts.append(jnp.dot(xh, wr_ref[h], preferred_element_type=F32))
        i_parts.append(jnp.dot(xh, wi_ref[h], preferred_element_type=F32))
    r = _sigmoid(jnp.concatenate(r_parts, axis=1) + br_ref[...])
    gate_i = _sigmoid(jnp.concatenate(i_parts, axis=1) + bi_ref[...])

    lam = lam_ref[...]
    softplus_neg_lam = jnp.maximum(-lam, 0.0) + jnp.log1p(jnp.exp(-jnp.abs(lam)))
    a = jnp.exp((-LRU_C) * r * softplus_neg_lam)
    gap = 1.0 - a * a
    mult = gap * lax.rsqrt(jnp.maximum(gap, 1e-30))
    row = lax.broadcasted_iota(jnp.int32, (rows, width), 0)
    mult = jnp.where(jnp.logical_and(row == 0, s == 0), 1.0, mult)
    u = mult * (gate_i * xa)

    sub = jnp.bitwise_and(row, SUBLANES - 1)
    d = 1
    while d < SUBLANES:
        a_sh = _tile_roll(a, d)
        u_sh = _tile_roll(u, d)
        valid = sub >= d
        u = jnp.where(valid, u + a * u_sh, u)
        a = jnp.where(valid, a * a_sh, a)
        d *= 2
    a_s[...] = a
    u_s[...] = u
    carry = hprev_ref[0:1, :]
    for t in range(tiles):
        grp = slice(t * SUBLANES, (t + 1) * SUBLANES)
        a_g, u_g = a_s[grp, :], u_s[grp, :]
        u_s[grp, :] = u_g + a_g * carry
        carry = u_g[SUBLANES - 1:SUBLANES] + a_g[SUBLANES - 1:SUBLANES] * carry
    hprev_ref[...] = jnp.broadcast_to(carry, hprev_ref.shape)
    h = u_s[...]

    zg = zg_ref[...].astype(F32)
    gelu = 0.5 * zg * (1.0 + jnp.tanh(math.sqrt(2.0 / math.pi) * (zg + 0.044715 * (zg * zg * zg))))
    y = h * gelu
    ms = jnp.mean(y * y, axis=-1, keepdims=True)
    o_ref[...] = (y * lax.rsqrt(ms + NORM_EPS) * ng_ref[...]).astype(o_ref.dtype)


def _lru(z, conv_w, conv_b, w_r, b_r, w_i, b_i, lam, norm_gain, *, batch, seq, rows=256):
    heads = w_r.shape[0]
    width = heads * HEAD_DIM
    nblk = seq // rows
    row_spec = lambda c: pl.BlockSpec((rows, width), lambda b, s, c=c: (b * nblk + s, c))
    vec = pl.BlockSpec((1, width), lambda b, s: (0, 0))
    wspec = pl.BlockSpec((heads, HEAD_DIM, HEAD_DIM), lambda b, s: (0, 0, 0))
    return pl.pallas_call(
        functools.partial(_lru_kernel, rows=rows, heads=heads),
        grid=(batch, nblk),
        in_specs=[row_spec(0), row_spec(1),
                  pl.BlockSpec((CONV_WIDTH, width), lambda b, s: (0, 0)), vec,
                  wspec, vec, wspec, vec, vec, vec],
        out_specs=pl.BlockSpec((rows, width), lambda b, s: (b * nblk + s, 0)),
        out_shape=jax.ShapeDtypeStruct((batch * seq, width), BF16),
        scratch_shapes=[pltpu.VMEM((SUBLANES, width), F32), pltpu.VMEM((SUBLANES, width), F32),
                        pltpu.VMEM((rows, width), F32), pltpu.VMEM((rows, width), F32)],
        compiler_params=_params(("parallel", "arbitrary")),
        name="rg_lru",
    )(z, z, conv_w, conv_b.reshape(1, width), w_r.astype(BF16), b_r.reshape(1, width),
      w_i.astype(BF16), b_i.reshape(1, width), lam.reshape(1, width), norm_gain.reshape(1, width))


def _split3(x):
    hi = x.astype(BF16)
    r1 = x - hi.astype(F32)
    mid = r1.astype(BF16)
    lo = (r1 - mid.astype(F32)).astype(BF16)
    return hi, mid, lo


def _hgrn_kernel(zq_ref, zf_ref, zv_ref, zg_ref, lbp_ref, ng_ref, o_ref, st_s, *, rows, layer):
    s = pl.program_id(2)
    C, c = HGRN_CHUNK, HGRN_SUB
    D = HEAD_DIM

    @pl.when(s == 0)
    def _():
        st_s[...] = jnp.zeros_like(st_s)

    lbp = lbp_ref[...].astype(F32)
    e = jnp.exp(lbp - jnp.max(lbp, axis=0, keepdims=True))
    sm = e / jnp.sum(e, axis=0, keepdims=True)
    lb = jnp.zeros((1, D), F32)
    for r_ in range(1, layer + 1):
        lb = lb + sm[r_:r_ + 1]

    row = lax.broadcasted_iota(jnp.int32, (C, C), 0)
    col = lax.broadcasted_iota(jnp.int32, (C, C), 1)
    tri = (row >= col).astype(BF16)
    differ = jnp.bitwise_xor(row, col)
    levels = []
    m = C // 2
    while m >= c:
        levels.append((m, (row > col) & (differ >= m) & (differ < 2 * m)))
        m //= 2
    diag_masks = [(differ < c) & (row - col == dlt) for dlt in range(c)]

    for ch in range(rows // C):
        sl = slice(ch * C, (ch + 1) * C)
        zq = zq_ref[sl, :].astype(F32)
        zf = zf_ref[sl, :].astype(F32)
        q = zq * _sigmoid(zq) * (D ** -0.5)
        half_tanh = 0.5 * jnp.tanh(0.5 * zf)
        f = lb + (1.0 - lb) * (0.5 + half_tanh)
        k = (1.0 - lb) * (0.5 - half_tanh)
        log_sig = jnp.minimum(zf, 0.0) - jnp.log(1.0 + jnp.exp(-jnp.abs(zf)))
        y = jnp.log1p(-lb) + log_sig
        if layer == 0:
            log_f = y
        else:
            la = jnp.log(lb)
            log_f = jnp.maximum(la, y) + jnp.log(1.0 + jnp.exp(-jnp.abs(la - y)))

        hi, mid, lo = _split3(log_f)
        g3 = jnp.dot(tri, jnp.concatenate([hi, mid, lo], axis=1), preferred_element_type=F32)
        G = g3[:, 0:D] + g3[:, D:2 * D] + g3[:, 2 * D:3 * D]
        vb = zv_ref[sl, :].astype(BF16)
        g_last = G[C - 1:C]

        a = jnp.zeros((C, C), F32)
        dec = None
        for dlt in range(c):
            if dlt == 0:
                w = q * k
            else:
                f_sh = f if dlt == 1 else _tile_roll(f, dlt - 1)
                dec = f_sh if dlt == 1 else dec * f_sh
                w = q * _tile_roll(k, dlt) * dec
            a = jnp.where(diag_masks[dlt], jnp.sum(w, axis=-1, keepdims=True), a)

        for m, mask in levels:
            g_bnd = jnp.concatenate(
                [jnp.broadcast_to(G[p * 2 * m + m - 1:p * 2 * m + m], (2 * m, D))
                 for p in range(C // (2 * m))], axis=0)
            e_ref = jnp.exp(-jnp.abs(G - g_bnd))
            sc = lax.dot_general((q * e_ref).astype(BF16), (k * e_ref).astype(BF16),
                                 (((1,), (1,)), ((), ())), preferred_element_type=F32)
            a = jnp.where(mask, sc, a)

        st = st_s[...]
        qg = (q * jnp.exp(G)).astype(BF16)
        o = (jnp.dot(a.astype(BF16), vb, preferred_element_type=F32)
             + lax.dot_general(qg, st.astype(BF16), (((1,), (1,)), ((), ())),
                               preferred_element_type=F32))
        kd = (k * jnp.exp(g_last - G)).astype(BF16)
        st_s[...] = st * jnp.exp(g_last) + lax.dot_general(
            vb, kd, (((0,), (0,)), ((), ())), preferred_element_type=F32)

        ms = jnp.mean(o * o, axis=-1, keepdims=True)
        zg = zg_ref[sl, :].astype(F32)
        o_ref[sl, :] = (o * lax.rsqrt(ms + NORM_EPS) * ng_ref[...]
                        * (zg * _sigmoid(zg))).astype(o_ref.dtype)


def _hgrn(z, lower_bounds, norm_gain, *, layer, batch, seq, heads, col0, rows=512):
    nblk = seq // rows
    D = HEAD_DIM
    depth = lower_bounds.shape[0]

    def zspec(group):
        off = col0 // D + group * heads
        return pl.BlockSpec((rows, D), lambda b, h, s, off=off: (b * nblk + s, off + h))

    return pl.pallas_call(
        functools.partial(_hgrn_kernel, rows=rows, layer=layer),
        grid=(batch, heads, nblk),
        in_specs=[zspec(0), zspec(1), zspec(2), zspec(3),
                  pl.BlockSpec((depth, D), lambda b, h, s: (0, h)),
                  pl.BlockSpec((1, D), lambda b, h, s: (0, 0))],
        out_specs=pl.BlockSpec((rows, D), lambda b, h, s: (b * nblk + s, h)),
        out_shape=jax.ShapeDtypeStruct((batch * seq, heads * D), BF16),
        scratch_shapes=[pltpu.VMEM((D, D), F32)],
        compiler_params=_params(("parallel", "parallel", "arbitrary")),
        name="hgrn2",
    )(z, z, z, z, lower_bounds, norm_gain.reshape(1, D))


def _norm_rope(x, gain, cos, sin_signed, scale):
    ms = jnp.mean(x * x, axis=-1, keepdims=True)
    xn = x * lax.rsqrt(ms + NORM_EPS) * gain
    return (xn * cos + pltpu.roll(xn, HEAD_DIM // 2, 1) * sin_signed) * scale


def _flash_kernel(zq_ref, zk_ref, v_ref, cos_ref, sin_ref, qg_ref, kg_ref, lp_ref, g_ref, o_ref,
                  q_s, k_s, m_s, l_s, acc_s, a_s, p_s, *, blk, seq, lam_init):
    i = pl.program_id(2)
    D = HEAD_DIM
    lanes = blk // D

    @pl.when(i == 0)
    def _():
        def prep_keys(r, carry):
            rows = pl.ds(pl.multiple_of(r * blk, blk), blk)
            for c in range(2):
                k_s[rows, c * D:(c + 1) * D] = _norm_rope(
                    zk_ref[rows, c * D:(c + 1) * D].astype(F32), kg_ref[...],
                    cos_ref[rows, :], sin_ref[rows, :], 1.0).astype(k_s.dtype)
            return carry

        lax.fori_loop(0, seq // blk, prep_keys, 0)

    q_rows = pl.ds(pl.multiple_of(i * blk, blk), blk)
    for c in range(2):
        q_s[:, c * D:(c + 1) * D] = _norm_rope(
            zq_ref[:, c * D:(c + 1) * D].astype(F32), qg_ref[...],
            cos_ref[q_rows, :], sin_ref[q_rows, :], D ** -0.5 * LOG2_E).astype(q_s.dtype)
    q_ref, k_ref = q_s, k_s

    m_s[...] = jnp.full_like(m_s, -jnp.inf)
    l_s[...] = jnp.zeros_like(l_s)
    acc_s[...] = jnp.zeros_like(acc_s)

    def scores(t, masked, maps=(0, 1)):
        k = k_ref[pl.ds(pl.multiple_of(t * blk, blk), blk), :]
        for c in maps:
            q = q_ref[:, c * D:(c + 1) * D]
            s = lax.dot_general(q, k[:, c * D:(c + 1) * D], (((1,), (1,)), ((), ())),
                                preferred_element_type=F32)
            if masked:
                row = lax.broadcasted_iota(jnp.int32, (blk, blk), 0)
                col = lax.broadcasted_iota(jnp.int32, (blk, blk), 1)
                s = jnp.where(col <= row, s, -jnp.inf)
            m_prev = m_s[c]
            m_new = jnp.maximum(m_prev, jnp.max(s, axis=-1, keepdims=True))
            alpha = jnp.exp2(m_prev - m_new)
            p = jnp.exp2(s - jnp.tile(m_new, (1, lanes)))
            l_s[c] = alpha * l_s[c] + jnp.sum(p, axis=-1, keepdims=True)
            m_s[c] = m_new
            a_s[c] = alpha
            p_s[c] = p.astype(BF16)

    def weighted_values(t, maps=(0, 1)):
        v = v_ref[pl.ds(pl.multiple_of(t * blk, blk), blk), :]
        for c in maps:
            acc_s[c] = jnp.tile(a_s[c], (1, 2)) * acc_s[c] + jnp.dot(
                p_s[c], v, preferred_element_type=F32)

    @pl.when(i == 0)
    def _():
        scores(0, True)

    @pl.when(i > 0)
    def _():
        scores(0, False)

        def pipelined(t, carry):
            for c in range(2):
                weighted_values(t - 1, (c,))
                scores(t, False, (c,))
            return carry

        lax.fori_loop(1, i, pipelined, 0)
        weighted_values(i - 1)
        scores(i, True)

    weighted_values(i)

    lp = lp_ref[...].astype(F32)
    lam = (jnp.exp(jnp.sum(lp[0:1] * lp[1:2], axis=-1, keepdims=True))
           - jnp.exp(jnp.sum(lp[2:3] * lp[3:4], axis=-1, keepdims=True)) + lam_init)
    o = (acc_s[0] * jnp.tile(1.0 / l_s[0], (1, 2))
         - lam * (acc_s[1] * jnp.tile(1.0 / l_s[1], (1, 2))))
    ms = jnp.mean(o * o, axis=-1, keepdims=True)
    o_ref[...] = (o * lax.rsqrt(ms + NORM_EPS) * g_ref[...] * (1.0 - lam_init)).astype(o_ref.dtype)


def _flash(z, cos, sin_signed, q_gain, k_gain, lam_params, subln_gain, *, layer, batch, seq, heads,
           col0, blk=512):
    nblk = seq // blk
    D = HEAD_DIM
    lam_init = 0.8 - 0.6 * math.exp(-0.3 * layer)
    cb = col0 // (2 * D)
    qspec = pl.BlockSpec((blk, 2 * D), lambda b, h, i: (b * nblk + i, cb + h))
    kspec = pl.BlockSpec((seq, 2 * D), lambda b, h, i: (b, cb + heads + h))
    vspec = pl.BlockSpec((seq, 2 * D), lambda b, h, i: (b, cb + 2 * heads + h))
    table = pl.BlockSpec((seq, D), lambda b, h, i: (0, 0))
    vec = pl.BlockSpec((1, D), lambda b, h, i: (0, 0))
    return pl.pallas_call(
        functools.partial(_flash_kernel, blk=blk, seq=seq, lam_init=lam_init),
        grid=(batch, heads, nblk),
        in_specs=[qspec, kspec, vspec, table, table, vec, vec,
                  pl.BlockSpec((4, D), lambda b, h, i: (0, 0)),
                  pl.BlockSpec((1, 2 * D), lambda b, h, i: (0, 0))],
        out_specs=pl.BlockSpec((blk, 2 * D), lambda b, h, i: (b * nblk + i, h)),
        out_shape=jax.ShapeDtypeStruct((batch * seq, heads * 2 * D), BF16),
        scratch_shapes=[pltpu.VMEM((blk, 2 * D), BF16), pltpu.VMEM((seq, 2 * D), BF16),
                        pltpu.VMEM((2, blk, D), F32), pltpu.VMEM((2, blk, D), F32),
                        pltpu.VMEM((2, blk, 2 * D), F32), pltpu.VMEM((2, blk, D), F32),
                        pltpu.VMEM((2, blk, blk), BF16)],
        compiler_params=_params(("parallel", "parallel", "arbitrary")),
        name="diff_flash",
    )(z, z, z, cos, sin_signed, q_gain.reshape(1, D), k_gain.reshape(1, D), lam_params,
      subln_gain.reshape(1, 2 * D))


def kernel(x, ln1_gain, w_in, conv_w, conv_b, lru_w_r, lru_b_r, lru_w_i, lru_b_i, lru_lambda, lru_norm_gain, hgrn_lower_bounds, hgrn_norm_gain, q_norm_gain, k_norm_gain, diff_lambda, diff_subln_gain, w_out, ln2_gain, w_ff1, w_ff2):
    batch, seq, d_model = x.shape
    depth = w_in.shape[0]
    D = HEAD_DIM
    lru_width = conv_w.shape[-1]
    key_width = hgrn_lower_bounds.shape[-1]
    in_width = w_in.shape[-1]
    mix_width = w_out.shape[1]
    diff_width = in_width - 2 * key_width - 2 * mix_width
    val_width = mix_width - lru_width - diff_width
    assert q_norm_gain.shape[-1] == D and hgrn_norm_gain.shape[-1] == D
    assert lru_w_r.shape[-1] == D and key_width == val_width
    hgrn_heads = key_width // D
    diff_heads = diff_width // (2 * D)
    off_b = 2 * lru_width
    off_c = off_b + 2 * key_width + 2 * val_width

    inv = 1.0 / (ROPE_THETA ** (jnp.arange(0, D, 2, dtype=F32) / D))
    ang = jnp.arange(seq, dtype=F32)[:, None] * inv[None, :]
    ang = jnp.concatenate([ang, ang], axis=-1)
    cos = jnp.cos(ang)
    sign = jnp.concatenate([-jnp.ones((D // 2,), F32), jnp.ones((D // 2,), F32)])
    sin_signed = jnp.sin(ang) * sign[None, :]

    xf = x.reshape(batch * seq, d_model)
    xg, ss = _rmsnorm(xf, ln1_gain[0]), None
    w_in_b = w_in[0].astype(BF16)
    for l in range(depth):
        z, w_out_b = _matmul([xg], w_in_b, row_ss=ss, out_dtype=BF16, casts=[(w_out, l)])

        ya = _lru(z, conv_w[l], conv_b[l], lru_w_r[l], lru_b_r[l], lru_w_i[l], lru_b_i[l],
                  lru_lambda[l], lru_norm_gain[l], batch=batch, seq=seq)
        yb = _hgrn(z, hgrn_lower_bounds, hgrn_norm_gain[l], layer=l, batch=batch, seq=seq,
                   heads=hgrn_heads, col0=off_b)
        yc = _flash(z, cos, sin_signed, q_norm_gain[l], k_norm_gain[l], diff_lambda[l],
                    diff_subln_gain[l], layer=l, batch=batch, seq=seq, heads=diff_heads, col0=off_c)

        xf, xg, ss, w_ff1_b = _matmul([ya, yb, yc], w_out_b, mode="residual", residual=xf,
                                      next_gain=ln2_gain[l], tn=512, casts=[(w_ff1, l)])
        ff1_casts = [(w_ff2, l)] + ([(w_in, l + 1)] if l + 1 < depth else [])
        u, w_ff2_b, *rest = _matmul([xg], w_ff1_b, mode="relu2", row_ss=ss, out_dtype=BF16,
                                    casts=ff1_casts)
        if l + 1 < depth:
            w_in_b = rest[0]
            xf, xg, ss = _matmul([u], w_ff2_b, mode="residual", residual=xf,
                                 next_gain=ln1_gain[l + 1])
        else:
            xf = _matmul([u], w_ff2_b, mode="residual", residual=xf)
    return xf.reshape(batch, seq, d_model)
```

```python
import functools
import math

import jax
import jax.numpy as jnp
from jax import lax
from jax.experimental import pallas as pl
from jax.experimental.pallas import tpu as pltpu

F32 = jnp.float32
BF16 = jnp.bfloat16

NORM_EPS = 1e-6
LRU_C = 8.0
ROPE_THETA = 10000.0
CONV_WIDTH = 4
LOG2_E = 1.4426950408889634
HEAD_DIM = 128
HGRN_CHUNK = 128
SUBLANES = 8
HGRN_SUB = SUBLANES
VMEM_LIMIT = 56 * 1024 * 1024


def _params(semantics):
    return pltpu.CompilerParams(dimension_semantics=semantics, vmem_limit_bytes=VMEM_LIMIT)


def _sigmoid(x):
    return 0.5 * jnp.tanh(0.5 * x) + 0.5


def _rmsnorm_kernel(x_ref, g_ref, o_ref):
    x = x_ref[...]
    ms = jnp.mean(x * x, axis=-1, keepdims=True)
    o_ref[...] = (x * lax.rsqrt(ms + NORM_EPS) * g_ref[...]).astype(o_ref.dtype)


def _rmsnorm(x, gain, tm=256):
    n, d = x.shape
    return pl.pallas_call(
        _rmsnorm_kernel,
        grid=(n // tm,),
        in_specs=[pl.BlockSpec((tm, d), lambda i: (i, 0)),
                  pl.BlockSpec((1, d), lambda i: (0, 0))],
        out_specs=pl.BlockSpec((tm, d), lambda i: (i, 0)),
        out_shape=jax.ShapeDtypeStruct((n, d), BF16),
        compiler_params=_params(("parallel",)),
        name="rmsnorm",
    )(x, gain.reshape(1, d))


LANES = 128


def _row_factor(ss_ref, width, tn):
    rstd = lax.rsqrt(ss_ref[...] * (1.0 / width) + NORM_EPS)
    return jnp.tile(rstd, (1, tn // LANES))


def _finish(acc, j, *, mode, r_ref, g_ref, o_ref, xg_ref, ss_ref):
    if mode == "relu2":
        acc = jnp.square(jnp.maximum(acc, 0.0))
    elif mode == "residual":
        acc = acc + r_ref[...]
    o_ref[...] = acc.astype(o_ref.dtype)
    if xg_ref is not None:
        xg_ref[...] = (acc * g_ref[...]).astype(xg_ref.dtype)

        @pl.when(j == 0)
        def _():
            ss_ref[...] = jnp.zeros_like(ss_ref)

        ss_ref[...] += jnp.sum(acc * acc, axis=-1, keepdims=True)


def _unpack(refs, n_a, row_scale, mode, emit_norm, n_cast):
    it = iter(refs)
    a_refs = [next(it) for _ in range(n_a)]
    w_ref = next(it)
    ss_in = next(it) if row_scale else None
    r_ref = next(it) if mode == "residual" else None
    g_ref = next(it) if emit_norm else None
    cast_src = [next(it) for _ in range(n_cast)]
    o_ref = next(it)
    xg_ref = next(it) if emit_norm else None
    ss_ref = next(it) if emit_norm else None
    for src_ref in cast_src:
        dst_ref = next(it)
        dst_ref[...] = src_ref[...].astype(dst_ref.dtype)
    return a_refs, w_ref, ss_in, r_ref, g_ref, o_ref, xg_ref, ss_ref, list(it)


def _mm_fullk_kernel(*refs, splits, mode, row_scale, emit_norm, n_cast):
    a_refs, w_ref, ss_in, r_ref, g_ref, o_ref, xg_ref, ss_ref, _ = _unpack(
        refs, len(splits), row_scale, mode, emit_norm, n_cast)
    acc, off = None, 0
    for a_ref, kp in zip(a_refs, splits):
        part = jnp.dot(a_ref[...], w_ref[off:off + kp, :], preferred_element_type=F32)
        acc = part if acc is None else acc + part
        off += kp
    if row_scale:
        acc = acc * _row_factor(ss_in, off, o_ref.shape[-1])
    _finish(acc, pl.program_id(1), mode=mode, r_ref=r_ref, g_ref=g_ref, o_ref=o_ref,
            xg_ref=xg_ref, ss_ref=ss_ref)


def _mm_kloop_kernel(*refs, nk, mode, emit_norm):
    a_refs, w_ref, _, r_ref, g_ref, o_ref, xg_ref, ss_ref, (acc_ref,) = _unpack(
        refs, 1, False, mode, emit_norm, 0)
    k = pl.program_id(2)

    @pl.when(k == 0)
    def _():
        acc_ref[...] = jnp.zeros_like(acc_ref)

    acc_ref[...] += jnp.dot(a_refs[0][...], w_ref[...], preferred_element_type=F32)

    @pl.when(k == nk - 1)
    def _():
        _finish(acc_ref[...], pl.program_id(1), mode=mode, r_ref=r_ref, g_ref=g_ref, o_ref=o_ref,
                xg_ref=xg_ref, ss_ref=ss_ref)


def _matmul(a_parts, w, *, mode="plain", residual=None, row_ss=None, next_gain=None, casts=(),
            out_dtype=F32, tm=1024, tn=1024, tk=2048):
    m = a_parts[0].shape[0]
    splits = tuple(p.shape[1] for p in a_parts)
    kdim = sum(splits)
    n = w.shape[-1]
    fullk = kdim <= 4096
    row_scale = row_ss is not None
    emit_norm = next_gain is not None
    tm, tn = min(tm, m), min(tn, n)

    if fullk:
        grid = (m // tm, n // tn)
        tile = lambda i, j: (i, j)
        rows = lambda i, j: (i, 0)
        cols = lambda i, j: (0, j)
        in_specs = [pl.BlockSpec((tm, kp), rows) for kp in splits]
        in_specs.append(pl.BlockSpec((kdim, tn), cols))
        body = functools.partial(_mm_fullk_kernel, splits=splits, mode=mode, row_scale=row_scale,
                                 emit_norm=emit_norm, n_cast=len(casts))
        scratch = []
        semantics = ("parallel", "arbitrary")
    else:
        assert len(a_parts) == 1 and not row_scale and not casts
        tk = min(tk, kdim)
        nk = kdim // tk
        grid = (m // tm, n // tn, nk)
        tile = lambda i, j, k: (i, j)
        rows = lambda i, j, k: (i, 0)
        cols = lambda i, j, k: (0, j)
        in_specs = [pl.BlockSpec((tm, tk), lambda i, j, k: (i, k)),
                    pl.BlockSpec((tk, tn), lambda i, j, k: (k, j))]
        body = functools.partial(_mm_kloop_kernel, nk=nk, mode=mode, emit_norm=emit_norm)
        scratch = [pltpu.VMEM((tm, tn), F32)]
        semantics = ("parallel", "arbitrary", "arbitrary")

    args = list(a_parts) + [w]
    if row_scale:
        in_specs.append(pl.BlockSpec((tm, LANES), rows))
        args.append(row_ss)
    if mode == "residual":
        in_specs.append(pl.BlockSpec((tm, tn), tile))
        args.append(residual)
    out_specs = [pl.BlockSpec((tm, tn), tile)]
    out_shape = [jax.ShapeDtypeStruct((m, n), out_dtype)]
    if emit_norm:
        in_specs.append(pl.BlockSpec((1, tn), cols))
        args.append(next_gain.reshape(1, n))
        out_specs += [pl.BlockSpec((tm, tn), tile), pl.BlockSpec((tm, LANES), rows)]
        out_shape += [jax.ShapeDtypeStruct((m, n), BF16), jax.ShapeDtypeStruct((m, LANES), F32)]
    for w_f32, layer in casts:
        _, wr, wc = w_f32.shape
        col_blocks = max(c for c in range(1, grid[1] + 1)
                         if grid[1] % c == 0 and wc % (c * LANES) == 0)
        br, bc = wr // grid[0], wc // col_blocks
        assert wr % grid[0] == 0 and br % 8 == 0
        col = lambda j, last=col_blocks - 1: jnp.minimum(j, last)
        in_specs.append(pl.BlockSpec((None, br, bc),
                                     lambda i, j, layer=layer, col=col: (layer, i, col(j))))
        args.append(w_f32)
        out_specs.append(pl.BlockSpec((br, bc), lambda i, j, col=col: (i, col(j))))
        out_shape.append(jax.ShapeDtypeStruct((wr, wc), BF16))
    outs = pl.pallas_call(
        body, grid=grid, in_specs=in_specs, out_specs=out_specs, out_shape=out_shape,
        scratch_shapes=scratch, compiler_params=_params(semantics),
        name=("matmul_fullk_" if fullk else "matmul_kloop_") + mode,
    )(*args)
    return outs if len(outs) > 1 else outs[0]


def _tile_roll(x, shift):
    rows, d = x.shape
    x3 = x.reshape(rows // SUBLANES, SUBLANES, d)
    return pltpu.roll(x3, shift, 1).reshape(rows, d)


def _lru_kernel(zg_ref, zx_ref, cw_ref, cb_ref, wr_ref, br_ref, wi_ref, bi_ref, lam_ref, ng_ref,
                o_ref, xprev_ref, hprev_ref, a_s, u_s, *, rows, heads):
    s = pl.program_id(1)
    width = heads * HEAD_DIM
    tiles = rows // SUBLANES

    @pl.when(s == 0)
    def _():
        xprev_ref[...] = jnp.zeros_like(xprev_ref)
        hprev_ref[...] = jnp.zeros_like(hprev_ref)

    assert zx_ref.dtype == BF16
    xzb = zx_ref[...]
    xz = xzb.astype(F32)
    prev8 = xprev_ref[...]
    row8 = lax.broadcasted_iota(jnp.int32, (SUBLANES, width), 0)
    lag = (lax.broadcasted_iota(jnp.int32, (rows, rows), 0)
           - lax.broadcasted_iota(jnp.int32, (rows, rows), 1))
    cw = cw_ref[...]
    xa = cw[CONV_WIDTH - 1:CONV_WIDTH] * xz + cb_ref[...]
    for d in range(1, CONV_WIDTH):
        moved = jnp.dot((lag == d).astype(BF16), xzb, preferred_element_type=F32)
        head = jnp.where(row8 < d, pltpu.roll(prev8, d, 0), moved[0:SUBLANES])
        shifted = jnp.concatenate([head, moved[SUBLANES:]], axis=0)
        xa = xa + cw[CONV_WIDTH - 1 - d:CONV_WIDTH - d] * shifted
    xprev_ref[...] = xz[rows - SUBLANES:rows]

    xab = xa.astype(BF16)
    r_parts, i_parts = [], []
    for h in range(heads):
        xh = xab[:, h * HEAD_DIM:(h + 1) * HEAD_DIM]
        r_parts.append(jnp.dot(xh, wr_ref[h], preferred_element_type=F32))
        i_parts.append(jnp.dot(xh, wi_ref[h], preferred_element_type=F32))
    r = _sigmoid(jnp.concatenate(r_parts, axis=1) + br_ref[...])
    gate_i = _sigmoid(jnp.concatenate(i_parts, axis=1) + bi_ref[...])

    lam = lam_ref[...]
    softplus_neg_lam = jnp.maximum(-lam, 0.0) + jnp.log1p(jnp.exp(-jnp.abs(lam)))
    a = jnp.exp((-LRU_C) * r * softplus_neg_lam)
    gap = 1.0 - a * a
    mult = gap * lax.rsqrt(jnp.maximum(gap, 1e-30))
    row = lax.broadcasted_iota(jnp.int32, (rows, width), 0)
    mult = jnp.where(jnp.logical_and(row == 0, s == 0), 1.0, mult)
    u = mult * (gate_i * xa)

    sub = jnp.bitwise_and(row, SUBLANES - 1)
    d = 1
    while d < SUBLANES:
        a_sh = _tile_roll(a, d)
        u_sh = _tile_roll(u, d)
        valid = sub >= d
        u = jnp.where(valid, u + a * u_sh, u)
        a = jnp.where(valid, a * a_sh, a)
        d *= 2
    a_s[...] = a
    u_s[...] = u
    carry = hprev_ref[0:1, :]
    for t in range(tiles):
        grp = slice(t * SUBLANES, (t + 1) * SUBLANES)
        a_g, u_g = a_s[grp, :], u_s[grp, :]
        u_s[grp, :] = u_g + a_g * carry
        carry = u_g[SUBLANES - 1:SUBLANES] + a_g[SUBLANES - 1:SUBLANES] * carry
    hprev_ref[...] = jnp.broadcast_to(carry, hprev_ref.shape)
    h = u_s[...]

    zg = zg_ref[...].astype(F32)
    gelu = 0.5 * zg * (1.0 + jnp.tanh(math.sqrt(2.0 / math.pi) * (zg + 0.044715 * (zg * zg * zg))))
    y = h * gelu
    ms = jnp.mean(y * y, axis=-1, keepdims=True)
    o_ref[...] = (y * lax.rsqrt(ms + NORM_EPS) * ng_ref[...]).astype(o_ref.dtype)


def _lru(z, conv_w, conv_b, w_r, b_r, w_i, b_i, lam, norm_gain, *, batch, seq, rows=256):
    heads = w_r.shape[0]
    width = heads * HEAD_DIM
    nblk = seq // rows
    row_spec = lambda c: pl.BlockSpec((rows, width), lambda b, s, c=c: (b * nblk + s, c))
    vec = pl.BlockSpec((1, width), lambda b, s: (0, 0))
    wspec = pl.BlockSpec((heads, HEAD_DIM, HEAD_DIM), lambda b, s: (0, 0, 0))
    return pl.pallas_call(
        functools.partial(_lru_kernel, rows=rows, heads=heads),
        grid=(batch, nblk),
        in_specs=[row_spec(0), row_spec(1),
                  pl.BlockSpec((CONV_WIDTH, width), lambda b, s: (0, 0)), vec,
                  wspec, vec, wspec, vec, vec, vec],
        out_specs=pl.BlockSpec((rows, width), lambda b, s: (b * nblk + s, 0)),
        out_shape=jax.ShapeDtypeStruct((batch * seq, width), BF16),
        scratch_shapes=[pltpu.VMEM((SUBLANES, width), F32), pltpu.VMEM((SUBLANES, width), F32),
                        pltpu.VMEM((rows, width), F32), pltpu.VMEM((rows, width), F32)],
        compiler_params=_params(("parallel", "arbitrary")),
        name="rg_lru",
    )(z, z, conv_w, conv_b.reshape(1, width), w_r.astype(BF16), b_r.reshape(1, width),
      w_i.astype(BF16), b_i.reshape(1, width), lam.reshape(1, width), norm_gain.reshape(1, width))


def _split3(x):
    hi = x.astype(BF16)
    r1 = x - hi.astype(F32)
    mid = r1.astype(BF16)
    lo = (r1 - mid.astype(F32)).astype(BF16)
    return hi, mid, lo


def _hgrn_kernel(zq_ref, zf_ref, zv_ref, zg_ref, lbp_ref, ng_ref, o_ref, st_s, *, rows, layer):
    s = pl.program_id(2)
    C, c = HGRN_CHUNK, HGRN_SUB
    D = HEAD_DIM

    @pl.when(s == 0)
    def _():
        st_s[...] = jnp.zeros_like(st_s)

    lbp = lbp_ref[...].astype(F32)
    e = jnp.exp(lbp - jnp.max(lbp, axis=0, keepdims=True))
    sm = e / jnp.sum(e, axis=0, keepdims=True)
    lb = jnp.zeros((1, D), F32)
    for r_ in range(1, layer + 1):
        lb = lb + sm[r_:r_ + 1]

    row = lax.broadcasted_iota(jnp.int32, (C, C), 0)
    col = lax.broadcasted_iota(jnp.int32, (C, C), 1)
    tri = (row >= col).astype(BF16)
    differ = jnp.bitwise_xor(row, col)
    levels = []
    m = C // 2
    while m >= c:
        levels.append((m, (row > col) & (differ >= m) & (differ < 2 * m)))
        m //= 2
    diag_masks = [(differ < c) & (row - col == dlt) for dlt in range(c)]

    for ch in range(rows // C):
        sl = slice(ch * C, (ch + 1) * C)
        zq = zq_ref[sl, :].astype(F32)
        zf = zf_ref[sl, :].astype(F32)
        q = zq * _sigmoid(zq) * (D ** -0.5)
        half_tanh = 0.5 * jnp.tanh(0.5 * zf)
        f = lb + (1.0 - lb) * (0.5 + half_tanh)
        k = (1.0 - lb) * (0.5 - half_tanh)
        log_sig = jnp.minimum(zf, 0.0) - jnp.log(1.0 + jnp.exp(-jnp.abs(zf)))
        y = jnp.log1p(-lb) + log_sig
        if layer == 0:
            log_f = y
        else:
            la = jnp.log(lb)
            log_f = jnp.maximum(la, y) + jnp.log(1.0 + jnp.exp(-jnp.abs(la - y)))

        hi, mid, lo = _split3(log_f)
        g3 = jnp.dot(tri, jnp.concatenate([hi, mid, lo], axis=1), preferred_element_type=F32)
        G = g3[:, 0:D] + g3[:, D:2 * D] + g3[:, 2 * D:3 * D]
        vb = zv_ref[sl, :].astype(BF16)
        g_last = G[C - 1:C]

        a = jnp.zeros((C, C), F32)
        dec = None
        for dlt in range(c):
            if dlt == 0:
                w = q * k
            else:
                f_sh = f if dlt == 1 else _tile_roll(f, dlt - 1)
                dec = f_sh if dlt == 1 else dec * f_sh
                w = q * _tile_roll(k, dlt) * dec
            a = jnp.where(diag_masks[dlt], jnp.sum(w, axis=-1, keepdims=True), a)

        for m, mask in levels:
            g_bnd = jnp.concatenate(
                [jnp.broadcast_to(G[p * 2 * m + m - 1:p * 2 * m + m], (2 * m, D))
                 for p in range(C // (2 * m))], axis=0)
            e_ref = jnp.exp(-jnp.abs(G - g_bnd))
            sc = lax.dot_general((q * e_ref).astype(BF16), (k * e_ref).astype(BF16),
                                 (((1,), (1,)), ((), ())), preferred_element_type=F32)
            a = jnp.where(mask, sc, a)

        st = st_s[...]
        qg = (q * jnp.exp(G)).astype(BF16)
        o = (jnp.dot(a.astype(BF16), vb, preferred_element_type=F32)
             + lax.dot_general(qg, st.astype(BF16), (((1,), (1,)), ((), ())),
                               preferred_element_type=F32))
        kd = (k * jnp.exp(g_last - G)).astype(BF16)
        st_s[...] = st * jnp.exp(g_last) + lax.dot_general(
            vb, kd, (((0,), (0,)), ((), ())), preferred_element_type=F32)

        ms = jnp.mean(o * o, axis=-1, keepdims=True)
        zg = zg_ref[sl, :].astype(F32)
        o_ref[sl, :] = (o * lax.rsqrt(ms + NORM_EPS) * ng_ref[...]
                        * (zg * _sigmoid(zg))).astype(o_ref.dtype)


def _hgrn(z, lower_bounds, norm_gain, *, layer, batch, seq, heads, col0, rows=512):
    nblk = seq // rows
    D = HEAD_DIM
    depth = lower_bounds.shape[0]

    def zspec(group):
        off = col0 // D + group * heads
        return pl.BlockSpec((rows, D), lambda b, h, s, off=off: (b * nblk + s, off + h))

    return pl.pallas_call(
        functools.partial(_hgrn_kernel, rows=rows, layer=layer),
        grid=(batch, heads, nblk),
        in_specs=[zspec(0), zspec(1), zspec(2), zspec(3),
                  pl.BlockSpec((depth, D), lambda b, h, s: (0, h)),
                  pl.BlockSpec((1, D), lambda b, h, s: (0, 0))],
        out_specs=pl.BlockSpec((rows, D), lambda b, h, s: (b * nblk + s, h)),
        out_shape=jax.ShapeDtypeStruct((batch * seq, heads * D), BF16),
        scratch_shapes=[pltpu.VMEM((D, D), F32)],
        compiler_params=_params(("parallel", "parallel", "arbitrary")),
        name="hgrn2",
    )(z, z, z, z, lower_bounds, norm_gain.reshape(1, D))


def _norm_rope(x, gain, cos, sin_signed, scale):
    ms = jnp.mean(x * x, axis=-1, keepdims=True)
    xn = x * lax.rsqrt(ms + NORM_EPS) * gain
    return (xn * cos + pltpu.roll(xn, HEAD_DIM // 2, 1) * sin_signed) * scale


def _flash_kernel(zq_ref, zk_ref, v_ref, cos_ref, sin_ref, qg_ref, kg_ref, lp_ref, g_ref, o_ref,
                  q_s, k_s, m_s, l_s, acc_s, a_s, p_s, *, blk, seq, heads, lam_init):
    i = pl.program_id(2)
    D = HEAD_DIM
    lanes = blk // D
    maps = tuple(range(2 * heads))

    @pl.when(i == 0)
    def _():
        def prep_keys(r, carry):
            rows = pl.ds(pl.multiple_of(r * blk, blk), blk)
            for e in maps:
                k_s[rows, e * D:(e + 1) * D] = _norm_rope(
                    zk_ref[rows, e * D:(e + 1) * D].astype(F32), kg_ref[...],
                    cos_ref[rows, :], sin_ref[rows, :], 1.0).astype(k_s.dtype)
            return carry

        lax.fori_loop(0, seq // blk, prep_keys, 0)

    q_rows = pl.ds(pl.multiple_of(i * blk, blk), blk)
    for e in maps:
        q_s[:, e * D:(e + 1) * D] = _norm_rope(
            zq_ref[:, e * D:(e + 1) * D].astype(F32), qg_ref[...],
            cos_ref[q_rows, :], sin_ref[q_rows, :], D ** -0.5 * LOG2_E).astype(q_s.dtype)

    m_s[...] = jnp.full_like(m_s, -jnp.inf)
    l_s[...] = jnp.zeros_like(l_s)
    acc_s[...] = jnp.zeros_like(acc_s)

    def scores(t, masked, which=maps):
        kv_rows = pl.ds(pl.multiple_of(t * blk, blk), blk)
        for e in which:
            s = lax.dot_general(q_s[:, e * D:(e + 1) * D], k_s[kv_rows, e * D:(e + 1) * D],
                                (((1,), (1,)), ((), ())), preferred_element_type=F32)
            if masked:
                row = lax.broadcasted_iota(jnp.int32, (blk, blk), 0)
                col = lax.broadcasted_iota(jnp.int32, (blk, blk), 1)
                s = jnp.where(col <= row, s, -jnp.inf)
            m_prev = m_s[e]
            m_new = jnp.maximum(m_prev, jnp.max(s, axis=-1, keepdims=True))
            alpha = jnp.exp2(m_prev - m_new)
            p = jnp.exp2(s - jnp.tile(m_new, (1, lanes)))
            l_s[e] = alpha * l_s[e] + jnp.sum(p, axis=-1, keepdims=True)
            m_s[e] = m_new
            a_s[e] = alpha
            p_s[e] = p.astype(BF16)

    def weighted_values(t, which=maps):
        kv_rows = pl.ds(pl.multiple_of(t * blk, blk), blk)
        for e in which:
            v = v_ref[kv_rows, (e // 2) * 2 * D:(e // 2 + 1) * 2 * D]
            acc_s[e] = jnp.tile(a_s[e], (1, 2)) * acc_s[e] + jnp.dot(
                p_s[e], v, preferred_element_type=F32)

    @pl.when(i == 0)
    def _():
        scores(0, True)

    @pl.when(i > 0)
    def _():
        scores(0, False)

        def pipelined(t, carry):
            for e in maps:
                weighted_values(t - 1, (e,))
                scores(t, False, (e,))
            return carry

        lax.fori_loop(1, i, pipelined, 0)
        weighted_values(i - 1)
        scores(i, True)

    weighted_values(i)

    lp = lp_ref[...].astype(F32)
    lam = (jnp.exp(jnp.sum(lp[0:1] * lp[1:2], axis=-1, keepdims=True))
           - jnp.exp(jnp.sum(lp[2:3] * lp[3:4], axis=-1, keepdims=True)) + lam_init)
    for h in range(heads):
        o = (acc_s[2 * h] * jnp.tile(1.0 / l_s[2 * h], (1, 2))
             - lam * (acc_s[2 * h + 1] * jnp.tile(1.0 / l_s[2 * h + 1], (1, 2))))
        ms = jnp.mean(o * o, axis=-1, keepdims=True)
        o_ref[:, h * 2 * D:(h + 1) * 2 * D] = (
            o * lax.rsqrt(ms + NORM_EPS) * g_ref[...] * (1.0 - lam_init)).astype(o_ref.dtype)


def _flash(z, cos, sin_signed, q_gain, k_gain, lam_params, subln_gain, *, layer, batch, seq, heads,
           col0, blk=512, heads_per_step=2):
    nblk = seq // blk
    D = HEAD_DIM
    hp = heads_per_step
    width = hp * 2 * D
    groups = heads // hp
    lam_init = 0.8 - 0.6 * math.exp(-0.3 * layer)
    cb = col0 // width
    qspec = pl.BlockSpec((blk, width), lambda b, h, i: (b * nblk + i, cb + h))
    kspec = pl.BlockSpec((seq, width), lambda b, h, i: (b, cb + groups + h))
    vspec = pl.BlockSpec((seq, width), lambda b, h, i: (b, cb + 2 * groups + h))
    table = pl.BlockSpec((seq, D), lambda b, h, i: (0, 0))
    vec = pl.BlockSpec((1, D), lambda b, h, i: (0, 0))
    nmap = 2 * hp
    return pl.pallas_call(
        functools.partial(_flash_kernel, blk=blk, seq=seq, heads=hp, lam_init=lam_init),
        grid=(batch, groups, nblk),
        in_specs=[qspec, kspec, vspec, table, table, vec, vec,
                  pl.BlockSpec((4, D), lambda b, h, i: (0, 0)),
                  pl.BlockSpec((1, 2 * D), lambda b, h, i: (0, 0))],
        out_specs=pl.BlockSpec((blk, width), lambda b, h, i: (b * nblk + i, h)),
        out_shape=jax.ShapeDtypeStruct((batch * seq, heads * 2 * D), BF16),
        scratch_shapes=[pltpu.VMEM((blk, width), BF16), pltpu.VMEM((seq, width), BF16),
                        pltpu.VMEM((nmap, blk, D), F32), pltpu.VMEM((nmap, blk, D), F32),
                        pltpu.VMEM((nmap, blk, 2 * D), F32), pltpu.VMEM((nmap, blk, D), F32),
                        pltpu.VMEM((nmap, blk, blk), BF16)],
        compiler_params=_params(("parallel", "parallel", "arbitrary")),
        name="diff_flash",
    )(z, z, z, cos, sin_signed, q_gain.reshape(1, D), k_gain.reshape(1, D), lam_params,
      subln_gain.reshape(1, 2 * D))


def kernel(x, ln1_gain, w_in, conv_w, conv_b, lru_w_r, lru_b_r, lru_w_i, lru_b_i, lru_lambda, lru_norm_gain, hgrn_lower_bounds, hgrn_norm_gain, q_norm_gain, k_norm_gain, diff_lambda, diff_subln_gain, w_out, ln2_gain, w_ff1, w_ff2):
    batch, seq, d_model = x.shape
    depth = w_in.shape[0]
    D = HEAD_DIM
    lru_width = conv_w.shape[-1]
    key_width = hgrn_lower_bounds.shape[-1]
    in_width = w_in.shape[-1]
    mix_width = w_out.shape[1]
    diff_width = in_width - 2 * key_width - 2 * mix_width
    val_width = mix_width - lru_width - diff_width
    assert q_norm_gain.shape[-1] == D and hgrn_norm_gain.shape[-1] == D
    assert lru_w_r.shape[-1] == D and key_width == val_width
    hgrn_heads = key_width // D
    diff_heads = diff_width // (2 * D)
    off_b = 2 * lru_width
    off_c = off_b + 2 * key_width + 2 * val_width

    inv = 1.0 / (ROPE_THETA ** (jnp.arange(0, D, 2, dtype=F32) / D))
    ang = jnp.arange(seq, dtype=F32)[:, None] * inv[None, :]
    ang = jnp.concatenate([ang, ang], axis=-1)
    cos = jnp.cos(ang)
    sign = jnp.concatenate([-jnp.ones((D // 2,), F32), jnp.ones((D // 2,), F32)])
    sin_signed = jnp.sin(ang) * sign[None, :]

    xf = x.reshape(batch * seq, d_model)
    xg, ss = _rmsnorm(xf, ln1_gain[0]), None
    w_in_b = w_in[0].astype(BF16)
    for l in range(depth):
        z, w_out_b = _matmul([xg], w_in_b, row_ss=ss, out_dtype=BF16, casts=[(w_out, l)])

        ya = _lru(z, conv_w[l], conv_b[l], lru_w_r[l], lru_b_r[l], lru_w_i[l], lru_b_i[l],
                  lru_lambda[l], lru_norm_gain[l], batch=batch, seq=seq)
        yb = _hgrn(z, hgrn_lower_bounds, hgrn_norm_gain[l], layer=l, batch=batch, seq=seq,
                   heads=hgrn_heads, col0=off_b)
        yc = _flash(z, cos, sin_signed, q_norm_gain[l], k_norm_gain[l], diff_lambda[l],
                    diff_subln_gain[l], layer=l, batch=batch, seq=seq, heads=diff_heads, col0=off_c)

        xf, xg, ss, w_ff1_b = _matmul([ya, yb, yc], w_out_b, mode="residual", residual=xf,
                                      next_gain=ln2_gain[l], tn=512, casts=[(w_ff1, l)])
        ff1_casts = [(w_ff2, l)] + ([(w_in, l + 1)] if l + 1 < depth else [])
        u, w_ff2_b, *rest = _matmul([xg], w_ff1_b, mode="relu2", row_ss=ss, out_dtype=BF16,
                                    casts=ff1_casts)
        if l + 1 < depth:
            w_in_b = rest[0]
            xf, xg, ss = _matmul([u], w_ff2_b, mode="residual", residual=xf,
                                 next_gain=ln1_gain[l + 1])
        else:
            xf = _matmul([u], w_ff2_b, mode="residual", residual=xf)
    return xf.reshape(batch, seq, d_model)
```

```python
import functools
import math

import jax
import jax.numpy as jnp
from jax import lax
from jax.experimental import pallas as pl
from jax.experimental.pallas import tpu as pltpu

F32 = jnp.float32
BF16 = jnp.bfloat16

NORM_EPS = 1e-6
LRU_C = 8.0
ROPE_THETA = 10000.0
CONV_WIDTH = 4
LOG2_E = 1.4426950408889634
HEAD_DIM = 128
HGRN_CHUNK = 128
SUBLANES = 8
HGRN_SUB = SUBLANES
VMEM_LIMIT = 56 * 1024 * 1024


def _params(semantics):
    return pltpu.CompilerParams(dimension_semantics=semantics, vmem_limit_bytes=VMEM_LIMIT)


def _sigmoid(x):
    return 0.5 * jnp.tanh(0.5 * x) + 0.5


def _rmsnorm_kernel(x_ref, g_ref, o_ref):
    x = x_ref[...]
    ms = jnp.mean(x * x, axis=-1, keepdims=True)
    o_ref[...] = (x * lax.rsqrt(ms + NORM_EPS) * g_ref[...]).astype(o_ref.dtype)


def _rmsnorm(x, gain, tm=256):
    n, d = x.shape
    return pl.pallas_call(
        _rmsnorm_kernel,
        grid=(n // tm,),
        in_specs=[pl.BlockSpec((tm, d), lambda i: (i, 0)),
                  pl.BlockSpec((1, d), lambda i: (0, 0))],
        out_specs=pl.BlockSpec((tm, d), lambda i: (i, 0)),
        out_shape=jax.ShapeDtypeStruct((n, d), BF16),
        compiler_params=_params(("parallel",)),
        name="rmsnorm",
    )(x, gain.reshape(1, d))


LANES = 128


def _row_factor(ss_ref, width, tn):
    rstd = lax.rsqrt(ss_ref[...] * (1.0 / width) + NORM_EPS)
    return jnp.tile(rstd, (1, tn // LANES))


def _finish(acc, j, *, mode, r_ref, g_ref, o_ref, xg_ref, ss_ref):
    if mode == "relu2":
        acc = jnp.square(jnp.maximum(acc, 0.0))
    elif mode == "residual":
        acc = acc + r_ref[...]
    o_ref[...] = acc.astype(o_ref.dtype)
    if xg_ref is not None:
        xg_ref[...] = (acc * g_ref[...]).astype(xg_ref.dtype)

        @pl.when(j == 0)
        def _():
            ss_ref[...] = jnp.zeros_like(ss_ref)

        ss_ref[...] += jnp.sum(acc * acc, axis=-1, keepdims=True)


def _unpack(refs, n_a, row_scale, mode, emit_norm, n_cast):
    it = iter(refs)
    a_refs = [next(it) for _ in range(n_a)]
    w_ref = next(it)
    ss_in = next(it) if row_scale else None
    r_ref = next(it) if mode == "residual" else None
    g_ref = next(it) if emit_norm else None
    cast_src = [next(it) for _ in range(n_cast)]
    o_ref = next(it)
    xg_ref = next(it) if emit_norm else None
    ss_ref = next(it) if emit_norm else None
    for src_ref in cast_src:
        dst_ref = next(it)
        dst_ref[...] = src_ref[...].astype(dst_ref.dtype)
    return a_refs, w_ref, ss_in, r_ref, g_ref, o_ref, xg_ref, ss_ref, list(it)


def _mm_fullk_kernel(*refs, splits, mode, row_scale, emit_norm, n_cast):
    a_refs, w_ref, ss_in, r_ref, g_ref, o_ref, xg_ref, ss_ref, _ = _unpack(
        refs, len(splits), row_scale, mode, emit_norm, n_cast)
    acc, off = None, 0
    for a_ref, kp in zip(a_refs, splits):
        part = jnp.dot(a_ref[...], w_ref[off:off + kp, :], preferred_element_type=F32)
        acc = part if acc is None else acc + part
        off += kp
    if row_scale:
        acc = acc * _row_factor(ss_in, off, o_ref.shape[-1])
    _finish(acc, pl.program_id(1), mode=mode, r_ref=r_ref, g_ref=g_ref, o_ref=o_ref,
            xg_ref=xg_ref, ss_ref=ss_ref)


def _mm_kloop_kernel(*refs, nk, mode, emit_norm):
    a_refs, w_ref, _, r_ref, g_ref, o_ref, xg_ref, ss_ref, (acc_ref,) = _unpack(
        refs, 1, False, mode, emit_norm, 0)
    k = pl.program_id(2)

    @pl.when(k == 0)
    def _():
        acc_ref[...] = jnp.zeros_like(acc_ref)

    acc_ref[...] += jnp.dot(a_refs[0][...], w_ref[...], preferred_element_type=F32)

    @pl.when(k == nk - 1)
    def _():
        _finish(acc_ref[...], pl.program_id(1), mode=mode, r_ref=r_ref, g_ref=g_ref, o_ref=o_ref,
                xg_ref=xg_ref, ss_ref=ss_ref)


def _matmul(a_parts, w, *, mode="plain", residual=None, row_ss=None, next_gain=None, casts=(),
            out_dtype=F32, tm=1024, tn=1024, tk=2048):
    m = a_parts[0].shape[0]
    splits = tuple(p.shape[1] for p in a_parts)
    kdim = sum(splits)
    n = w.shape[-1]
    fullk = kdim <= 4096
    row_scale = row_ss is not None
    emit_norm = next_gain is not None
    tm, tn = min(tm, m), min(tn, n)

    if fullk:
        grid = (m // tm, n // tn)
        tile = lambda i, j: (i, j)
        rows = lambda i, j: (i, 0)
        cols = lambda i, j: (0, j)
        in_specs = [pl.BlockSpec((tm, kp), rows) for kp in splits]
        in_specs.append(pl.BlockSpec((kdim, tn), cols))
        body = functools.partial(_mm_fullk_kernel, splits=splits, mode=mode, row_scale=row_scale,
                                 emit_norm=emit_norm, n_cast=len(casts))
        scratch = []
        semantics = ("parallel", "arbitrary")
    else:
        assert len(a_parts) == 1 and not row_scale and not casts
        tk = min(tk, kdim)
        nk = kdim // tk
        grid = (m // tm, n // tn, nk)
        tile = lambda i, j, k: (i, j)
        rows = lambda i, j, k: (i, 0)
        cols = lambda i, j, k: (0, j)
        in_specs = [pl.BlockSpec((tm, tk), lambda i, j, k: (i, k)),
                    pl.BlockSpec((tk, tn), lambda i, j, k: (k, j))]
        body = functools.partial(_mm_kloop_kernel, nk=nk, mode=mode, emit_norm=emit_norm)
        scratch = [pltpu.VMEM((tm, tn), F32)]
        semantics = ("parallel", "arbitrary", "arbitrary")

    args = list(a_parts) + [w]
    if row_scale:
        in_specs.append(pl.BlockSpec((tm, LANES), rows))
        args.append(row_ss)
    if mode == "residual":
        in_specs.append(pl.BlockSpec((tm, tn), tile))
        args.append(residual)
    out_specs = [pl.BlockSpec((tm, tn), tile)]
    out_shape = [jax.ShapeDtypeStruct((m, n), out_dtype)]
    if emit_norm:
        in_specs.append(pl.BlockSpec((1, tn), cols))
        args.append(next_gain.reshape(1, n))
        out_specs += [pl.BlockSpec((tm, tn), tile), pl.BlockSpec((tm, LANES), rows)]
        out_shape += [jax.ShapeDtypeStruct((m, n), BF16), jax.ShapeDtypeStruct((m, LANES), F32)]
    for w_f32, layer in casts:
        _, wr, wc = w_f32.shape
        col_blocks = max(c for c in range(1, grid[1] + 1)
                         if grid[1] % c == 0 and wc % (c * LANES) == 0)
        br, bc = wr // grid[0], wc // col_blocks
        assert wr % grid[0] == 0 and br % 8 == 0
        col = lambda j, last=col_blocks - 1: jnp.minimum(j, last)
        in_specs.append(pl.BlockSpec((None, br, bc),
                                     lambda i, j, layer=layer, col=col: (layer, i, col(j))))
        args.append(w_f32)
        out_specs.append(pl.BlockSpec((br, bc), lambda i, j, col=col: (i, col(j))))
        out_shape.append(jax.ShapeDtypeStruct((wr, wc), BF16))
    outs = pl.pallas_call(
        body, grid=grid, in_specs=in_specs, out_specs=out_specs, out_shape=out_shape,
        scratch_shapes=scratch, compiler_params=_params(semantics),
        name=("matmul_fullk_" if fullk else "matmul_kloop_") + mode,
    )(*args)
    return outs if len(outs) > 1 else outs[0]


def _tile_roll(x, shift):
    rows, d = x.shape
    x3 = x.reshape(rows // SUBLANES, SUBLANES, d)
    return pltpu.roll(x3, shift, 1).reshape(rows, d)


def _lru_kernel(zg_ref, zx_ref, cw_ref, cb_ref, wr_ref, br_ref, wi_ref, bi_ref, lam_ref, ng_ref,
                o_ref, xprev_ref, hprev_ref, a_s, u_s, *, rows, heads):
    s = pl.program_id(1)
    width = heads * HEAD_DIM
    tiles = rows // SUBLANES

    @pl.when(s == 0)
    def _():
        xprev_ref[...] = jnp.zeros_like(xprev_ref)
        hprev_ref[...] = jnp.zeros_like(hprev_ref)

    assert zx_ref.dtype == BF16
    xzb = zx_ref[...]
    xz = xzb.astype(F32)
    prev8 = xprev_ref[...]
    row8 = lax.broadcasted_iota(jnp.int32, (SUBLANES, width), 0)
    lag = (lax.broadcasted_iota(jnp.int32, (rows, rows), 0)
           - lax.broadcasted_iota(jnp.int32, (rows, rows), 1))
    cw = cw_ref[...]
    xa = cw[CONV_WIDTH - 1:CONV_WIDTH] * xz + cb_ref[...]
    for d in range(1, CONV_WIDTH):
        moved = jnp.dot((lag == d).astype(BF16), xzb, preferred_element_type=F32)
        head = jnp.where(row8 < d, pltpu.roll(prev8, d, 0), moved[0:SUBLANES])
        shifted = jnp.concatenate([head, moved[SUBLANES:]], axis=0)
        xa = xa + cw[CONV_WIDTH - 1 - d:CONV_WIDTH - d] * shifted
    xprev_ref[...] = xz[rows - SUBLANES:rows]

    xab = xa.astype(BF16)
    r_parts, i_parts = [], []
    for h in range(heads):
        xh = xab[:, h * HEAD_DIM:(h + 1) * HEAD_DIM]
        r_parts.append(jnp.dot(xh, wr_ref[h], preferred_element_type=F32))
        i_parts.append(jnp.dot(xh, wi_ref[h], preferred_element_type=F32))
    r = _sigmoid(jnp.concatenate(r_parts, axis=1) + br_ref[...])
    gate_i = _sigmoid(jnp.concatenate(i_parts, axis=1) + bi_ref[...])

    lam = lam_ref[...]
    softplus_neg_lam = jnp.maximum(-lam, 0.0) + jnp.log1p(jnp.exp(-jnp.abs(lam)))
    a = jnp.exp((-LRU_C) * r * softplus_neg_lam)
    gap = 1.0 - a * a
    mult = gap * lax.rsqrt(jnp.maximum(gap, 1e-30))
    row = lax.broadcasted_iota(jnp.int32, (rows, width), 0)
    mult = jnp.where(jnp.logical_and(row == 0, s == 0), 1.0, mult)
    u = mult * (gate_i * xa)

    sub = jnp.bitwise_and(row, SUBLANES - 1)
    d = 1
    while d < SUBLANES:
        a_sh = _tile_roll(a, d)
        u_sh = _tile_roll(u, d)
        valid = sub >= d
        u = jnp.where(valid, u + a * u_sh, u)
        a = jnp.where(valid, a * a_sh, a)
        d *= 2
    a_s[...] = a
    u_s[...] = u
    carry = hprev_ref[0:1, :]
    for t in range(tiles):
        grp = slice(t * SUBLANES, (t + 1) * SUBLANES)
        a_g, u_g = a_s[grp, :], u_s[grp, :]
        u_s[grp, :] = u_g + a_g * carry
        carry = u_g[SUBLANES - 1:SUBLANES] + a_g[SUBLANES - 1:SUBLANES] * carry
    hprev_ref[...] = jnp.broadcast_to(carry, hprev_ref.shape)
    h = u_s[...]

    zg = zg_ref[...].astype(F32)
    gelu = 0.5 * zg * (1.0 + jnp.tanh(math.sqrt(2.0 / math.pi) * (zg + 0.044715 * (zg * zg * zg))))
    y = h * gelu
    ms = jnp.mean(y * y, axis=-1, keepdims=True)
    o_ref[...] = (y * lax.rsqrt(ms + NORM_EPS) * ng_ref[...]).astype(o_ref.dtype)


def _lru(z, conv_w, conv_b, w_r, b_r, w_i, b_i, lam, norm_gain, *, batch, seq, rows=256):
    heads = w_r.shape[0]
    width = heads * HEAD_DIM
    nblk = seq // rows
    row_spec = lambda c: pl.BlockSpec((rows, width), lambda b, s, c=c: (b * nblk + s, c))
    vec = pl.BlockSpec((1, width), lambda b, s: (0, 0))
    wspec = pl.BlockSpec((heads, HEAD_DIM, HEAD_DIM), lambda b, s: (0, 0, 0))
    return pl.pallas_call(
        functools.partial(_lru_kernel, rows=rows, heads=heads),
        grid=(batch, nblk),
        in_specs=[row_spec(0), row_spec(1),
                  pl.BlockSpec((CONV_WIDTH, width), lambda b, s: (0, 0)), vec,
                  wspec, vec, wspec, vec, vec, vec],
        out_specs=pl.BlockSpec((rows, width), lambda b, s: (b * nblk + s, 0)),
        out_shape=jax.ShapeDtypeStruct((batch * seq, width), BF16),
        scratch_shapes=[pltpu.VMEM((SUBLANES, width), F32), pltpu.VMEM((SUBLANES, width), F32),
                        pltpu.VMEM((rows, width), F32), pltpu.VMEM((rows, width), F32)],
        compiler_params=_params(("parallel", "arbitrary")),
        name="rg_lru",
    )(z, z, conv_w, conv_b.reshape(1, width), w_r.astype(BF16), b_r.reshape(1, width),
      w_i.astype(BF16), b_i.reshape(1, width), lam.reshape(1, width), norm_gain.reshape(1, width))


def _split3(x):
    hi = x.astype(BF16)
    r1 = x - hi.astype(F32)
    mid = r1.astype(BF16)
    lo = (r1 - mid.astype(F32)).astype(BF16)
    return hi, mid, lo


def _hgrn_kernel(zq_ref, zf_ref, zv_ref, zg_ref, lbp_ref, ng_ref, o_ref, st_s, *, rows, layer,
                 heads):
    s = pl.program_id(2)
    C, c = HGRN_CHUNK, HGRN_SUB
    D = HEAD_DIM

    @pl.when(s == 0)
    def _():
        st_s[...] = jnp.zeros_like(st_s)

    lbp = lbp_ref[...].astype(F32)
    e = jnp.exp(lbp - jnp.max(lbp, axis=0, keepdims=True))
    sm = e / jnp.sum(e, axis=0, keepdims=True)
    lb_all = jnp.zeros((1, heads * D), F32)
    for r_ in range(1, layer + 1):
        lb_all = lb_all + sm[r_:r_ + 1]

    row = lax.broadcasted_iota(jnp.int32, (C, C), 0)
    col = lax.broadcasted_iota(jnp.int32, (C, C), 1)
    tri = (row >= col).astype(BF16)
    differ = jnp.bitwise_xor(row, col)
    levels = []
    m = C // 2
    while m >= c:
        levels.append((m, (row > col) & (differ >= m) & (differ < 2 * m)))
        m //= 2
    diag_masks = [(differ < c) & (row - col == dlt) for dlt in range(c)]

    for ch, h in [(ch, h) for ch in range(rows // C) for h in range(heads)]:
        sl = slice(ch * C, (ch + 1) * C)
        hc = slice(h * D, (h + 1) * D)
        lb = lb_all[:, hc]
        zq = zq_ref[sl, hc].astype(F32)
        zf = zf_ref[sl, hc].astype(F32)
        q = zq * _sigmoid(zq) * (D ** -0.5)
        half_tanh = 0.5 * jnp.tanh(0.5 * zf)
        f = lb + (1.0 - lb) * (0.5 + half_tanh)
        k = (1.0 - lb) * (0.5 - half_tanh)
        log_sig = jnp.minimum(zf, 0.0) - jnp.log(1.0 + jnp.exp(-jnp.abs(zf)))
        y = jnp.log1p(-lb) + log_sig
        if layer == 0:
            log_f = y
        else:
            la = jnp.log(lb)
            log_f = jnp.maximum(la, y) + jnp.log(1.0 + jnp.exp(-jnp.abs(la - y)))

        hi, mid, lo = _split3(log_f)
        g3 = jnp.dot(tri, jnp.concatenate([hi, mid, lo], axis=1), preferred_element_type=F32)
        G = g3[:, 0:D] + g3[:, D:2 * D] + g3[:, 2 * D:3 * D]
        vb = zv_ref[sl, hc].astype(BF16)
        g_last = G[C - 1:C]

        a = jnp.zeros((C, C), F32)
        dec = None
        for dlt in range(c):
            if dlt == 0:
                w = q * k
            else:
                f_sh = f if dlt == 1 else _tile_roll(f, dlt - 1)
                dec = f_sh if dlt == 1 else dec * f_sh
                w = q * _tile_roll(k, dlt) * dec
            a = jnp.where(diag_masks[dlt], jnp.sum(w, axis=-1, keepdims=True), a)

        for m, mask in levels:
            g_bnd = jnp.concatenate(
                [jnp.broadcast_to(G[p * 2 * m + m - 1:p * 2 * m + m], (2 * m, D))
                 for p in range(C // (2 * m))], axis=0)
            e_ref = jnp.exp(-jnp.abs(G - g_bnd))
            sc = lax.dot_general((q * e_ref).astype(BF16), (k * e_ref).astype(BF16),
                                 (((1,), (1,)), ((), ())), preferred_element_type=F32)
            a = jnp.where(mask, sc, a)

        st = st_s[h]
        qg = (q * jnp.exp(G)).astype(BF16)
        o = (jnp.dot(a.astype(BF16), vb, preferred_element_type=F32)
             + lax.dot_general(qg, st.astype(BF16), (((1,), (1,)), ((), ())),
                               preferred_element_type=F32))
        kd = (k * jnp.exp(g_last - G)).astype(BF16)
        st_s[h] = st * jnp.exp(g_last) + lax.dot_general(
            vb, kd, (((0,), (0,)), ((), ())), preferred_element_type=F32)

        ms = jnp.mean(o * o, axis=-1, keepdims=True)
        zg = zg_ref[sl, hc].astype(F32)
        o_ref[sl, hc] = (o * lax.rsqrt(ms + NORM_EPS) * ng_ref[...]
                         * (zg * _sigmoid(zg))).astype(o_ref.dtype)


def _hgrn(z, lower_bounds, norm_gain, *, layer, batch, seq, heads, col0, rows=512, heads_per_step=2):
    nblk = seq // rows
    D = HEAD_DIM
    depth = lower_bounds.shape[0]
    hp = heads_per_step
    width = hp * D
    groups = heads // hp

    def zspec(section):
        off = col0 // width + section * groups
        return pl.BlockSpec((rows, width), lambda b, h, s, off=off: (b * nblk + s, off + h))

    return pl.pallas_call(
        functools.partial(_hgrn_kernel, rows=rows, layer=layer, heads=hp),
        grid=(batch, groups, nblk),
        in_specs=[zspec(0), zspec(1), zspec(2), zspec(3),
                  pl.BlockSpec((depth, width), lambda b, h, s: (0, h)),
                  pl.BlockSpec((1, D), lambda b, h, s: (0, 0))],
        out_specs=pl.BlockSpec((rows, width), lambda b, h, s: (b * nblk + s, h)),
        out_shape=jax.ShapeDtypeStruct((batch * seq, heads * D), BF16),
        scratch_shapes=[pltpu.VMEM((hp, D, D), F32)],
        compiler_params=_params(("parallel", "parallel", "arbitrary")),
        name="hgrn2",
    )(z, z, z, z, lower_bounds, norm_gain.reshape(1, D))


def _norm_rope(x, gain, cos, sin_signed, scale):
    sq = x * x
    hi = sq.astype(BF16)
    lo = (sq - hi.astype(F32)).astype(BF16)
    ones = jnp.ones((2 * HEAD_DIM, HEAD_DIM), BF16)
    ms = jnp.dot(jnp.concatenate([hi, lo], axis=1), ones,
                 preferred_element_type=F32) * (1.0 / HEAD_DIM)
    xn = x * lax.rsqrt(ms + NORM_EPS) * gain
    return (xn * cos + pltpu.roll(xn, HEAD_DIM // 2, 1) * sin_signed) * scale


def _flash_kernel(zq_ref, zk_ref, v_ref, cos_ref, sin_ref, qg_ref, kg_ref, lp_ref, g_ref, o_ref,
                  q_s, k_s, m_s, l_s, acc_s, a_s, p_s, *, blk, seq, heads, lam_init):
    i = pl.program_id(2)
    D = HEAD_DIM
    lanes = blk // D
    maps = tuple(range(2 * heads))

    @pl.when(i == 0)
    def _():
        def prep_keys(r, carry):
            rows = pl.ds(pl.multiple_of(r * blk, blk), blk)
            for e in maps:
                k_s[rows, e * D:(e + 1) * D] = _norm_rope(
                    zk_ref[rows, e * D:(e + 1) * D].astype(F32), kg_ref[...],
                    cos_ref[rows, :], sin_ref[rows, :], 1.0).astype(k_s.dtype)
            return carry

        lax.fori_loop(0, seq // blk, prep_keys, 0)

    q_rows = pl.ds(pl.multiple_of(i * blk, blk), blk)
    for e in maps:
        q_s[:, e * D:(e + 1) * D] = _norm_rope(
            zq_ref[:, e * D:(e + 1) * D].astype(F32), qg_ref[...],
            cos_ref[q_rows, :], sin_ref[q_rows, :], D ** -0.5 * LOG2_E).astype(q_s.dtype)

    m_s[...] = jnp.full_like(m_s, -jnp.inf)
    l_s[...] = jnp.zeros_like(l_s)
    acc_s[...] = jnp.zeros_like(acc_s)

    def scores(t, masked, which=maps):
        kv_rows = pl.ds(pl.multiple_of(t * blk, blk), blk)
        for e in which:
            s = lax.dot_general(q_s[:, e * D:(e + 1) * D], k_s[kv_rows, e * D:(e + 1) * D],
                                (((1,), (1,)), ((), ())), preferred_element_type=F32)
            if masked:
                row = lax.broadcasted_iota(jnp.int32, (blk, blk), 0)
                col = lax.broadcasted_iota(jnp.int32, (blk, blk), 1)
                s = jnp.where(col <= row, s, -jnp.inf)
            m_prev = m_s[e]
            m_new = jnp.maximum(m_prev, jnp.max(s, axis=-1, keepdims=True))
            alpha = jnp.exp2(m_prev - m_new)
            p = jnp.exp2(s - jnp.tile(m_new, (1, lanes)))
            l_s[e] = alpha * l_s[e] + jnp.sum(p, axis=-1, keepdims=True)
            m_s[e] = m_new
            a_s[e] = alpha
            p_s[e] = p.astype(BF16)

    def weighted_values(t, which=maps):
        kv_rows = pl.ds(pl.multiple_of(t * blk, blk), blk)
        for e in which:
            v = v_ref[kv_rows, (e // 2) * 2 * D:(e // 2 + 1) * 2 * D]
            acc_s[e] = jnp.tile(a_s[e], (1, 2)) * acc_s[e] + jnp.dot(
                p_s[e], v, preferred_element_type=F32)

    @pl.when(i == 0)
    def _():
        scores(0, True)

    @pl.when(i > 0)
    def _():
        scores(0, False)

        def pipelined(t, carry):
            for e in maps:
                weighted_values(t - 1, (e,))
                scores(t, False, (e,))
            return carry

        lax.fori_loop(1, i, pipelined, 0)
        weighted_values(i - 1)
        scores(i, True)

    weighted_values(i)

    lp = lp_ref[...].astype(F32)
    lam = (jnp.exp(jnp.sum(lp[0:1] * lp[1:2], axis=-1, keepdims=True))
           - jnp.exp(jnp.sum(lp[2:3] * lp[3:4], axis=-1, keepdims=True)) + lam_init)
    for h in range(heads):
        o = (acc_s[2 * h] * jnp.tile(1.0 / l_s[2 * h], (1, 2))
             - lam * (acc_s[2 * h + 1] * jnp.tile(1.0 / l_s[2 * h + 1], (1, 2))))
        ms = jnp.mean(o * o, axis=-1, keepdims=True)
        o_ref[:, h * 2 * D:(h + 1) * 2 * D] = (
            o * lax.rsqrt(ms + NORM_EPS) * g_ref[...] * (1.0 - lam_init)).astype(o_ref.dtype)


def _flash(z, cos, sin_signed, q_gain, k_gain, lam_params, subln_gain, *, layer, batch, seq, heads,
           col0, blk=512, heads_per_step=2):
    nblk = seq // blk
    D = HEAD_DIM
    hp = heads_per_step
    width = hp * 2 * D
    groups = heads // hp
    lam_init = 0.8 - 0.6 * math.exp(-0.3 * layer)
    cb = col0 // width
    qspec = pl.BlockSpec((blk, width), lambda b, h, i: (b * nblk + i, cb + h))
    kspec = pl.BlockSpec((seq, width), lambda b, h, i: (b, cb + groups + h))
    vspec = pl.BlockSpec((seq, width), lambda b, h, i: (b, cb + 2 * groups + h))
    table = pl.BlockSpec((seq, D), lambda b, h, i: (0, 0))
    vec = pl.BlockSpec((1, D), lambda b, h, i: (0, 0))
    nmap = 2 * hp
    return pl.pallas_call(
        functools.partial(_flash_kernel, blk=blk, seq=seq, heads=hp, lam_init=lam_init),
        grid=(batch, groups, nblk),
        in_specs=[qspec, kspec, vspec, table, table, vec, vec,
                  pl.BlockSpec((4, D), lambda b, h, i: (0, 0)),
                  pl.BlockSpec((1, 2 * D), lambda b, h, i: (0, 0))],
        out_specs=pl.BlockSpec((blk, width), lambda b, h, i: (b * nblk + i, h)),
        out_shape=jax.ShapeDtypeStruct((batch * seq, heads * 2 * D), BF16),
        scratch_shapes=[pltpu.VMEM((blk, width), BF16), pltpu.VMEM((seq, width), BF16),
                        pltpu.VMEM((nmap, blk, D), F32), pltpu.VMEM((nmap, blk, D), F32),
                        pltpu.VMEM((nmap, blk, 2 * D), F32), pltpu.VMEM((nmap, blk, D), F32),
                        pltpu.VMEM((nmap, blk, blk), BF16)],
        compiler_params=_params(("parallel", "parallel", "arbitrary")),
        name="diff_flash",
    )(z, z, z, cos, sin_signed, q_gain.reshape(1, D), k_gain.reshape(1, D), lam_params,
      subln_gain.reshape(1, 2 * D))


def kernel(x, ln1_gain, w_in, conv_w, conv_b, lru_w_r, lru_b_r, lru_w_i, lru_b_i, lru_lambda, lru_norm_gain, hgrn_lower_bounds, hgrn_norm_gain, q_norm_gain, k_norm_gain, diff_lambda, diff_subln_gain, w_out, ln2_gain, w_ff1, w_ff2):
    batch, seq, d_model = x.shape
    depth = w_in.shape[0]
    D = HEAD_DIM
    lru_width = conv_w.shape[-1]
    key_width = hgrn_lower_bounds.shape[-1]
    in_width = w_in.shape[-1]
    mix_width = w_out.shape[1]
    diff_width = in_width - 2 * key_width - 2 * mix_width
    val_width = mix_width - lru_width - diff_width
    assert q_norm_gain.shape[-1] == D and hgrn_norm_gain.shape[-1] == D
    assert lru_w_r.shape[-1] == D and key_width == val_width
    hgrn_heads = key_width // D
    diff_heads = diff_width // (2 * D)
    off_b = 2 * lru_width
    off_c = off_b + 2 * key_width + 2 * val_width

    inv = 1.0 / (ROPE_THETA ** (jnp.arange(0, D, 2, dtype=F32) / D))
    ang = jnp.arange(seq, dtype=F32)[:, None] * inv[None, :]
    ang = jnp.concatenate([ang, ang], axis=-1)
    cos = jnp.cos(ang)
    sign = jnp.concatenate([-jnp.ones((D // 2,), F32), jnp.ones((D // 2,), F32)])
    sin_signed = jnp.sin(ang) * sign[None, :]

    xf = x.reshape(batch * seq, d_model)
    xg, ss = _rmsnorm(xf, ln1_gain[0]), None
    w_in_b = w_in[0].astype(BF16)
    for l in range(depth):
        z, w_out_b = _matmul([xg], w_in_b, row_ss=ss, out_dtype=BF16, casts=[(w_out, l)])

        ya = _lru(z, conv_w[l], conv_b[l], lru_w_r[l], lru_b_r[l], lru_w_i[l], lru_b_i[l],
                  lru_lambda[l], lru_norm_gain[l], batch=batch, seq=seq)
        yb = _hgrn(z, hgrn_lower_bounds, hgrn_norm_gain[l], layer=l, batch=batch, seq=seq,
                   heads=hgrn_heads, col0=off_b)
        yc = _flash(z, cos, sin_signed, q_norm_gain[l], k_norm_gain[l], diff_lambda[l],
                    diff_subln_gain[l], layer=l, batch=batch, seq=seq, heads=diff_heads, col0=off_c)

        xf, xg, ss, w_ff1_b = _matmul([ya, yb, yc], w_out_b, mode="residual", residual=xf,
                                      next_gain=ln2_gain[l], tn=512, casts=[(w_ff1, l)])
        ff1_casts = [(w_ff2, l)] + ([(w_in, l + 1)] if l + 1 < depth else [])
        u, w_ff2_b, *rest = _matmul([xg], w_ff1_b, mode="relu2", row_ss=ss, out_dtype=BF16,
                                    casts=ff1_casts)
        if l + 1 < depth:
            w_in_b = rest[0]
            xf, xg, ss = _matmul([u], w_ff2_b, mode="residual", residual=xf,
                                 next_gain=ln1_gain[l + 1])
        else:
            xf = _matmul([u], w_ff2_b, mode="residual", residual=xf)
    return xf.reshape(batch, seq, d_model)
```

```python
import functools
import math

import jax
import jax.numpy as jnp
from jax import lax
from jax.experimental import pallas as pl
from jax.experimental.pallas import tpu as pltpu

F32 = jnp.float32
BF16 = jnp.bfloat16

NORM_EPS = 1e-6
LRU_C = 8.0
ROPE_THETA = 10000.0
CONV_WIDTH = 4
LOG2_E = 1.4426950408889634
HEAD_DIM = 128
HGRN_CHUNK = 128
SUBLANES = 8
HGRN_SUB = SUBLANES
VMEM_LIMIT = 56 * 1024 * 1024
VMEM_LIMIT_WIDE = 62 * 1024 * 1024


def _params(semantics, vmem_limit=VMEM_LIMIT):
    return pltpu.CompilerParams(dimension_semantics=semantics, vmem_limit_bytes=vmem_limit)


def _sigmoid(x):
    return 0.5 * jnp.tanh(0.5 * x) + 0.5


def _rmsnorm_kernel(x_ref, g_ref, o_ref):
    x = x_ref[...]
    ms = jnp.mean(x * x, axis=-1, keepdims=True)
    o_ref[...] = (x * lax.rsqrt(ms + NORM_EPS) * g_ref[...]).astype(o_ref.dtype)


def _rmsnorm(x, gain, tm=256):
    n, d = x.shape
    return pl.pallas_call(
        _rmsnorm_kernel,
        grid=(n // tm,),
        in_specs=[pl.BlockSpec((tm, d), lambda i: (i, 0)),
                  pl.BlockSpec((1, d), lambda i: (0, 0))],
        out_specs=pl.BlockSpec((tm, d), lambda i: (i, 0)),
        out_shape=jax.ShapeDtypeStruct((n, d), BF16),
        compiler_params=_params(("parallel",)),
        name="rmsnorm",
    )(x, gain.reshape(1, d))


LANES = 128


def _row_factor(ss_ref, width, tn):
    rstd = lax.rsqrt(ss_ref[...] * (1.0 / width) + NORM_EPS)
    return jnp.tile(rstd, (1, tn // LANES))


def _finish(acc, j, *, mode, r_ref, g_ref, o_ref, xg_ref, ss_ref):
    if mode == "relu2":
        acc = jnp.square(jnp.maximum(acc, 0.0))
    elif mode == "residual":
        acc = acc + r_ref[...]
    o_ref[...] = acc.astype(o_ref.dtype)
    if xg_ref is not None:
        xg_ref[...] = (acc * g_ref[...]).astype(xg_ref.dtype)

        @pl.when(j == 0)
        def _():
            ss_ref[...] = jnp.zeros_like(ss_ref)

        ss_ref[...] += jnp.sum(acc * acc, axis=-1, keepdims=True)


def _unpack(refs, n_a, row_scale, mode, emit_norm, n_cast):
    it = iter(refs)
    a_refs = [next(it) for _ in range(n_a)]
    w_ref = next(it)
    ss_in = next(it) if row_scale else None
    r_ref = next(it) if mode == "residual" else None
    g_ref = next(it) if emit_norm else None
    cast_src = [next(it) for _ in range(n_cast)]
    o_ref = next(it)
    xg_ref = next(it) if emit_norm else None
    ss_ref = next(it) if emit_norm else None
    for src_ref in cast_src:
        dst_ref = next(it)
        dst_ref[...] = src_ref[...].astype(dst_ref.dtype)
    return a_refs, w_ref, ss_in, r_ref, g_ref, o_ref, xg_ref, ss_ref, list(it)


def _mm_fullk_kernel(*refs, splits, mode, row_scale, emit_norm, n_cast):
    a_refs, w_ref, ss_in, r_ref, g_ref, o_ref, xg_ref, ss_ref, _ = _unpack(
        refs, len(splits), row_scale, mode, emit_norm, n_cast)
    acc, off = None, 0
    for a_ref, kp in zip(a_refs, splits):
        part = jnp.dot(a_ref[...], w_ref[off:off + kp, :], preferred_element_type=F32)
        acc = part if acc is None else acc + part
        off += kp
    if row_scale:
        acc = acc * _row_factor(ss_in, off, o_ref.shape[-1])
    _finish(acc, pl.program_id(1), mode=mode, r_ref=r_ref, g_ref=g_ref, o_ref=o_ref,
            xg_ref=xg_ref, ss_ref=ss_ref)


def _mm_kloop_kernel(*refs, nk, mode, emit_norm):
    a_refs, w_ref, _, r_ref, g_ref, o_ref, xg_ref, ss_ref, (acc_ref,) = _unpack(
        refs, 1, False, mode, emit_norm, 0)
    k = pl.program_id(2)

    @pl.when(k == 0)
    def _():
        acc_ref[...] = jnp.zeros_like(acc_ref)

    acc_ref[...] += jnp.dot(a_refs[0][...], w_ref[...], preferred_element_type=F32)

    @pl.when(k == nk - 1)
    def _():
        _finish(acc_ref[...], pl.program_id(1), mode=mode, r_ref=r_ref, g_ref=g_ref, o_ref=o_ref,
                xg_ref=xg_ref, ss_ref=ss_ref)


def _matmul(a_parts, w, *, mode="plain", residual=None, row_ss=None, next_gain=None, casts=(),
            out_dtype=F32, tm=1024, tn=1024, tk=2048, vmem_limit=VMEM_LIMIT):
    m = a_parts[0].shape[0]
    splits = tuple(p.shape[1] for p in a_parts)
    kdim = sum(splits)
    n = w.shape[-1]
    fullk = kdim <= 4096
    row_scale = row_ss is not None
    emit_norm = next_gain is not None
    tm, tn = min(tm, m), min(tn, n)

    if fullk:
        grid = (m // tm, n // tn)
        tile = lambda i, j: (i, j)
        rows = lambda i, j: (i, 0)
        cols = lambda i, j: (0, j)
        in_specs = [pl.BlockSpec((tm, kp), rows) for kp in splits]
        in_specs.append(pl.BlockSpec((kdim, tn), cols))
        body = functools.partial(_mm_fullk_kernel, splits=splits, mode=mode, row_scale=row_scale,
                                 emit_norm=emit_norm, n_cast=len(casts))
        scratch = []
        semantics = ("parallel", "arbitrary")
    else:
        assert len(a_parts) == 1 and not row_scale and not casts
        tk = min(tk, kdim)
        nk = kdim // tk
        grid = (m // tm, n // tn, nk)
        tile = lambda i, j, k: (i, j)
        rows = lambda i, j, k: (i, 0)
        cols = lambda i, j, k: (0, j)
        in_specs = [pl.BlockSpec((tm, tk), lambda i, j, k: (i, k)),
                    pl.BlockSpec((tk, tn), lambda i, j, k: (k, j))]
        body = functools.partial(_mm_kloop_kernel, nk=nk, mode=mode, emit_norm=emit_norm)
        scratch = [pltpu.VMEM((tm, tn), F32)]
        semantics = ("parallel", "arbitrary", "arbitrary")

    args = list(a_parts) + [w]
    if row_scale:
        in_specs.append(pl.BlockSpec((tm, LANES), rows))
        args.append(row_ss)
    if mode == "residual":
        in_specs.append(pl.BlockSpec((tm, tn), tile))
        args.append(residual)
    out_specs = [pl.BlockSpec((tm, tn), tile)]
    out_shape = [jax.ShapeDtypeStruct((m, n), out_dtype)]
    if emit_norm:
        in_specs.append(pl.BlockSpec((1, tn), cols))
        args.append(next_gain.reshape(1, n))
        out_specs += [pl.BlockSpec((tm, tn), tile), pl.BlockSpec((tm, LANES), rows)]
        out_shape += [jax.ShapeDtypeStruct((m, n), BF16), jax.ShapeDtypeStruct((m, LANES), F32)]
    for w_f32, layer in casts:
        _, wr, wc = w_f32.shape
        col_blocks = max(c for c in range(1, grid[1] + 1)
                         if grid[1] % c == 0 and wc % (c * LANES) == 0)
        br, bc = wr // grid[0], wc // col_blocks
        assert wr % grid[0] == 0 and br % 8 == 0
        col = lambda j, last=col_blocks - 1: jnp.minimum(j, last)
        in_specs.append(pl.BlockSpec((None, br, bc),
                                     lambda i, j, layer=layer, col=col: (layer, i, col(j))))
        args.append(w_f32)
        out_specs.append(pl.BlockSpec((br, bc), lambda i, j, col=col: (i, col(j))))
        out_shape.append(jax.ShapeDtypeStruct((wr, wc), BF16))
    outs = pl.pallas_call(
        body, grid=grid, in_specs=in_specs, out_specs=out_specs, out_shape=out_shape,
        scratch_shapes=scratch, compiler_params=_params(semantics, vmem_limit),
        name=("matmul_fullk_" if fullk else "matmul_kloop_") + mode,
    )(*args)
    return outs if len(outs) > 1 else outs[0]


def _cast_job(w_f32, layer, grid):
    _, wr, wc = w_f32.shape
    steps = math.prod(grid)
    br = wr // steps
    assert wr % steps == 0 and br % 16 == 0

    def step(*idx):
        flat = idx[0]
        for n, i in zip(grid[1:], idx[1:]):
            flat = flat * n + i
        return flat

    return (pl.BlockSpec((None, br, wc), lambda *idx: (layer, step(*idx), 0)),
            pl.BlockSpec((br, wc), lambda *idx: (step(*idx), 0)),
            jax.ShapeDtypeStruct((wr, wc), BF16))


def _hosting_cast(body, n_in, n_out):
    def hosted(*refs):
        ins, src = refs[:n_in], refs[n_in]
        outs, dst = refs[n_in + 1:n_in + 1 + n_out], refs[n_in + 1 + n_out]
        dst[...] = src[...].astype(dst.dtype)
        body(*ins, *outs, *refs[n_in + 2 + n_out:])
    return hosted


def _call_with_cast(body, cast, *, grid, in_specs, out_specs, out_shape, args, **kwargs):
    if cast is None:
        return pl.pallas_call(body, grid=grid, in_specs=in_specs, out_specs=out_specs,
                              out_shape=out_shape, **kwargs)(*args)
    c_in, c_out, c_shape = _cast_job(cast[0], cast[1], grid)
    return pl.pallas_call(
        _hosting_cast(body, len(in_specs), 1), grid=grid, in_specs=in_specs + [c_in],
        out_specs=[out_specs, c_out], out_shape=[out_shape, c_shape], **kwargs)(*args, cast[0])


def _tile_roll(x, shift):
    rows, d = x.shape
    x3 = x.reshape(rows // SUBLANES, SUBLANES, d)
    return pltpu.roll(x3, shift, 1).reshape(rows, d)


def _lru_kernel(zg_ref, zx_ref, cw_ref, cb_ref, wr_ref, br_ref, wi_ref, bi_ref, lam_ref, ng_ref,
                o_ref, xprev_ref, hprev_ref, a_s, u_s, *, rows, heads):
    s = pl.program_id(1)
    width = heads * HEAD_DIM
    tiles = rows // SUBLANES

    @pl.when(s == 0)
    def _():
        xprev_ref[...] = jnp.zeros_like(xprev_ref)
        hprev_ref[...] = jnp.zeros_like(hprev_ref)

    assert zx_ref.dtype == BF16
    xzb = zx_ref[...]
    xz = xzb.astype(F32)
    prev8 = xprev_ref[...]
    row8 = lax.broadcasted_iota(jnp.int32, (SUBLANES, width), 0)
    lag = (lax.broadcasted_iota(jnp.int32, (rows, rows), 0)
           - lax.broadcasted_iota(jnp.int32, (rows, rows), 1))
    cw = cw_ref[...]
    xa = cw[CONV_WIDTH - 1:CONV_WIDTH] * xz + cb_ref[...]
    for d in range(1, CONV_WIDTH):
        moved = jnp.dot((lag == d).astype(BF16), xzb, preferred_element_type=F32)
        head = jnp.where(row8 < d, pltpu.roll(prev8, d, 0), moved[0:SUBLANES])
        shifted = jnp.concatenate([head, moved[SUBLANES:]], axis=0)
        xa = xa + cw[CONV_WIDTH - 1 - d:CONV_WIDTH - d] * shifted
    xprev_ref[...] = xz[rows - SUBLANES:rows]

    xab = xa.astype(BF16)
    r_parts, i_parts = [], []
    for h in range(heads):
        xh = xab[:, h * HEAD_DIM:(h + 1) * HEAD_DIM]
        r_parts.append(jnp.dot(xh, wr_ref[h], preferred_element_type=F32))
        i_parts.append(jnp.dot(xh, wi_ref[h], preferred_element_type=F32))
    r = _sigmoid(jnp.concatenate(r_parts, axis=1) + br_ref[...])
    gate_i = _sigmoid(jnp.concatenate(i_parts, axis=1) + bi_ref[...])

    lam = lam_ref[...]
    softplus_neg_lam = jnp.maximum(-lam, 0.0) + jnp.log1p(jnp.exp(-jnp.abs(lam)))
    a = jnp.exp((-LRU_C) * r * softplus_neg_lam)
    gap = 1.0 - a * a
    mult = gap * lax.rsqrt(jnp.maximum(gap, 1e-30))
    row = lax.broadcasted_iota(jnp.int32, (rows, width), 0)
    mult = jnp.where(jnp.logical_and(row == 0, s == 0), 1.0, mult)
    u = mult * (gate_i * xa)

    sub = jnp.bitwise_and(row, SUBLANES - 1)
    d = 1
    while d < SUBLANES:
        a_sh = _tile_roll(a, d)
        u_sh = _tile_roll(u, d)
        valid = sub >= d
        u = jnp.where(valid, u + a * u_sh, u)
        a = jnp.where(valid, a * a_sh, a)
        d *= 2
    a_s[...] = a
    u_s[...] = u
    carry = hprev_ref[0:1, :]
    for t in range(tiles):
        grp = slice(t * SUBLANES, (t + 1) * SUBLANES)
        a_g, u_g = a_s[grp, :], u_s[grp, :]
        u_s[grp, :] = u_g + a_g * carry
        carry = u_g[SUBLANES - 1:SUBLANES] + a_g[SUBLANES - 1:SUBLANES] * carry
    hprev_ref[...] = jnp.broadcast_to(carry, hprev_ref.shape)
    h = u_s[...]

    zg = zg_ref[...].astype(F32)
    gelu = 0.5 * zg * (1.0 + jnp.tanh(math.sqrt(2.0 / math.pi) * (zg + 0.044715 * (zg * zg * zg))))
    y = h * gelu
    ms = jnp.mean(y * y, axis=-1, keepdims=True)
    o_ref[...] = (y * lax.rsqrt(ms + NORM_EPS) * ng_ref[...]).astype(o_ref.dtype)


def _lru(z, conv_w, conv_b, w_r, b_r, w_i, b_i, lam, norm_gain, *, batch, seq, rows=256,
         cast=None):
    heads = w_r.shape[0]
    width = heads * HEAD_DIM
    nblk = seq // rows
    row_spec = lambda c: pl.BlockSpec((rows, width), lambda b, s, c=c: (b * nblk + s, c))
    vec = pl.BlockSpec((1, width), lambda b, s: (0, 0))
    wspec = pl.BlockSpec((heads, HEAD_DIM, HEAD_DIM), lambda b, s: (0, 0, 0))
    return _call_with_cast(
        functools.partial(_lru_kernel, rows=rows, heads=heads), cast,
        grid=(batch, nblk),
        in_specs=[row_spec(0), row_spec(1),
                  pl.BlockSpec((CONV_WIDTH, width), lambda b, s: (0, 0)), vec,
                  wspec, vec, wspec, vec, vec, vec],
        out_specs=pl.BlockSpec((rows, width), lambda b, s: (b * nblk + s, 0)),
        out_shape=jax.ShapeDtypeStruct((batch * seq, width), BF16),
        args=(z, z, conv_w, conv_b.reshape(1, width), w_r.astype(BF16), b_r.reshape(1, width),
              w_i.astype(BF16), b_i.reshape(1, width), lam.reshape(1, width),
              norm_gain.reshape(1, width)),
        scratch_shapes=[pltpu.VMEM((SUBLANES, width), F32), pltpu.VMEM((SUBLANES, width), F32),
                        pltpu.VMEM((rows, width), F32), pltpu.VMEM((rows, width), F32)],
        compiler_params=_params(("parallel", "arbitrary")),
        name="rg_lru")


def _split3(x):
    hi = x.astype(BF16)
    r1 = x - hi.astype(F32)
    mid = r1.astype(BF16)
    lo = (r1 - mid.astype(F32)).astype(BF16)
    return hi, mid, lo


def _hgrn_kernel(zq_ref, zf_ref, zv_ref, zg_ref, lbp_ref, ng_ref, o_ref, st_s, *, rows, layer,
                 heads):
    s = pl.program_id(2)
    C, c = HGRN_CHUNK, HGRN_SUB
    D = HEAD_DIM

    @pl.when(s == 0)
    def _():
        st_s[...] = jnp.zeros_like(st_s)

    lbp = lbp_ref[...].astype(F32)
    e = jnp.exp(lbp - jnp.max(lbp, axis=0, keepdims=True))
    sm = e / jnp.sum(e, axis=0, keepdims=True)
    lb_all = jnp.zeros((1, heads * D), F32)
    for r_ in range(1, layer + 1):
        lb_all = lb_all + sm[r_:r_ + 1]

    row = lax.broadcasted_iota(jnp.int32, (C, C), 0)
    col = lax.broadcasted_iota(jnp.int32, (C, C), 1)
    tri = (row >= col).astype(BF16)
    differ = jnp.bitwise_xor(row, col)
    levels = []
    m = C // 2
    while m >= c:
        levels.append((m, (row > col) & (differ >= m) & (differ < 2 * m)))
        m //= 2
    diag_masks = [(differ < c) & (row - col == dlt) for dlt in range(c)]

    for ch, h in [(ch, h) for ch in range(rows // C) for h in range(heads)]:
        sl = slice(ch * C, (ch + 1) * C)
        hc = slice(h * D, (h + 1) * D)
        lb = lb_all[:, hc]
        zq = zq_ref[sl, hc].astype(F32)
        zf = zf_ref[sl, hc].astype(F32)
        q = zq * _sigmoid(zq) * (D ** -0.5)
        half_tanh = 0.5 * jnp.tanh(0.5 * zf)
        f = lb + (1.0 - lb) * (0.5 + half_tanh)
        k = (1.0 - lb) * (0.5 - half_tanh)
        log_sig = jnp.minimum(zf, 0.0) - jnp.log(1.0 + jnp.exp(-jnp.abs(zf)))
        y = jnp.log1p(-lb) + log_sig
        if layer == 0:
            log_f = y
        else:
            la = jnp.log(lb)
            log_f = jnp.maximum(la, y) + jnp.log(1.0 + jnp.exp(-jnp.abs(la - y)))

        hi, mid, lo = _split3(log_f)
        g3 = jnp.dot(tri, jnp.concatenate([hi, mid, lo], axis=1), preferred_element_type=F32)
        G = g3[:, 0:D] + g3[:, D:2 * D] + g3[:, 2 * D:3 * D]
        vb = zv_ref[sl, hc].astype(BF16)
        g_last = G[C - 1:C]

        a = jnp.zeros((C, C), F32)
        dec = None
        for dlt in range(c):
            if dlt == 0:
                w = q * k
            else:
                f_sh = f if dlt == 1 else _tile_roll(f, dlt - 1)
                dec = f_sh if dlt == 1 else dec * f_sh
                w = q * _tile_roll(k, dlt) * dec
            a = jnp.where(diag_masks[dlt], jnp.sum(w, axis=-1, keepdims=True), a)

        for m, mask in levels:
            g_bnd = jnp.concatenate(
                [jnp.broadcast_to(G[p * 2 * m + m - 1:p * 2 * m + m], (2 * m, D))
                 for p in range(C // (2 * m))], axis=0)
            e_ref = jnp.exp(-jnp.abs(G - g_bnd))
            sc = lax.dot_general((q * e_ref).astype(BF16), (k * e_ref).astype(BF16),
                                 (((1,), (1,)), ((), ())), preferred_element_type=F32)
            a = jnp.where(mask, sc, a)

        st = st_s[h]
        qg = (q * jnp.exp(G)).astype(BF16)
        o = (jnp.dot(a.astype(BF16), vb, preferred_element_type=F32)
             + lax.dot_general(qg, st.astype(BF16), (((1,), (1,)), ((), ())),
                               preferred_element_type=F32))
        kd = (k * jnp.exp(g_last - G)).astype(BF16)
        st_s[h] = st * jnp.exp(g_last) + lax.dot_general(
            vb, kd, (((0,), (0,)), ((), ())), preferred_element_type=F32)

        ms = jnp.mean(o * o, axis=-1, keepdims=True)
        zg = zg_ref[sl, hc].astype(F32)
        o_ref[sl, hc] = (o * lax.rsqrt(ms + NORM_EPS) * ng_ref[...]
                         * (zg * _sigmoid(zg))).astype(o_ref.dtype)


def _hgrn(z, lower_bounds, norm_gain, *, layer, batch, seq, heads, col0, rows=512, heads_per_step=2,
          cast=None):
    nblk = seq // rows
    D = HEAD_DIM
    depth = lower_bounds.shape[0]
    hp = heads_per_step
    width = hp * D
    groups = heads // hp

    def zspec(section):
        off = col0 // width + section * groups
        return pl.BlockSpec((rows, width), lambda b, h, s, off=off: (b * nblk + s, off + h))

    return _call_with_cast(
        functools.partial(_hgrn_kernel, rows=rows, layer=layer, heads=hp), cast,
        grid=(batch, groups, nblk),
        in_specs=[zspec(0), zspec(1), zspec(2), zspec(3),
                  pl.BlockSpec((depth, width), lambda b, h, s: (0, h)),
                  pl.BlockSpec((1, D), lambda b, h, s: (0, 0))],
        out_specs=pl.BlockSpec((rows, width), lambda b, h, s: (b * nblk + s, h)),
        out_shape=jax.ShapeDtypeStruct((batch * seq, heads * D), BF16),
        args=(z, z, z, z, lower_bounds, norm_gain.reshape(1, D)),
        scratch_shapes=[pltpu.VMEM((hp, D, D), F32)],
        compiler_params=_params(("parallel", "parallel", "arbitrary")),
        name="hgrn2")


def _norm_rope(x, gain, cos, sin_signed, scale):
    sq = x * x
    hi = sq.astype(BF16)
    lo = (sq - hi.astype(F32)).astype(BF16)
    ones = jnp.ones((2 * HEAD_DIM, HEAD_DIM), BF16)
    ms = jnp.dot(jnp.concatenate([hi, lo], axis=1), ones,
                 preferred_element_type=F32) * (1.0 / HEAD_DIM)
    xn = x * lax.rsqrt(ms + NORM_EPS) * gain
    return (xn * cos + pltpu.roll(xn, HEAD_DIM // 2, 1) * sin_signed) * scale


def _flash_kernel(zq_ref, zk_ref, v_ref, cos_ref, sin_ref, qg_ref, kg_ref, lp_ref, g_ref, o_ref,
                  q_s, k_s, m_s, l_s, acc_s, a_s, p_s, *, blk, seq, heads, lam_init):
    i = pl.program_id(2)
    D = HEAD_DIM
    lanes = blk // D
    maps = tuple(range(2 * heads))

    @pl.when(i == 0)
    def _():
        def prep_keys(r, carry):
            rows = pl.ds(pl.multiple_of(r * blk, blk), blk)
            for e in maps:
                k_s[rows, e * D:(e + 1) * D] = _norm_rope(
                    zk_ref[rows, e * D:(e + 1) * D].astype(F32), kg_ref[...],
                    cos_ref[rows, :], sin_ref[rows, :], 1.0).astype(k_s.dtype)
            return carry

        lax.fori_loop(0, seq // blk, prep_keys, 0)

    q_rows = pl.ds(pl.multiple_of(i * blk, blk), blk)
    for e in maps:
        q_s[:, e * D:(e + 1) * D] = _norm_rope(
            zq_ref[:, e * D:(e + 1) * D].astype(F32), qg_ref[...],
            cos_ref[q_rows, :], sin_ref[q_rows, :], D ** -0.5 * LOG2_E).astype(q_s.dtype)

    m_s[...] = jnp.full_like(m_s, -jnp.inf)
    l_s[...] = jnp.zeros_like(l_s)
    acc_s[...] = jnp.zeros_like(acc_s)

    def scores(t, masked, which=maps):
        kv_rows = pl.ds(pl.multiple_of(t * blk, blk), blk)
        for e in which:
            s = lax.dot_general(q_s[:, e * D:(e + 1) * D], k_s[kv_rows, e * D:(e + 1) * D],
                                (((1,), (1,)), ((), ())), preferred_element_type=F32)
            if masked:
                row = lax.broadcasted_iota(jnp.int32, (blk, blk), 0)
                col = lax.broadcasted_iota(jnp.int32, (blk, blk), 1)
                s = jnp.where(col <= row, s, -jnp.inf)
            m_prev = m_s[e]
            m_new = jnp.maximum(m_prev, jnp.max(s, axis=-1, keepdims=True))
            alpha = jnp.exp2(m_prev - m_new)
            p = jnp.exp2(s - jnp.tile(m_new, (1, lanes)))
            l_s[e] = alpha * l_s[e] + jnp.sum(p, axis=-1, keepdims=True)
            m_s[e] = m_new
            a_s[e] = alpha
            p_s[e] = p.astype(BF16)

    def weighted_values(t, which=maps):
        kv_rows = pl.ds(pl.multiple_of(t * blk, blk), blk)
        for e in which:
            v = v_ref[kv_rows, (e // 2) * 2 * D:(e // 2 + 1) * 2 * D]
            acc_s[e] = jnp.tile(a_s[e], (1, 2)) * acc_s[e] + jnp.dot(
                p_s[e], v, preferred_element_type=F32)

    @pl.when(i == 0)
    def _():
        scores(0, True)

    @pl.when(i > 0)
    def _():
        scores(0, False)

        def pipelined(t, carry):
            for e in maps:
                weighted_values(t - 1, (e,))
                scores(t, False, (e,))
            return carry

        lax.fori_loop(1, i, pipelined, 0)
        weighted_values(i - 1)
        scores(i, True)

    weighted_values(i)

    lp = lp_ref[...].astype(F32)
    lam = (jnp.exp(jnp.sum(lp[0:1] * lp[1:2], axis=-1, keepdims=True))
           - jnp.exp(jnp.sum(lp[2:3] * lp[3:4], axis=-1, keepdims=True)) + lam_init)
    for h in range(heads):
        o = (acc_s[2 * h] * jnp.tile(1.0 / l_s[2 * h], (1, 2))
             - lam * (acc_s[2 * h + 1] * jnp.tile(1.0 / l_s[2 * h + 1], (1, 2))))
        ms = jnp.mean(o * o, axis=-1, keepdims=True)
        o_ref[:, h * 2 * D:(h + 1) * 2 * D] = (
            o * lax.rsqrt(ms + NORM_EPS) * g_ref[...] * (1.0 - lam_init)).astype(o_ref.dtype)


def _flash(z, cos, sin_signed, q_gain, k_gain, lam_params, subln_gain, *, layer, batch, seq, heads,
           col0, blk=512, heads_per_step=2, cast=None):
    nblk = seq // blk
    D = HEAD_DIM
    hp = heads_per_step
    width = hp * 2 * D
    groups = heads // hp
    lam_init = 0.8 - 0.6 * math.exp(-0.3 * layer)
    cb = col0 // width
    qspec = pl.BlockSpec((blk, width), lambda b, h, i: (b * nblk + i, cb + h))
    kspec = pl.BlockSpec((seq, width), lambda b, h, i: (b, cb + groups + h))
    vspec = pl.BlockSpec((seq, width), lambda b, h, i: (b, cb + 2 * groups + h))
    table = pl.BlockSpec((seq, D), lambda b, h, i: (0, 0))
    vec = pl.BlockSpec((1, D), lambda b, h, i: (0, 0))
    nmap = 2 * hp
    return _call_with_cast(
        functools.partial(_flash_kernel, blk=blk, seq=seq, heads=hp, lam_init=lam_init), cast,
        grid=(batch, groups, nblk),
        in_specs=[qspec, kspec, vspec, table, table, vec, vec,
                  pl.BlockSpec((4, D), lambda b, h, i: (0, 0)),
                  pl.BlockSpec((1, 2 * D), lambda b, h, i: (0, 0))],
        out_specs=pl.BlockSpec((blk, width), lambda b, h, i: (b * nblk + i, h)),
        out_shape=jax.ShapeDtypeStruct((batch * seq, heads * 2 * D), BF16),
        args=(z, z, z, cos, sin_signed, q_gain.reshape(1, D), k_gain.reshape(1, D), lam_params,
              subln_gain.reshape(1, 2 * D)),
        scratch_shapes=[pltpu.VMEM((blk, width), BF16), pltpu.VMEM((seq, width), BF16),
                        pltpu.VMEM((nmap, blk, D), F32), pltpu.VMEM((nmap, blk, D), F32),
                        pltpu.VMEM((nmap, blk, 2 * D), F32), pltpu.VMEM((nmap, blk, D), F32),
                        pltpu.VMEM((nmap, blk, blk), BF16)],
        compiler_params=_params(("parallel", "parallel", "arbitrary")),
        name="diff_flash")


def kernel(x, ln1_gain, w_in, conv_w, conv_b, lru_w_r, lru_b_r, lru_w_i, lru_b_i, lru_lambda, lru_norm_gain, hgrn_lower_bounds, hgrn_norm_gain, q_norm_gain, k_norm_gain, diff_lambda, diff_subln_gain, w_out, ln2_gain, w_ff1, w_ff2):
    batch, seq, d_model = x.shape
    depth = w_in.shape[0]
    D = HEAD_DIM
    lru_width = conv_w.shape[-1]
    key_width = hgrn_lower_bounds.shape[-1]
    in_width = w_in.shape[-1]
    mix_width = w_out.shape[1]
    diff_width = in_width - 2 * key_width - 2 * mix_width
    val_width = mix_width - lru_width - diff_width
    assert q_norm_gain.shape[-1] == D and hgrn_norm_gain.shape[-1] == D
    assert lru_w_r.shape[-1] == D and key_width == val_width
    hgrn_heads = key_width // D
    diff_heads = diff_width // (2 * D)
    off_b = 2 * lru_width
    off_c = off_b + 2 * key_width + 2 * val_width

    inv = 1.0 / (ROPE_THETA ** (jnp.arange(0, D, 2, dtype=F32) / D))
    ang = jnp.arange(seq, dtype=F32)[:, None] * inv[None, :]
    ang = jnp.concatenate([ang, ang], axis=-1)
    cos = jnp.cos(ang)
    sign = jnp.concatenate([-jnp.ones((D // 2,), F32), jnp.ones((D // 2,), F32)])
    sin_signed = jnp.sin(ang) * sign[None, :]

    xf = x.reshape(batch * seq, d_model)
    xg, ss = _rmsnorm(xf, ln1_gain[0]), None
    w_in_b = w_in[0].astype(BF16)
    for l in range(depth):
        z, w_out_b = _matmul([xg], w_in_b, row_ss=ss, out_dtype=BF16, casts=[(w_out, l)])

        lru_out = _lru(z, conv_w[l], conv_b[l], lru_w_r[l], lru_b_r[l], lru_w_i[l], lru_b_i[l],
                       lru_lambda[l], lru_norm_gain[l], batch=batch, seq=seq,
                       cast=(w_in, l + 1) if l + 1 < depth else None)
        ya, w_in_b = lru_out if l + 1 < depth else (lru_out, None)
        yb, w_ff2_b = _hgrn(z, hgrn_lower_bounds, hgrn_norm_gain[l], layer=l, batch=batch, seq=seq,
                            heads=hgrn_heads, col0=off_b, cast=(w_ff2, l))
        yc, w_ff1_b = _flash(z, cos, sin_signed, q_norm_gain[l], k_norm_gain[l], diff_lambda[l],
                             diff_subln_gain[l], layer=l, batch=batch, seq=seq, heads=diff_heads,
                             col0=off_c, cast=(w_ff1, l))

        xf, xg, ss = _matmul([ya, yb, yc], w_out_b, mode="residual", residual=xf,
                             next_gain=ln2_gain[l], vmem_limit=VMEM_LIMIT_WIDE)
        u = _matmul([xg], w_ff1_b, mode="relu2", row_ss=ss, out_dtype=BF16)
        if l + 1 < depth:
            xf, xg, ss = _matmul([u], w_ff2_b, mode="residual", residual=xf,
                                 next_gain=ln1_gain[l + 1])
        else:
            xf = _matmul([u], w_ff2_b, mode="residual", residual=xf)
    return xf.reshape(batch, seq, d_model)
```

```python
import functools
import math

import jax
import jax.numpy as jnp
from jax import lax
from jax.experimental import pallas as pl
from jax.experimental.pallas import tpu as pltpu

F32 = jnp.float32
BF16 = jnp.bfloat16

NORM_EPS = 1e-6
LRU_C = 8.0
ROPE_THETA = 10000.0
CONV_WIDTH = 4
LOG2_E = 1.4426950408889634
HEAD_DIM = 128
HGRN_CHUNK = 128
SUBLANES = 8
HGRN_SUB = SUBLANES
VMEM_LIMIT = 56 * 1024 * 1024
VMEM_LIMIT_WIDE = 62 * 1024 * 1024


def _params(semantics, vmem_limit=VMEM_LIMIT):
    return pltpu.CompilerParams(dimension_semantics=semantics, vmem_limit_bytes=vmem_limit)


def _sigmoid(x):
    return 0.5 * jnp.tanh(0.5 * x) + 0.5


def _rmsnorm_kernel(x_ref, g_ref, o_ref):
    x = x_ref[...]
    ms = jnp.mean(x * x, axis=-1, keepdims=True)
    o_ref[...] = (x * lax.rsqrt(ms + NORM_EPS) * g_ref[...]).astype(o_ref.dtype)


def _rmsnorm(x, gain, tm=256):
    n, d = x.shape
    return pl.pallas_call(
        _rmsnorm_kernel,
        grid=(n // tm,),
        in_specs=[pl.BlockSpec((tm, d), lambda i: (i, 0)),
                  pl.BlockSpec((1, d), lambda i: (0, 0))],
        out_specs=pl.BlockSpec((tm, d), lambda i: (i, 0)),
        out_shape=jax.ShapeDtypeStruct((n, d), BF16),
        compiler_params=_params(("parallel",)),
        name="rmsnorm",
    )(x, gain.reshape(1, d))


LANES = 128


def _row_factor(ss_ref, width, tn):
    rstd = lax.rsqrt(ss_ref[...] * (1.0 / width) + NORM_EPS)
    return jnp.tile(rstd, (1, tn // LANES))


def _finish(acc, j, *, mode, r_ref, g_ref, o_ref, xg_ref, ss_ref):
    if mode == "relu2":
        acc = jnp.square(jnp.maximum(acc, 0.0))
    elif mode == "residual":
        acc = acc + r_ref[...]
    o_ref[...] = acc.astype(o_ref.dtype)
    if xg_ref is not None:
        xg_ref[...] = (acc * g_ref[...]).astype(xg_ref.dtype)

        @pl.when(j == 0)
        def _():
            ss_ref[...] = jnp.zeros_like(ss_ref)

        ss_ref[...] += jnp.sum(acc * acc, axis=-1, keepdims=True)


def _unpack(refs, n_a, row_scale, mode, emit_norm, n_cast):
    it = iter(refs)
    a_refs = [next(it) for _ in range(n_a)]
    w_ref = next(it)
    ss_in = next(it) if row_scale else None
    r_ref = next(it) if mode == "residual" else None
    g_ref = next(it) if emit_norm else None
    cast_src = [next(it) for _ in range(n_cast)]
    o_ref = next(it)
    xg_ref = next(it) if emit_norm else None
    ss_ref = next(it) if emit_norm else None
    for src_ref in cast_src:
        dst_ref = next(it)
        dst_ref[...] = src_ref[...].astype(dst_ref.dtype)
    return a_refs, w_ref, ss_in, r_ref, g_ref, o_ref, xg_ref, ss_ref, list(it)


def _mm_fullk_kernel(*refs, splits, mode, row_scale, emit_norm, n_cast):
    a_refs, w_ref, ss_in, r_ref, g_ref, o_ref, xg_ref, ss_ref, _ = _unpack(
        refs, len(splits), row_scale, mode, emit_norm, n_cast)
    acc, off = None, 0
    for a_ref, kp in zip(a_refs, splits):
        part = jnp.dot(a_ref[...], w_ref[off:off + kp, :], preferred_element_type=F32)
        acc = part if acc is None else acc + part
        off += kp
    if row_scale:
        acc = acc * _row_factor(ss_in, off, o_ref.shape[-1])
    _finish(acc, pl.program_id(1), mode=mode, r_ref=r_ref, g_ref=g_ref, o_ref=o_ref,
            xg_ref=xg_ref, ss_ref=ss_ref)


def _mm_kloop_kernel(*refs, nk, emit_norm):
    (a_ref,), w_ref, _, r_ref, g_ref, o_ref, xg_ref, ss_ref, _ = _unpack(
        refs, 1, False, "residual", emit_norm, 0)
    k = pl.program_id(2)

    @pl.when(k == 0)
    def _():
        o_ref[...] = r_ref[...] + jnp.dot(a_ref[...], w_ref[...], preferred_element_type=F32)

    @pl.when(k > 0)
    def _():
        o_ref[...] += jnp.dot(a_ref[...], w_ref[...], preferred_element_type=F32)

    if emit_norm:
        @pl.when(k == nk - 1)
        def _():
            acc = o_ref[...]
            xg_ref[...] = (acc * g_ref[...]).astype(xg_ref.dtype)

            @pl.when(pl.program_id(1) == 0)
            def _():
                ss_ref[...] = jnp.zeros_like(ss_ref)

            ss_ref[...] += jnp.sum(acc * acc, axis=-1, keepdims=True)


def _matmul(a_parts, w, *, mode="plain", residual=None, row_ss=None, next_gain=None, casts=(),
            out_dtype=F32, tm=1024, tn=1024, tk=4096, vmem_limit=VMEM_LIMIT):
    m = a_parts[0].shape[0]
    splits = tuple(p.shape[1] for p in a_parts)
    kdim = sum(splits)
    n = w.shape[-1]
    fullk = kdim <= 4096
    row_scale = row_ss is not None
    emit_norm = next_gain is not None
    tm, tn = min(tm, m), min(tn, n)

    if fullk:
        grid = (m // tm, n // tn)
        tile = lambda i, j: (i, j)
        rows = lambda i, j: (i, 0)
        cols = lambda i, j: (0, j)
        in_specs = [pl.BlockSpec((tm, kp), rows) for kp in splits]
        in_specs.append(pl.BlockSpec((kdim, tn), cols))
        body = functools.partial(_mm_fullk_kernel, splits=splits, mode=mode, row_scale=row_scale,
                                 emit_norm=emit_norm, n_cast=len(casts))
        scratch = []
        semantics = ("parallel", "arbitrary")
    else:
        assert len(a_parts) == 1 and not row_scale and not casts
        assert mode == "residual" and out_dtype == F32
        tk = min(tk, kdim)
        nk = kdim // tk
        grid = (m // tm, n // tn, nk)
        tile = lambda i, j, k: (i, j)
        rows = lambda i, j, k: (i, 0)
        cols = lambda i, j, k: (0, j)
        in_specs = [pl.BlockSpec((tm, tk), lambda i, j, k: (i, k)),
                    pl.BlockSpec((tk, tn), lambda i, j, k: (k, j))]
        body = functools.partial(_mm_kloop_kernel, nk=nk, emit_norm=emit_norm)
        scratch = []
        semantics = ("parallel", "arbitrary", "arbitrary")

    args = list(a_parts) + [w]
    if row_scale:
        in_specs.append(pl.BlockSpec((tm, LANES), rows))
        args.append(row_ss)
    if mode == "residual":
        in_specs.append(pl.BlockSpec((tm, tn), tile))
        args.append(residual)
    out_specs = [pl.BlockSpec((tm, tn), tile)]
    out_shape = [jax.ShapeDtypeStruct((m, n), out_dtype)]
    if emit_norm:
        in_specs.append(pl.BlockSpec((1, tn), cols))
        args.append(next_gain.reshape(1, n))
        out_specs += [pl.BlockSpec((tm, tn), tile), pl.BlockSpec((tm, LANES), rows)]
        out_shape += [jax.ShapeDtypeStruct((m, n), BF16), jax.ShapeDtypeStruct((m, LANES), F32)]
    for w_f32, layer in casts:
        _, wr, wc = w_f32.shape
        col_blocks = max(c for c in range(1, grid[1] + 1)
                         if grid[1] % c == 0 and wc % (c * LANES) == 0)
        br, bc = wr // grid[0], wc // col_blocks
        assert wr % grid[0] == 0 and br % 8 == 0
        col = lambda j, last=col_blocks - 1: jnp.minimum(j, last)
        in_specs.append(pl.BlockSpec((None, br, bc),
                                     lambda i, j, layer=layer, col=col: (layer, i, col(j))))
        args.append(w_f32)
        out_specs.append(pl.BlockSpec((br, bc), lambda i, j, col=col: (i, col(j))))
        out_shape.append(jax.ShapeDtypeStruct((wr, wc), BF16))
    outs = pl.pallas_call(
        body, grid=grid, in_specs=in_specs, out_specs=out_specs, out_shape=out_shape,
        scratch_shapes=scratch, compiler_params=_params(semantics, vmem_limit),
        name=("matmul_fullk_" if fullk else "matmul_kloop_") + mode,
    )(*args)
    return outs if len(outs) > 1 else outs[0]


def _cast_job(w_f32, layer, grid):
    _, wr, wc = w_f32.shape
    steps = math.prod(grid)
    br = wr // steps
    assert wr % steps == 0 and br % 16 == 0

    def step(*idx):
        flat = idx[0]
        for n, i in zip(grid[1:], idx[1:]):
            flat = flat * n + i
        return flat

    return (pl.BlockSpec((None, br, wc), lambda *idx: (layer, step(*idx), 0)),
            pl.BlockSpec((br, wc), lambda *idx: (step(*idx), 0)),
            jax.ShapeDtypeStruct((wr, wc), BF16))


def _hosting_cast(body, n_in, n_out):
    def hosted(*refs):
        ins, src = refs[:n_in], refs[n_in]
        outs, dst = refs[n_in + 1:n_in + 1 + n_out], refs[n_in + 1 + n_out]
        dst[...] = src[...].astype(dst.dtype)
        body(*ins, *outs, *refs[n_in + 2 + n_out:])
    return hosted


def _call_with_cast(body, cast, *, grid, in_specs, out_specs, out_shape, args, **kwargs):
    if cast is None:
        return pl.pallas_call(body, grid=grid, in_specs=in_specs, out_specs=out_specs,
                              out_shape=out_shape, **kwargs)(*args)
    c_in, c_out, c_shape = _cast_job(cast[0], cast[1], grid)
    return pl.pallas_call(
        _hosting_cast(body, len(in_specs), 1), grid=grid, in_specs=in_specs + [c_in],
        out_specs=[out_specs, c_out], out_shape=[out_shape, c_shape], **kwargs)(*args, cast[0])


def _tile_roll(x, shift):
    rows, d = x.shape
    x3 = x.reshape(rows // SUBLANES, SUBLANES, d)
    return pltpu.roll(x3, shift, 1).reshape(rows, d)


def _lru_kernel(zg_ref, zx_ref, cw_ref, cb_ref, wr_ref, br_ref, wi_ref, bi_ref, lam_ref, ng_ref,
                o_ref, xprev_ref, hprev_ref, a_s, u_s, *, rows, heads):
    s = pl.program_id(1)
    width = heads * HEAD_DIM
    tiles = rows // SUBLANES

    @pl.when(s == 0)
    def _():
        xprev_ref[...] = jnp.zeros_like(xprev_ref)
        hprev_ref[...] = jnp.zeros_like(hprev_ref)

    assert zx_ref.dtype == BF16
    xzb = zx_ref[...]
    xz = xzb.astype(F32)
    prev8 = xprev_ref[...]
    row8 = lax.broadcasted_iota(jnp.int32, (SUBLANES, width), 0)
    lag = (lax.broadcasted_iota(jnp.int32, (rows, rows), 0)
           - lax.broadcasted_iota(jnp.int32, (rows, rows), 1))
    cw = cw_ref[...]
    xa = cw[CONV_WIDTH - 1:CONV_WIDTH] * xz + cb_ref[...]
    for d in range(1, CONV_WIDTH):
        moved = jnp.dot((lag == d).astype(BF16), xzb, preferred_element_type=F32)
        head = jnp.where(row8 < d, pltpu.roll(prev8, d, 0), moved[0:SUBLANES])
        shifted = jnp.concatenate([head, moved[SUBLANES:]], axis=0)
        xa = xa + cw[CONV_WIDTH - 1 - d:CONV_WIDTH - d] * shifted
    xprev_ref[...] = xz[rows - SUBLANES:rows]

    xab = xa.astype(BF16)
    r_parts, i_parts = [], []
    for h in range(heads):
        xh = xab[:, h * HEAD_DIM:(h + 1) * HEAD_DIM]
        r_parts.append(jnp.dot(xh, wr_ref[h], preferred_element_type=F32))
        i_parts.append(jnp.dot(xh, wi_ref[h], preferred_element_type=F32))
    r = _sigmoid(jnp.concatenate(r_parts, axis=1) + br_ref[...])
    gate_i = _sigmoid(jnp.concatenate(i_parts, axis=1) + bi_ref[...])

    lam = lam_ref[...]
    softplus_neg_lam = jnp.maximum(-lam, 0.0) + jnp.log1p(jnp.exp(-jnp.abs(lam)))
    a = jnp.exp((-LRU_C) * r * softplus_neg_lam)
    gap = 1.0 - a * a
    mult = gap * lax.rsqrt(jnp.maximum(gap, 1e-30))
    row = lax.broadcasted_iota(jnp.int32, (rows, width), 0)
    mult = jnp.where(jnp.logical_and(row == 0, s == 0), 1.0, mult)
    u = mult * (gate_i * xa)

    sub = jnp.bitwise_and(row, SUBLANES - 1)
    d = 1
    while d < SUBLANES:
        a_sh = _tile_roll(a, d)
        u_sh = _tile_roll(u, d)
        valid = sub >= d
        u = jnp.where(valid, u + a * u_sh, u)
        a = jnp.where(valid, a * a_sh, a)
        d *= 2
    a_s[...] = a
    u_s[...] = u
    carry = hprev_ref[0:1, :]
    for t in range(tiles):
        grp = slice(t * SUBLANES, (t + 1) * SUBLANES)
        a_g, u_g = a_s[grp, :], u_s[grp, :]
        u_s[grp, :] = u_g + a_g * carry
        carry = u_g[SUBLANES - 1:SUBLANES] + a_g[SUBLANES - 1:SUBLANES] * carry
    hprev_ref[...] = jnp.broadcast_to(carry, hprev_ref.shape)
    h = u_s[...]

    zg = zg_ref[...].astype(F32)
    gelu = 0.5 * zg * (1.0 + jnp.tanh(math.sqrt(2.0 / math.pi) * (zg + 0.044715 * (zg * zg * zg))))
    y = h * gelu
    ms = jnp.mean(y * y, axis=-1, keepdims=True)
    o_ref[...] = (y * lax.rsqrt(ms + NORM_EPS) * ng_ref[...]).astype(o_ref.dtype)


def _lru(z, conv_w, conv_b, w_r, b_r, w_i, b_i, lam, norm_gain, *, batch, seq, rows=256,
         cast=None):
    heads = w_r.shape[0]
    width = heads * HEAD_DIM
    nblk = seq // rows
    row_spec = lambda c: pl.BlockSpec((rows, width), lambda b, s, c=c: (b * nblk + s, c))
    vec = pl.BlockSpec((1, width), lambda b, s: (0, 0))
    wspec = pl.BlockSpec((heads, HEAD_DIM, HEAD_DIM), lambda b, s: (0, 0, 0))
    return _call_with_cast(
        functools.partial(_lru_kernel, rows=rows, heads=heads), cast,
        grid=(batch, nblk),
        in_specs=[row_spec(0), row_spec(1),
                  pl.BlockSpec((CONV_WIDTH, width), lambda b, s: (0, 0)), vec,
                  wspec, vec, wspec, vec, vec, vec],
        out_specs=pl.BlockSpec((rows, width), lambda b, s: (b * nblk + s, 0)),
        out_shape=jax.ShapeDtypeStruct((batch * seq, width), BF16),
        args=(z, z, conv_w, conv_b.reshape(1, width), w_r.astype(BF16), b_r.reshape(1, width),
              w_i.astype(BF16), b_i.reshape(1, width), lam.reshape(1, width),
              norm_gain.reshape(1, width)),
        scratch_shapes=[pltpu.VMEM((SUBLANES, width), F32), pltpu.VMEM((SUBLANES, width), F32),
                        pltpu.VMEM((rows, width), F32), pltpu.VMEM((rows, width), F32)],
        compiler_params=_params(("parallel", "arbitrary")),
        name="rg_lru")


def _split3(x):
    hi = x.astype(BF16)
    r1 = x - hi.astype(F32)
    mid = r1.astype(BF16)
    lo = (r1 - mid.astype(F32)).astype(BF16)
    return hi, mid, lo


def _hgrn_kernel(zq_ref, zf_ref, zv_ref, zg_ref, lbp_ref, ng_ref, o_ref, st_s, *, rows, layer,
                 heads):
    s = pl.program_id(2)
    C, c = HGRN_CHUNK, HGRN_SUB
    D = HEAD_DIM

    @pl.when(s == 0)
    def _():
        st_s[...] = jnp.zeros_like(st_s)

    lbp = lbp_ref[...].astype(F32)
    e = jnp.exp(lbp - jnp.max(lbp, axis=0, keepdims=True))
    sm = e / jnp.sum(e, axis=0, keepdims=True)
    lb_all = jnp.zeros((1, heads * D), F32)
    for r_ in range(1, layer + 1):
        lb_all = lb_all + sm[r_:r_ + 1]

    row = lax.broadcasted_iota(jnp.int32, (C, C), 0)
    col = lax.broadcasted_iota(jnp.int32, (C, C), 1)
    tri = (row >= col).astype(BF16)
    differ = jnp.bitwise_xor(row, col)
    levels = []
    m = C // 2
    while m >= c:
        levels.append((m, (row > col) & (differ >= m) & (differ < 2 * m)))
        m //= 2
    diag_masks = [(differ < c) & (row - col == dlt) for dlt in range(c)]

    for ch, h in [(ch, h) for ch in range(rows // C) for h in range(heads)]:
        sl = slice(ch * C, (ch + 1) * C)
        hc = slice(h * D, (h + 1) * D)
        lb = lb_all[:, hc]
        zq = zq_ref[sl, hc].astype(F32)
        zf = zf_ref[sl, hc].astype(F32)
        q = zq * _sigmoid(zq) * (D ** -0.5)
        half_tanh = 0.5 * jnp.tanh(0.5 * zf)
        f = lb + (1.0 - lb) * (0.5 + half_tanh)
        k = (1.0 - lb) * (0.5 - half_tanh)
        log_sig = jnp.minimum(zf, 0.0) - jnp.log(1.0 + jnp.exp(-jnp.abs(zf)))
        y = jnp.log1p(-lb) + log_sig
        if layer == 0:
            log_f = y
        else:
            la = jnp.log(lb)
            log_f = jnp.maximum(la, y) + jnp.log(1.0 + jnp.exp(-jnp.abs(la - y)))

        hi, mid, lo = _split3(log_f)
        g3 = jnp.dot(tri, jnp.concatenate([hi, mid, lo], axis=1), preferred_element_type=F32)
        G = g3[:, 0:D] + g3[:, D:2 * D] + g3[:, 2 * D:3 * D]
        vb = zv_ref[sl, hc].astype(BF16)
        g_last = G[C - 1:C]

        a = jnp.zeros((C, C), F32)
        dec = None
        for dlt in range(c):
            if dlt == 0:
                w = q * k
            else:
                f_sh = f if dlt == 1 else _tile_roll(f, dlt - 1)
                dec = f_sh if dlt == 1 else dec * f_sh
                w = q * _tile_roll(k, dlt) * dec
            a = jnp.where(diag_masks[dlt], jnp.sum(w, axis=-1, keepdims=True), a)

        for m, mask in levels:
            g_bnd = jnp.concatenate(
                [jnp.broadcast_to(G[p * 2 * m + m - 1:p * 2 * m + m], (2 * m, D))
                 for p in range(C // (2 * m))], axis=0)
            e_ref = jnp.exp(-jnp.abs(G - g_bnd))
            sc = lax.dot_general((q * e_ref).astype(BF16), (k * e_ref).astype(BF16),
                                 (((1,), (1,)), ((), ())), preferred_element_type=F32)
            a = jnp.where(mask, sc, a)

        st = st_s[h]
        qg = (q * jnp.exp(G)).astype(BF16)
        o = (jnp.dot(a.astype(BF16), vb, preferred_element_type=F32)
             + lax.dot_general(qg, st.astype(BF16), (((1,), (1,)), ((), ())),
                               preferred_element_type=F32))
        kd = (k * jnp.exp(g_last - G)).astype(BF16)
        st_s[h] = st * jnp.exp(g_last) + lax.dot_general(
            vb, kd, (((0,), (0,)), ((), ())), preferred_element_type=F32)

        ms = jnp.mean(o * o, axis=-1, keepdims=True)
        zg = zg_ref[sl, hc].astype(F32)
        o_ref[sl, hc] = (o * lax.rsqrt(ms + NORM_EPS) * ng_ref[...]
                         * (zg * _sigmoid(zg))).astype(o_ref.dtype)


def _hgrn(z, lower_bounds, norm_gain, *, layer, batch, seq, heads, col0, rows=512, heads_per_step=2,
          cast=None):
    nblk = seq // rows
    D = HEAD_DIM
    depth = lower_bounds.shape[0]
    hp = heads_per_step
    width = hp * D
    groups = heads // hp

    def zspec(section):
        off = col0 // width + section * groups
        return pl.BlockSpec((rows, width), lambda b, h, s, off=off: (b * nblk + s, off + h))

    return _call_with_cast(
        functools.partial(_hgrn_kernel, rows=rows, layer=layer, heads=hp), cast,
        grid=(batch, groups, nblk),
        in_specs=[zspec(0), zspec(1), zspec(2), zspec(3),
                  pl.BlockSpec((depth, width), lambda b, h, s: (0, h)),
                  pl.BlockSpec((1, D), lambda b, h, s: (0, 0))],
        out_specs=pl.BlockSpec((rows, width), lambda b, h, s: (b * nblk + s, h)),
        out_shape=jax.ShapeDtypeStruct((batch * seq, heads * D), BF16),
        args=(z, z, z, z, lower_bounds, norm_gain.reshape(1, D)),
        scratch_shapes=[pltpu.VMEM((hp, D, D), F32)],
        compiler_params=_params(("parallel", "parallel", "arbitrary")),
        name="hgrn2")


def _norm_rope(x, gain, cos, sin_signed, scale):
    sq = x * x
    hi = sq.astype(BF16)
    lo = (sq - hi.astype(F32)).astype(BF16)
    ones = jnp.ones((2 * HEAD_DIM, HEAD_DIM), BF16)
    ms = jnp.dot(jnp.concatenate([hi, lo], axis=1), ones,
                 preferred_element_type=F32) * (1.0 / HEAD_DIM)
    xn = x * lax.rsqrt(ms + NORM_EPS) * gain
    return (xn * cos + pltpu.roll(xn, HEAD_DIM // 2, 1) * sin_signed) * scale


def _flash_kernel(zq_ref, zk_ref, v_ref, cos_ref, sin_ref, qg_ref, kg_ref, lp_ref, g_ref, o_ref,
                  q_s, k_s, m_s, l_s, acc_s, a_s, p_s, *, blk, seq, heads, lam_init):
    i = pl.program_id(2)
    D = HEAD_DIM
    lanes = blk // D
    maps = tuple(range(2 * heads))

    @pl.when(i == 0)
    def _():
        def prep_keys(r, carry):
            rows = pl.ds(pl.multiple_of(r * blk, blk), blk)
            for e in maps:
                k_s[rows, e * D:(e + 1) * D] = _norm_rope(
                    zk_ref[rows, e * D:(e + 1) * D].astype(F32), kg_ref[...],
                    cos_ref[rows, :], sin_ref[rows, :], 1.0).astype(k_s.dtype)
            return carry

        lax.fori_loop(0, seq // blk, prep_keys, 0)

    q_rows = pl.ds(pl.multiple_of(i * blk, blk), blk)
    for e in maps:
        q_s[:, e * D:(e + 1) * D] = _norm_rope(
            zq_ref[:, e * D:(e + 1) * D].astype(F32), qg_ref[...],
            cos_ref[q_rows, :], sin_ref[q_rows, :], D ** -0.5 * LOG2_E).astype(q_s.dtype)

    m_s[...] = jnp.full_like(m_s, -jnp.inf)
    l_s[...] = jnp.zeros_like(l_s)
    acc_s[...] = jnp.zeros_like(acc_s)

    def scores(t, masked, which=maps):
        kv_rows = pl.ds(pl.multiple_of(t * blk, blk), blk)
        for e in which:
            s = lax.dot_general(q_s[:, e * D:(e + 1) * D], k_s[kv_rows, e * D:(e + 1) * D],
                                (((1,), (1,)), ((), ())), preferred_element_type=F32)
            if masked:
                row = lax.broadcasted_iota(jnp.int32, (blk, blk), 0)
                col = lax.broadcasted_iota(jnp.int32, (blk, blk), 1)
                s = jnp.where(col <= row, s, -jnp.inf)
            m_prev = m_s[e]
            m_new = jnp.maximum(m_prev, jnp.max(s, axis=-1, keepdims=True))
            alpha = jnp.exp2(m_prev - m_new)
            p = jnp.exp2(s - jnp.tile(m_new, (1, lanes)))
            l_s[e] = alpha * l_s[e] + jnp.sum(p, axis=-1, keepdims=True)
            m_s[e] = m_new
            a_s[e] = alpha
            p_s[e] = p.astype(BF16)

    def weighted_values(t, which=maps):
        kv_rows = pl.ds(pl.multiple_of(t * blk, blk), blk)
        for e in which:
            v = v_ref[kv_rows, (e // 2) * 2 * D:(e // 2 + 1) * 2 * D]
            acc_s[e] = jnp.tile(a_s[e], (1, 2)) * acc_s[e] + jnp.dot(
                p_s[e], v, preferred_element_type=F32)

    @pl.when(i == 0)
    def _():
        scores(0, True)

    @pl.when(i > 0)
    def _():
        scores(0, False)

        def pipelined(t, carry):
            for e in maps:
                weighted_values(t - 1, (e,))
                scores(t, False, (e,))
            return carry

        lax.fori_loop(1, i, pipelined, 0)
        weighted_values(i - 1)
        scores(i, True)

    weighted_values(i)

    lp = lp_ref[...].astype(F32)
    lam = (jnp.exp(jnp.sum(lp[0:1] * lp[1:2], axis=-1, keepdims=True))
           - jnp.exp(jnp.sum(lp[2:3] * lp[3:4], axis=-1, keepdims=True)) + lam_init)
    for h in range(heads):
        o = (acc_s[2 * h] * jnp.tile(1.0 / l_s[2 * h], (1, 2))
             - lam * (acc_s[2 * h + 1] * jnp.tile(1.0 / l_s[2 * h + 1], (1, 2))))
        ms = jnp.mean(o * o, axis=-1, keepdims=True)
        o_ref[:, h * 2 * D:(h + 1) * 2 * D] = (
            o * lax.rsqrt(ms + NORM_EPS) * g_ref[...] * (1.0 - lam_init)).astype(o_ref.dtype)


def _flash(z, cos, sin_signed, q_gain, k_gain, lam_params, subln_gain, *, layer, batch, seq, heads,
           col0, blk=512, heads_per_step=2, cast=None):
    nblk = seq // blk
    D = HEAD_DIM
    hp = heads_per_step
    width = hp * 2 * D
    groups = heads // hp
    lam_init = 0.8 - 0.6 * math.exp(-0.3 * layer)
    cb = col0 // width
    qspec = pl.BlockSpec((blk, width), lambda b, h, i: (b * nblk + i, cb + h))
    kspec = pl.BlockSpec((seq, width), lambda b, h, i: (b, cb + groups + h))
    vspec = pl.BlockSpec((seq, width), lambda b, h, i: (b, cb + 2 * groups + h))
    table = pl.BlockSpec((seq, D), lambda b, h, i: (0, 0))
    vec = pl.BlockSpec((1, D), lambda b, h, i: (0, 0))
    nmap = 2 * hp
    return _call_with_cast(
        functools.partial(_flash_kernel, blk=blk, seq=seq, heads=hp, lam_init=lam_init), cast,
        grid=(batch, groups, nblk),
        in_specs=[qspec, kspec, vspec, table, table, vec, vec,
                  pl.BlockSpec((4, D), lambda b, h, i: (0, 0)),
                  pl.BlockSpec((1, 2 * D), lambda b, h, i: (0, 0))],
        out_specs=pl.BlockSpec((blk, width), lambda b, h, i: (b * nblk + i, h)),
        out_shape=jax.ShapeDtypeStruct((batch * seq, heads * 2 * D), BF16),
        args=(z, z, z, cos, sin_signed, q_gain.reshape(1, D), k_gain.reshape(1, D), lam_params,
              subln_gain.reshape(1, 2 * D)),
        scratch_shapes=[pltpu.VMEM((blk, width), BF16), pltpu.VMEM((seq, width), BF16),
                        pltpu.VMEM((nmap, blk, D), F32), pltpu.VMEM((nmap, blk, D), F32),
                        pltpu.VMEM((nmap, blk, 2 * D), F32), pltpu.VMEM((nmap, blk, D), F32),
                        pltpu.VMEM((nmap, blk, blk), BF16)],
        compiler_params=_params(("parallel", "parallel", "arbitrary")),
        name="diff_flash")


def kernel(x, ln1_gain, w_in, conv_w, conv_b, lru_w_r, lru_b_r, lru_w_i, lru_b_i, lru_lambda, lru_norm_gain, hgrn_lower_bounds, hgrn_norm_gain, q_norm_gain, k_norm_gain, diff_lambda, diff_subln_gain, w_out, ln2_gain, w_ff1, w_ff2):
    batch, seq, d_model = x.shape
    depth = w_in.shape[0]
    D = HEAD_DIM
    lru_width = conv_w.shape[-1]
    key_width = hgrn_lower_bounds.shape[-1]
    in_width = w_in.shape[-1]
    mix_width = w_out.shape[1]
    diff_width = in_width - 2 * key_width - 2 * mix_width
    val_width = mix_width - lru_width - diff_width
    assert q_norm_gain.shape[-1] == D and hgrn_norm_gain.shape[-1] == D
    assert lru_w_r.shape[-1] == D and key_width == val_width
    hgrn_heads = key_width // D
    diff_heads = diff_width // (2 * D)
    off_b = 2 * lru_width
    off_c = off_b + 2 * key_width + 2 * val_width

    inv = 1.0 / (ROPE_THETA ** (jnp.arange(0, D, 2, dtype=F32) / D))
    ang = jnp.arange(seq, dtype=F32)[:, None] * inv[None, :]
    ang = jnp.concatenate([ang, ang], axis=-1)
    cos = jnp.cos(ang)
    sign = jnp.concatenate([-jnp.ones((D // 2,), F32), jnp.ones((D // 2,), F32)])
    sin_signed = jnp.sin(ang) * sign[None, :]

    xf = x.reshape(batch * seq, d_model)
    xg, ss = _rmsnorm(xf, ln1_gain[0]), None
    w_in_b = w_in[0].astype(BF16)
    for l in range(depth):
        z, w_out_b = _matmul([xg], w_in_b, row_ss=ss, out_dtype=BF16, casts=[(w_out, l)])

        lru_out = _lru(z, conv_w[l], conv_b[l], lru_w_r[l], lru_b_r[l], lru_w_i[l], lru_b_i[l],
                       lru_lambda[l], lru_norm_gain[l], batch=batch, seq=seq,
                       cast=(w_in, l + 1) if l + 1 < depth else None)
        ya, w_in_b = lru_out if l + 1 < depth else (lru_out, None)
        yb, w_ff2_b = _hgrn(z, hgrn_lower_bounds, hgrn_norm_gain[l], layer=l, batch=batch, seq=seq,
                            heads=hgrn_heads, col0=off_b, cast=(w_ff2, l))
        yc, w_ff1_b = _flash(z, cos, sin_signed, q_norm_gain[l], k_norm_gain[l], diff_lambda[l],
                             diff_subln_gain[l], layer=l, batch=batch, seq=seq, heads=diff_heads,
                             col0=off_c, cast=(w_ff1, l))

        xf, xg, ss = _matmul([ya, yb, yc], w_out_b, mode="residual", residual=xf,
                             next_gain=ln2_gain[l], vmem_limit=VMEM_LIMIT_WIDE)
        u = _matmul([xg], w_ff1_b, mode="relu2", row_ss=ss, out_dtype=BF16)
        if l + 1 < depth:
            xf, xg, ss = _matmul([u], w_ff2_b, mode="residual", residual=xf,
                                 next_gain=ln1_gain[l + 1], vmem_limit=VMEM_LIMIT_WIDE)
        else:
            xf = _matmul([u], w_ff2_b, mode="residual", residual=xf)
    return xf.reshape(batch, seq, d_model)
```

```python
import functools
import math

import jax
import jax.numpy as jnp
from jax import lax
from jax.experimental import pallas as pl
from jax.experimental.pallas import tpu as pltpu

F32 = jnp.float32
BF16 = jnp.bfloat16

NORM_EPS = 1e-6
LRU_C = 8.0
ROPE_THETA = 10000.0
CONV_WIDTH = 4
LOG2_E = 1.4426950408889634
HEAD_DIM = 128
HGRN_CHUNK = 128
SUBLANES = 8
HGRN_SUB = SUBLANES
VMEM_LIMIT = 56 * 1024 * 1024
VMEM_LIMIT_WIDE = 62 * 1024 * 1024


def _params(semantics, vmem_limit=VMEM_LIMIT):
    return pltpu.CompilerParams(dimension_semantics=semantics, vmem_limit_bytes=vmem_limit)


def _sigmoid(x):
    return 0.5 * jnp.tanh(0.5 * x) + 0.5


def _rmsnorm_kernel(x_ref, g_ref, o_ref):
    x = x_ref[...]
    ms = jnp.mean(x * x, axis=-1, keepdims=True)
    o_ref[...] = (x * lax.rsqrt(ms + NORM_EPS) * g_ref[...]).astype(o_ref.dtype)


def _rmsnorm(x, gain, tm=256):
    n, d = x.shape
    return pl.pallas_call(
        _rmsnorm_kernel,
        grid=(n // tm,),
        in_specs=[pl.BlockSpec((tm, d), lambda i: (i, 0)),
                  pl.BlockSpec((1, d), lambda i: (0, 0))],
        out_specs=pl.BlockSpec((tm, d), lambda i: (i, 0)),
        out_shape=jax.ShapeDtypeStruct((n, d), BF16),
        compiler_params=_params(("parallel",)),
        name="rmsnorm",
    )(x, gain.reshape(1, d))


LANES = 128


def _row_factor(ss_ref, width, tn):
    rstd = lax.rsqrt(ss_ref[...] * (1.0 / width) + NORM_EPS)
    return jnp.tile(rstd, (1, tn // LANES))


def _finish(acc, j, *, mode, r_ref, g_ref, o_ref, xg_ref, ss_ref):
    if mode == "relu2":
        acc = jnp.square(jnp.maximum(acc, 0.0))
    elif mode == "residual":
        acc = acc + r_ref[...]
    o_ref[...] = acc.astype(o_ref.dtype)
    if xg_ref is not None:
        xg_ref[...] = (acc * g_ref[...]).astype(xg_ref.dtype)

        @pl.when(j == 0)
        def _():
            ss_ref[...] = jnp.zeros_like(ss_ref)

        ss_ref[...] += jnp.sum(acc * acc, axis=-1, keepdims=True)


def _unpack(refs, n_a, row_scale, mode, emit_norm, n_cast):
    it = iter(refs)
    a_refs = [next(it) for _ in range(n_a)]
    w_ref = next(it)
    ss_in = next(it) if row_scale else None
    r_ref = next(it) if mode == "residual" else None
    g_ref = next(it) if emit_norm else None
    cast_src = [next(it) for _ in range(n_cast)]
    o_ref = next(it)
    xg_ref = next(it) if emit_norm else None
    ss_ref = next(it) if emit_norm else None
    for src_ref in cast_src:
        dst_ref = next(it)
        dst_ref[...] = src_ref[...].astype(dst_ref.dtype)
    return a_refs, w_ref, ss_in, r_ref, g_ref, o_ref, xg_ref, ss_ref, list(it)


def _mm_fullk_kernel(*refs, splits, mode, row_scale, emit_norm, n_cast):
    a_refs, w_ref, ss_in, r_ref, g_ref, o_ref, xg_ref, ss_ref, _ = _unpack(
        refs, len(splits), row_scale, mode, emit_norm, n_cast)
    acc, off = None, 0
    for a_ref, kp in zip(a_refs, splits):
        part = jnp.dot(a_ref[...], w_ref[off:off + kp, :], preferred_element_type=F32)
        acc = part if acc is None else acc + part
        off += kp
    if row_scale:
        acc = acc * _row_factor(ss_in, off, o_ref.shape[-1])
    _finish(acc, pl.program_id(1), mode=mode, r_ref=r_ref, g_ref=g_ref, o_ref=o_ref,
            xg_ref=xg_ref, ss_ref=ss_ref)


def _mm_kloop_kernel(*refs, nk, emit_norm):
    (a_ref,), w_ref, _, r_ref, g_ref, o_ref, xg_ref, ss_ref, _ = _unpack(
        refs, 1, False, "residual", emit_norm, 0)
    k = pl.program_id(2)

    @pl.when(k == 0)
    def _():
        o_ref[...] = r_ref[...] + jnp.dot(a_ref[...], w_ref[...], preferred_element_type=F32)

    @pl.when(k > 0)
    def _():
        o_ref[...] += jnp.dot(a_ref[...], w_ref[...], preferred_element_type=F32)

    if emit_norm:
        @pl.when(k == nk - 1)
        def _():
            acc = o_ref[...]
            xg_ref[...] = (acc * g_ref[...]).astype(xg_ref.dtype)

            @pl.when(pl.program_id(1) == 0)
            def _():
                ss_ref[...] = jnp.zeros_like(ss_ref)

            ss_ref[...] += jnp.sum(acc * acc, axis=-1, keepdims=True)


def _matmul(a_parts, w, *, mode="plain", residual=None, row_ss=None, next_gain=None, casts=(),
            out_dtype=F32, tm=1024, tn=1024, tk=4096, vmem_limit=VMEM_LIMIT):
    m = a_parts[0].shape[0]
    splits = tuple(p.shape[1] for p in a_parts)
    kdim = sum(splits)
    n = w.shape[-1]
    fullk = kdim <= 4096
    row_scale = row_ss is not None
    emit_norm = next_gain is not None
    tm, tn = min(tm, m), min(tn, n)

    if fullk:
        grid = (m // tm, n // tn)
        tile = lambda i, j: (i, j)
        rows = lambda i, j: (i, 0)
        cols = lambda i, j: (0, j)
        in_specs = [pl.BlockSpec((tm, kp), rows) for kp in splits]
        in_specs.append(pl.BlockSpec((kdim, tn), cols))
        body = functools.partial(_mm_fullk_kernel, splits=splits, mode=mode, row_scale=row_scale,
                                 emit_norm=emit_norm, n_cast=len(casts))
        scratch = []
        semantics = ("parallel", "arbitrary")
    else:
        assert len(a_parts) == 1 and not row_scale and not casts
        assert mode == "residual" and out_dtype == F32
        tk = min(tk, kdim)
        nk = kdim // tk
        grid = (m // tm, n // tn, nk)
        tile = lambda i, j, k: (i, j)
        rows = lambda i, j, k: (i, 0)
        cols = lambda i, j, k: (0, j)
        in_specs = [pl.BlockSpec((tm, tk), lambda i, j, k: (i, k)),
                    pl.BlockSpec((tk, tn), lambda i, j, k: (k, j))]
        body = functools.partial(_mm_kloop_kernel, nk=nk, emit_norm=emit_norm)
        scratch = []
        semantics = ("parallel", "arbitrary", "arbitrary")

    args = list(a_parts) + [w]
    if row_scale:
        in_specs.append(pl.BlockSpec((tm, LANES), rows))
        args.append(row_ss)
    if mode == "residual":
        in_specs.append(pl.BlockSpec((tm, tn), tile))
        args.append(residual)
    out_specs = [pl.BlockSpec((tm, tn), tile)]
    out_shape = [jax.ShapeDtypeStruct((m, n), out_dtype)]
    if emit_norm:
        in_specs.append(pl.BlockSpec((1, tn), cols))
        args.append(next_gain.reshape(1, n))
        out_specs += [pl.BlockSpec((tm, tn), tile), pl.BlockSpec((tm, LANES), rows)]
        out_shape += [jax.ShapeDtypeStruct((m, n), BF16), jax.ShapeDtypeStruct((m, LANES), F32)]
    for w_f32, layer in casts:
        _, wr, wc = w_f32.shape
        col_blocks = max(c for c in range(1, grid[1] + 1)
                         if grid[1] % c == 0 and wc % (c * LANES) == 0)
        br, bc = wr // grid[0], wc // col_blocks
        assert wr % grid[0] == 0 and br % 8 == 0
        col = lambda j, last=col_blocks - 1: jnp.minimum(j, last)
        in_specs.append(pl.BlockSpec((None, br, bc),
                                     lambda i, j, layer=layer, col=col: (layer, i, col(j))))
        args.append(w_f32)
        out_specs.append(pl.BlockSpec((br, bc), lambda i, j, col=col: (i, col(j))))
        out_shape.append(jax.ShapeDtypeStruct((wr, wc), BF16))
    outs = pl.pallas_call(
        body, grid=grid, in_specs=in_specs, out_specs=out_specs, out_shape=out_shape,
        scratch_shapes=scratch, compiler_params=_params(semantics, vmem_limit),
        name=("matmul_fullk_" if fullk else "matmul_kloop_") + mode,
    )(*args)
    return outs if len(outs) > 1 else outs[0]


def _cast_job(w_f32, layer, grid):
    _, wr, wc = w_f32.shape
    steps = math.prod(grid)
    br = wr // steps
    assert wr % steps == 0 and br % 16 == 0

    def step(*idx):
        flat = idx[0]
        for n, i in zip(grid[1:], idx[1:]):
            flat = flat * n + i
        return flat

    return (pl.BlockSpec((None, br, wc), lambda *idx: (layer, step(*idx), 0)),
            pl.BlockSpec((br, wc), lambda *idx: (step(*idx), 0)),
            jax.ShapeDtypeStruct((wr, wc), BF16))


def _hosting_casts(body, n_in, n_cast):
    def hosted(*refs):
        ins, srcs = refs[:n_in], refs[n_in:n_in + n_cast]
        out, dsts = refs[n_in + n_cast], refs[n_in + n_cast + 1:n_in + 2 * n_cast + 1]
        for src, dst in zip(srcs, dsts):
            dst[...] = src[...].astype(dst.dtype)
        body(*ins, out, *refs[n_in + 2 * n_cast + 1:])
    return hosted


def _call_with_casts(body, casts, *, grid, in_specs, out_specs, out_shape, args, **kwargs):
    if not casts:
        return pl.pallas_call(body, grid=grid, in_specs=in_specs, out_specs=out_specs,
                              out_shape=out_shape, **kwargs)(*args)
    jobs = [_cast_job(w, layer, grid) for w, layer in casts]
    return pl.pallas_call(
        _hosting_casts(body, len(in_specs), len(jobs)), grid=grid,
        in_specs=in_specs + [j[0] for j in jobs],
        out_specs=[out_specs] + [j[1] for j in jobs],
        out_shape=[out_shape] + [j[2] for j in jobs], **kwargs)(*args, *[w for w, _ in casts])


def _tile_roll(x, shift):
    rows, d = x.shape
    x3 = x.reshape(rows // SUBLANES, SUBLANES, d)
    return pltpu.roll(x3, shift, 1).reshape(rows, d)


def _lru_kernel(zg_ref, zx_ref, cw_ref, cb_ref, wr_ref, br_ref, wi_ref, bi_ref, lam_ref, ng_ref,
                o_ref, xprev_ref, hprev_ref, a_s, u_s, *, rows, heads):
    s = pl.program_id(1)
    width = heads * HEAD_DIM
    tiles = rows // SUBLANES

    @pl.when(s == 0)
    def _():
        xprev_ref[...] = jnp.zeros_like(xprev_ref)
        hprev_ref[...] = jnp.zeros_like(hprev_ref)

    assert zx_ref.dtype == BF16
    xzb = zx_ref[...]
    xz = xzb.astype(F32)
    prev8 = xprev_ref[...]
    row8 = lax.broadcasted_iota(jnp.int32, (SUBLANES, width), 0)
    lag = (lax.broadcasted_iota(jnp.int32, (rows, rows), 0)
           - lax.broadcasted_iota(jnp.int32, (rows, rows), 1))
    cw = cw_ref[...]
    xa = cw[CONV_WIDTH - 1:CONV_WIDTH] * xz + cb_ref[...]
    for d in range(1, CONV_WIDTH):
        moved = jnp.dot((lag == d).astype(BF16), xzb, preferred_element_type=F32)
        head = jnp.where(row8 < d, pltpu.roll(prev8, d, 0), moved[0:SUBLANES])
        shifted = jnp.concatenate([head, moved[SUBLANES:]], axis=0)
        xa = xa + cw[CONV_WIDTH - 1 - d:CONV_WIDTH - d] * shifted
    xprev_ref[...] = xz[rows - SUBLANES:rows]

    xab = xa.astype(BF16)
    r_parts, i_parts = [], []
    for h in range(heads):
        xh = xab[:, h * HEAD_DIM:(h + 1) * HEAD_DIM]
        r_parts.append(jnp.dot(xh, wr_ref[h], preferred_element_type=F32))
        i_parts.append(jnp.dot(xh, wi_ref[h], preferred_element_type=F32))
    r = _sigmoid(jnp.concatenate(r_parts, axis=1) + br_ref[...])
    gate_i = _sigmoid(jnp.concatenate(i_parts, axis=1) + bi_ref[...])

    lam = lam_ref[...]
    softplus_neg_lam = jnp.maximum(-lam, 0.0) + jnp.log1p(jnp.exp(-jnp.abs(lam)))
    a = jnp.exp((-LRU_C) * r * softplus_neg_lam)
    gap = 1.0 - a * a
    mult = gap * lax.rsqrt(jnp.maximum(gap, 1e-30))
    row = lax.broadcasted_iota(jnp.int32, (rows, width), 0)
    mult = jnp.where(jnp.logical_and(row == 0, s == 0), 1.0, mult)
    u = mult * (gate_i * xa)

    sub = jnp.bitwise_and(row, SUBLANES - 1)
    d = 1
    while d < SUBLANES:
        a_sh = _tile_roll(a, d)
        u_sh = _tile_roll(u, d)
        valid = sub >= d
        u = jnp.where(valid, u + a * u_sh, u)
        a = jnp.where(valid, a * a_sh, a)
        d *= 2
    a_s[...] = a
    u_s[...] = u
    carry = hprev_ref[0:1, :]
    for t in range(tiles):
        grp = slice(t * SUBLANES, (t + 1) * SUBLANES)
        a_g, u_g = a_s[grp, :], u_s[grp, :]
        u_s[grp, :] = u_g + a_g * carry
        carry = u_g[SUBLANES - 1:SUBLANES] + a_g[SUBLANES - 1:SUBLANES] * carry
    hprev_ref[...] = jnp.broadcast_to(carry, hprev_ref.shape)
    h = u_s[...]

    zg = zg_ref[...].astype(F32)
    gelu = 0.5 * zg * (1.0 + jnp.tanh(math.sqrt(2.0 / math.pi) * (zg + 0.044715 * (zg * zg * zg))))
    y = h * gelu
    ms = jnp.mean(y * y, axis=-1, keepdims=True)
    o_ref[...] = (y * lax.rsqrt(ms + NORM_EPS) * ng_ref[...]).astype(o_ref.dtype)


def _lru(z, conv_w, conv_b, w_r, b_r, w_i, b_i, lam, norm_gain, *, batch, seq, rows=256,
         casts=()):
    heads = w_r.shape[0]
    width = heads * HEAD_DIM
    nblk = seq // rows
    row_spec = lambda c: pl.BlockSpec((rows, width), lambda b, s, c=c: (b * nblk + s, c))
    vec = pl.BlockSpec((1, width), lambda b, s: (0, 0))
    wspec = pl.BlockSpec((heads, HEAD_DIM, HEAD_DIM), lambda b, s: (0, 0, 0))
    return _call_with_casts(
        functools.partial(_lru_kernel, rows=rows, heads=heads), casts,
        grid=(batch, nblk),
        in_specs=[row_spec(0), row_spec(1),
                  pl.BlockSpec((CONV_WIDTH, width), lambda b, s: (0, 0)), vec,
                  wspec, vec, wspec, vec, vec, vec],
        out_specs=pl.BlockSpec((rows, width), lambda b, s: (b * nblk + s, 0)),
        out_shape=jax.ShapeDtypeStruct((batch * seq, width), BF16),
        args=(z, z, conv_w, conv_b.reshape(1, width), w_r.astype(BF16), b_r.reshape(1, width),
              w_i.astype(BF16), b_i.reshape(1, width), lam.reshape(1, width),
              norm_gain.reshape(1, width)),
        scratch_shapes=[pltpu.VMEM((SUBLANES, width), F32), pltpu.VMEM((SUBLANES, width), F32),
                        pltpu.VMEM((rows, width), F32), pltpu.VMEM((rows, width), F32)],
        compiler_params=_params(("parallel", "arbitrary")),
        name="rg_lru")


def _split3(x):
    hi = x.astype(BF16)
    r1 = x - hi.astype(F32)
    mid = r1.astype(BF16)
    lo = (r1 - mid.astype(F32)).astype(BF16)
    return hi, mid, lo


def _hgrn_kernel(zq_ref, zf_ref, zv_ref, zg_ref, lbp_ref, ng_ref, o_ref, st_s, *, rows, layer,
                 heads):
    s = pl.program_id(2)
    C, c = HGRN_CHUNK, HGRN_SUB
    D = HEAD_DIM

    @pl.when(s == 0)
    def _():
        st_s[...] = jnp.zeros_like(st_s)

    lbp = lbp_ref[...].astype(F32)
    e = jnp.exp(lbp - jnp.max(lbp, axis=0, keepdims=True))
    sm = e / jnp.sum(e, axis=0, keepdims=True)
    lb_all = jnp.zeros((1, heads * D), F32)
    for r_ in range(1, layer + 1):
        lb_all = lb_all + sm[r_:r_ + 1]

    row = lax.broadcasted_iota(jnp.int32, (C, C), 0)
    col = lax.broadcasted_iota(jnp.int32, (C, C), 1)
    tri = (row >= col).astype(BF16)
    differ = jnp.bitwise_xor(row, col)
    levels = []
    m = C // 2
    while m >= c:
        levels.append((m, (row > col) & (differ >= m) & (differ < 2 * m)))
        m //= 2
    diag_masks = [(differ < c) & (row - col == dlt) for dlt in range(c)]

    for ch, h in [(ch, h) for ch in range(rows // C) for h in range(heads)]:
        sl = slice(ch * C, (ch + 1) * C)
        hc = slice(h * D, (h + 1) * D)
        lb = lb_all[:, hc]
        zq = zq_ref[sl, hc].astype(F32)
        zf = zf_ref[sl, hc].astype(F32)
        q = zq * _sigmoid(zq) * (D ** -0.5)
        half_tanh = 0.5 * jnp.tanh(0.5 * zf)
        f = lb + (1.0 - lb) * (0.5 + half_tanh)
        k = (1.0 - lb) * (0.5 - half_tanh)
        log_sig = jnp.minimum(zf, 0.0) - jnp.log(1.0 + jnp.exp(-jnp.abs(zf)))
        y = jnp.log1p(-lb) + log_sig
        if layer == 0:
            log_f = y
        else:
            la = jnp.log(lb)
            log_f = jnp.maximum(la, y) + jnp.log(1.0 + jnp.exp(-jnp.abs(la - y)))

        hi, mid, lo = _split3(log_f)
        g3 = jnp.dot(tri, jnp.concatenate([hi, mid, lo], axis=1), preferred_element_type=F32)
        G = g3[:, 0:D] + g3[:, D:2 * D] + g3[:, 2 * D:3 * D]
        vb = zv_ref[sl, hc].astype(BF16)
        g_last = G[C - 1:C]

        a = jnp.zeros((C, C), F32)
        dec = None
        for dlt in range(c):
            if dlt == 0:
                w = q * k
            else:
                f_sh = f if dlt == 1 else _tile_roll(f, dlt - 1)
                dec = f_sh if dlt == 1 else dec * f_sh
                w = q * _tile_roll(k, dlt) * dec
            a = jnp.where(diag_masks[dlt], jnp.sum(w, axis=-1, keepdims=True), a)

        for m, mask in levels:
            g_bnd = jnp.concatenate(
                [jnp.broadcast_to(G[p * 2 * m + m - 1:p * 2 * m + m], (2 * m, D))
                 for p in range(C // (2 * m))], axis=0)
            e_ref = jnp.exp(-jnp.abs(G - g_bnd))
            sc = lax.dot_general((q * e_ref).astype(BF16), (k * e_ref).astype(BF16),
                                 (((1,), (1,)), ((), ())), preferred_element_type=F32)
            a = jnp.where(mask, sc, a)

        st = st_s[h]
        qg = (q * jnp.exp(G)).astype(BF16)
        o = (jnp.dot(a.astype(BF16), vb, preferred_element_type=F32)
             + lax.dot_general(qg, st.astype(BF16), (((1,), (1,)), ((), ())),
                               preferred_element_type=F32))
        kd = (k * jnp.exp(g_last - G)).astype(BF16)
        st_s[h] = st * jnp.exp(g_last) + lax.dot_general(
            vb, kd, (((0,), (0,)), ((), ())), preferred_element_type=F32)

        ms = jnp.mean(o * o, axis=-1, keepdims=True)
        zg = zg_ref[sl, hc].astype(F32)
        o_ref[sl, hc] = (o * lax.rsqrt(ms + NORM_EPS) * ng_ref[...]
                         * (zg * _sigmoid(zg))).astype(o_ref.dtype)


def _hgrn(z, lower_bounds, norm_gain, *, layer, batch, seq, heads, col0, rows=512, heads_per_step=2,
          casts=()):
    nblk = seq // rows
    D = HEAD_DIM
    depth = lower_bounds.shape[0]
    hp = heads_per_step
    width = hp * D
    groups = heads // hp

    def zspec(section):
        off = col0 // width + section * groups
        return pl.BlockSpec((rows, width), lambda b, h, s, off=off: (b * nblk + s, off + h))

    return _call_with_casts(
        functools.partial(_hgrn_kernel, rows=rows, layer=layer, heads=hp), casts,
        grid=(batch, groups, nblk),
        in_specs=[zspec(0), zspec(1), zspec(2), zspec(3),
                  pl.BlockSpec((depth, width), lambda b, h, s: (0, h)),
                  pl.BlockSpec((1, D), lambda b, h, s: (0, 0))],
        out_specs=pl.BlockSpec((rows, width), lambda b, h, s: (b * nblk + s, h)),
        out_shape=jax.ShapeDtypeStruct((batch * seq, heads * D), BF16),
        args=(z, z, z, z, lower_bounds, norm_gain.reshape(1, D)),
        scratch_shapes=[pltpu.VMEM((hp, D, D), F32)],
        compiler_params=_params(("parallel", "parallel", "arbitrary")),
        name="hgrn2")


def _norm_rope(x, gain, cos, sin_signed, scale):
    sq = x * x
    hi = sq.astype(BF16)
    lo = (sq - hi.astype(F32)).astype(BF16)
    ones = jnp.ones((2 * HEAD_DIM, HEAD_DIM), BF16)
    ms = jnp.dot(jnp.concatenate([hi, lo], axis=1), ones,
                 preferred_element_type=F32) * (1.0 / HEAD_DIM)
    xn = x * lax.rsqrt(ms + NORM_EPS) * gain
    return (xn * cos + pltpu.roll(xn, HEAD_DIM // 2, 1) * sin_signed) * scale


def _flash_kernel(zq_ref, zk_ref, v_ref, cos_ref, sin_ref, qg_ref, kg_ref, lp_ref, g_ref, o_ref,
                  q_s, k_s, m_s, l_s, acc_s, a_s, p_s, *, blk, seq, heads, lam_init):
    j = pl.program_id(2)
    D = HEAD_DIM
    lanes = blk // D
    nmap = 2 * heads
    first, second = tuple(range(nmap)), tuple(range(nmap, 2 * nmap))
    both = first + second

    @pl.when(j == 0)
    def _():
        def prep_keys(r, carry):
            rows = pl.ds(pl.multiple_of(r * blk, blk), blk)
            for m in range(nmap):
                k_s[rows, m * D:(m + 1) * D] = _norm_rope(
                    zk_ref[rows, m * D:(m + 1) * D].astype(F32), kg_ref[...],
                    cos_ref[rows, :], sin_ref[rows, :], 1.0).astype(k_s.dtype)
            return carry

        lax.fori_loop(0, seq // blk, prep_keys, 0)

    q_rows = pl.ds(pl.multiple_of(j * 2 * blk, 2 * blk), 2 * blk)
    for m in range(nmap):
        q_s[:, m * D:(m + 1) * D] = _norm_rope(
            zq_ref[:, m * D:(m + 1) * D].astype(F32), qg_ref[...],
            cos_ref[q_rows, :], sin_ref[q_rows, :], D ** -0.5 * LOG2_E).astype(q_s.dtype)

    m_s[...] = jnp.full_like(m_s, -jnp.inf)
    l_s[...] = jnp.zeros_like(l_s)
    acc_s[...] = jnp.zeros_like(acc_s)

    def scores(t, e, masked):
        half, m = divmod(e, nmap)
        kv_rows = pl.ds(pl.multiple_of(t * blk, blk), blk)
        s = lax.dot_general(q_s[half * blk:(half + 1) * blk, m * D:(m + 1) * D],
                            k_s[kv_rows, m * D:(m + 1) * D],
                            (((1,), (1,)), ((), ())), preferred_element_type=F32)
        if masked:
            row = lax.broadcasted_iota(jnp.int32, (blk, blk), 0)
            col = lax.broadcasted_iota(jnp.int32, (blk, blk), 1)
            s = jnp.where(col <= row, s, -jnp.inf)
        m_prev = m_s[e]
        m_new = jnp.maximum(m_prev, jnp.max(s, axis=-1, keepdims=True))
        alpha = jnp.exp2(m_prev - m_new)
        p = jnp.exp2(s - jnp.tile(m_new, (1, lanes)))
        l_s[e] = alpha * l_s[e] + jnp.sum(p, axis=-1, keepdims=True)
        m_s[e] = m_new
        a_s[e] = alpha
        p_s[e] = p.astype(BF16)

    def weighted_values(t, e):
        head = (e % nmap) // 2
        kv_rows = pl.ds(pl.multiple_of(t * blk, blk), blk)
        acc_s[e] = jnp.tile(a_s[e], (1, 2)) * acc_s[e] + jnp.dot(
            p_s[e], v_ref[kv_rows, head * 2 * D:(head + 1) * 2 * D], preferred_element_type=F32)

    def stage(values=None, plain=None, diagonal=None):
        for e in both:
            if values is not None and e in values[1]:
                weighted_values(values[0], e)
            if plain is not None and e in plain[1]:
                scores(plain[0], e, False)
            if diagonal is not None and e in diagonal[1]:
                scores(diagonal[0], e, True)

    a = 2 * j

    @pl.when(j == 0)
    def _():
        stage(plain=(0, second), diagonal=(0, first))
        stage(values=(0, both), diagonal=(1, second))

    @pl.when(j > 0)
    def _():
        stage(plain=(0, both))

        def pipelined(t, carry):
            stage(values=(t - 1, both), plain=(t, both))
            return carry

        lax.fori_loop(1, a, pipelined, 0)
        stage(values=(a - 1, both), plain=(a, second), diagonal=(a, first))
        stage(values=(a, both), diagonal=(a + 1, second))

    stage(values=(a + 1, second))

    lp = lp_ref[...].astype(F32)
    lam = (jnp.exp(jnp.sum(lp[0:1] * lp[1:2], axis=-1, keepdims=True))
           - jnp.exp(jnp.sum(lp[2:3] * lp[3:4], axis=-1, keepdims=True)) + lam_init)
    for half in range(2):
        for h in range(heads):
            e = half * nmap + 2 * h
            o = (acc_s[e] * jnp.tile(1.0 / l_s[e], (1, 2))
                 - lam * (acc_s[e + 1] * jnp.tile(1.0 / l_s[e + 1], (1, 2))))
            ms = jnp.mean(o * o, axis=-1, keepdims=True)
            o_ref[half * blk:(half + 1) * blk, h * 2 * D:(h + 1) * 2 * D] = (
                o * lax.rsqrt(ms + NORM_EPS) * g_ref[...] * (1.0 - lam_init)).astype(o_ref.dtype)


def _flash(z, cos, sin_signed, q_gain, k_gain, lam_params, subln_gain, *, layer, batch, seq, heads,
           col0, blk=512, heads_per_step=2):
    npair = seq // (2 * blk)
    D = HEAD_DIM
    hp = heads_per_step
    width = hp * 2 * D
    groups = heads // hp
    lam_init = 0.8 - 0.6 * math.exp(-0.3 * layer)
    cb = col0 // width
    qspec = pl.BlockSpec((2 * blk, width), lambda b, h, j: (b * npair + j, cb + h))
    kspec = pl.BlockSpec((seq, width), lambda b, h, j: (b, cb + groups + h))
    vspec = pl.BlockSpec((seq, width), lambda b, h, j: (b, cb + 2 * groups + h))
    table = pl.BlockSpec((seq, D), lambda b, h, j: (0, 0))
    vec = pl.BlockSpec((1, D), lambda b, h, j: (0, 0))
    nstream = 4 * hp
    return pl.pallas_call(
        functools.partial(_flash_kernel, blk=blk, seq=seq, heads=hp, lam_init=lam_init),
        grid=(batch, groups, npair),
        in_specs=[qspec, kspec, vspec, table, table, vec, vec,
                  pl.BlockSpec((4, D), lambda b, h, j: (0, 0)),
                  pl.BlockSpec((1, 2 * D), lambda b, h, j: (0, 0))],
        out_specs=pl.BlockSpec((2 * blk, width), lambda b, h, j: (b * npair + j, h)),
        out_shape=jax.ShapeDtypeStruct((batch * seq, heads * 2 * D), BF16),
        scratch_shapes=[pltpu.VMEM((2 * blk, width), BF16), pltpu.VMEM((seq, width), BF16),
                        pltpu.VMEM((nstream, blk, D), F32), pltpu.VMEM((nstream, blk, D), F32),
                        pltpu.VMEM((nstream, blk, 2 * D), F32), pltpu.VMEM((nstream, blk, D), F32),
                        pltpu.VMEM((nstream, blk, blk), BF16)],
        compiler_params=_params(("parallel", "parallel", "arbitrary")),
        name="diff_flash",
    )(z, z, z, cos, sin_signed, q_gain.reshape(1, D), k_gain.reshape(1, D), lam_params,
      subln_gain.reshape(1, 2 * D))


def kernel(x, ln1_gain, w_in, conv_w, conv_b, lru_w_r, lru_b_r, lru_w_i, lru_b_i, lru_lambda, lru_norm_gain, hgrn_lower_bounds, hgrn_norm_gain, q_norm_gain, k_norm_gain, diff_lambda, diff_subln_gain, w_out, ln2_gain, w_ff1, w_ff2):
    batch, seq, d_model = x.shape
    depth = w_in.shape[0]
    D = HEAD_DIM
    lru_width = conv_w.shape[-1]
    key_width = hgrn_lower_bounds.shape[-1]
    in_width = w_in.shape[-1]
    mix_width = w_out.shape[1]
    diff_width = in_width - 2 * key_width - 2 * mix_width
    val_width = mix_width - lru_width - diff_width
    assert q_norm_gain.shape[-1] == D and hgrn_norm_gain.shape[-1] == D
    assert lru_w_r.shape[-1] == D and key_width == val_width
    hgrn_heads = key_width // D
    diff_heads = diff_width // (2 * D)
    off_b = 2 * lru_width
    off_c = off_b + 2 * key_width + 2 * val_width

    inv = 1.0 / (ROPE_THETA ** (jnp.arange(0, D, 2, dtype=F32) / D))
    ang = jnp.arange(seq, dtype=F32)[:, None] * inv[None, :]
    ang = jnp.concatenate([ang, ang], axis=-1)
    cos = jnp.cos(ang)
    sign = jnp.concatenate([-jnp.ones((D // 2,), F32), jnp.ones((D // 2,), F32)])
    sin_signed = jnp.sin(ang) * sign[None, :]

    xf = x.reshape(batch * seq, d_model)
    xg, ss = _rmsnorm(xf, ln1_gain[0]), None
    w_in_b = w_in[0].astype(BF16)
    for l in range(depth):
        z, w_out_b = _matmul([xg], w_in_b, row_ss=ss, out_dtype=BF16, casts=[(w_out, l)])

        lru_out = _lru(z, conv_w[l], conv_b[l], lru_w_r[l], lru_b_r[l], lru_w_i[l], lru_b_i[l],
                       lru_lambda[l], lru_norm_gain[l], batch=batch, seq=seq,
                       casts=[(w_in, l + 1)] if l + 1 < depth else [])
        ya, w_in_b = lru_out if l + 1 < depth else (lru_out, None)
        yb, w_ff1_b, w_ff2_b = _hgrn(z, hgrn_lower_bounds, hgrn_norm_gain[l], layer=l, batch=batch,
                                     seq=seq, heads=hgrn_heads, col0=off_b,
                                     casts=[(w_ff1, l), (w_ff2, l)])
        yc = _flash(z, cos, sin_signed, q_norm_gain[l], k_norm_gain[l], diff_lambda[l],
                    diff_subln_gain[l], layer=l, batch=batch, seq=seq, heads=diff_heads, col0=off_c)

        xf, xg, ss = _matmul([ya, yb, yc], w_out_b, mode="residual", residual=xf,
                             next_gain=ln2_gain[l], vmem_limit=VMEM_LIMIT_WIDE)
        u = _matmul([xg], w_ff1_b, mode="relu2", row_ss=ss, out_dtype=BF16)
        if l + 1 < depth:
            xf, xg, ss = _matmul([u], w_ff2_b, mode="residual", residual=xf,
                                 next_gain=ln1_gain[l + 1], vmem_limit=VMEM_LIMIT_WIDE)
        else:
            xf = _matmul([u], w_ff2_b, mode="residual", residual=xf)
    return xf.reshape(batch, seq, d_model)
```
